```python
import math
import jax, jax.numpy as jnp
from jax import lax
import numpy as np

D_MODEL = 1024
BATCH = 8
SEQ = 8192
DEPTH = 2

HEAD_DIM = 64
N_SLOTS = D_MODEL // HEAD_DIM
SB_HEADS = N_SLOTS // 2
SB_WIDTH = SB_HEADS * HEAD_DIM
DIFF_HEADS = N_SLOTS // 4
DIFF_VDIM = 2 * HEAD_DIM
DIFF_WIDTH = DIFF_HEADS * DIFF_VDIM
DIFF_QK_WIDTH = DIFF_HEADS * 2 * HEAD_DIM
EVEN_IN = 3 * SB_WIDTH + 2 * DIFF_QK_WIDTH + DIFF_WIDTH
DIL_HEADS = N_SLOTS
ODD_IN = 3 * DIL_HEADS * HEAD_DIM
DIL_BRANCHES = ((128, 1), (512, 4), (2048, 16))
W_MAX = 2048
Q_BLOCK = 128
NUM_BUCKETS = 32
MAX_DISTANCE = 128
D_FF = 4 * D_MODEL
PLE_DIM = 256
N_EVEN = (DEPTH + 1) // 2
N_ODD = DEPTH // 2
NORM_EPS = 1e-6
SUBLN_EPS = 1e-5
NEG = -1e30

kernel_name = "hybrid_stickbreak_diff_dilated_trunk"


def rmsnorm(x, g, eps=NORM_EPS):
    x32 = x.astype(jnp.float32)
    y = x32 * lax.rsqrt(jnp.mean(x32 * x32, axis=-1, keepdims=True) + eps) * g.astype(jnp.float32)
    return y.astype(x.dtype)


def t5_bucket(dist):
    n = jnp.maximum(dist, 0)
    max_exact = NUM_BUCKETS // 2
    nf = jnp.maximum(n, 1).astype(jnp.float32)
    large = max_exact + (jnp.log(nf / max_exact) / math.log(MAX_DISTANCE / max_exact)
                         * (NUM_BUCKETS - max_exact)).astype(jnp.int32)
    large = jnp.minimum(large, NUM_BUCKETS - 1)
    return jnp.where(n < max_exact, n, large)


def even_mixer(h, w_in, w_out, lq1, lk1, lq2, lk2, subln_g, t5_table, layer_idx):
    B, S, _ = h.shape
    f32 = jnp.float32
    proj = h @ w_in
    cuts = np.cumsum([SB_WIDTH, SB_WIDTH, SB_WIDTH, DIFF_QK_WIDTH, DIFF_QK_WIDTH]).tolist()
    qa, ka, va, qb, kb, vb = jnp.split(proj, cuts, axis=-1)
    qa = qa.reshape(B, S, SB_HEADS, HEAD_DIM).astype(f32)
    ka = ka.reshape(B, S, SB_HEADS, HEAD_DIM).astype(f32)
    va = va.reshape(B, S, SB_HEADS, HEAD_DIM).astype(f32)
    qb = qb.reshape(B, S, DIFF_HEADS, 2, HEAD_DIM).astype(f32)
    kb = kb.reshape(B, S, DIFF_HEADS, 2, HEAD_DIM).astype(f32)
    vb = vb.reshape(B, S, DIFF_HEADS, DIFF_VDIM).astype(f32)
    scale = HEAD_DIM ** -0.5
    lambda_init = 0.8 - 0.6 * math.exp(-0.3 * layer_idx)
    lam = (jnp.exp(jnp.sum(lq1.astype(f32) * lk1.astype(f32)))
           - jnp.exp(jnp.sum(lq2.astype(f32) * lk2.astype(f32))) + lambda_init)
    table = t5_table.astype(f32)
    kpos = jnp.arange(S)

    def block(i):
        t0 = i * Q_BLOCK
        qpos = t0 + jnp.arange(Q_BLOCK)
        rel = qpos[:, None] - kpos[None, :]
        qa_b = lax.dynamic_slice_in_dim(qa, t0, Q_BLOCK, axis=1)
        z = jnp.einsum('bqhd,bkhd->bhqk', qa_b, ka) * scale
        strict = rel > 0
        log_keep = jnp.where(strict, -jax.nn.softplus(z), 0.0)
        later = lax.cumsum(log_keep, axis=3, reverse=True) - log_keep
        w_sb = jnp.where(strict, jnp.exp(jax.nn.log_sigmoid(z) + later), 0.0)
        o_sb = jnp.einsum('bhqk,bkhd->bqhd', w_sb, va)
        qb_b = lax.dynamic_slice_in_dim(qb, t0, Q_BLOCK, axis=1)
        sc = jnp.einsum('bqhmd,bkhmd->bhmqk', qb_b, kb) * scale
        bias = table[t5_bucket(rel)]
        bias = bias[..., SB_HEADS:].reshape(Q_BLOCK, S, DIFF_HEADS, 2).transpose(2, 3, 0, 1)
        sc = jnp.where(rel >= 0, sc + bias, NEG)
        prob = jax.nn.softmax(sc, axis=-1)
        attn = prob[:, :, 0] - lam * prob[:, :, 1]
        o_d = jnp.einsum('bhqk,bkhd->bqhd', attn, vb)
        o_d = rmsnorm(o_d, subln_g, SUBLN_EPS) * (1.0 - lambda_init)
        return jnp.concatenate([o_sb.reshape(B, Q_BLOCK, SB_WIDTH),
                                o_d.reshape(B, Q_BLOCK, DIFF_WIDTH)], axis=-1)

    o = lax.map(block, jnp.arange(S // Q_BLOCK))
    o = o.transpose(1, 0, 2, 3).reshape(B, S, SB_WIDTH + DIFF_WIDTH).astype(h.dtype)
    return o @ w_out


def odd_mixer(h, w_in, w_out, t5_table):
    B, S, _ = h.shape
    f32 = jnp.float32
    q, k, v = jnp.split(h @ w_in, 3, axis=-1)
    q = q.reshape(B, S, DIL_HEADS, HEAD_DIM).astype(f32)
    k = k.reshape(B, S, DIL_HEADS, HEAD_DIM).astype(f32)
    v = v.reshape(B, S, DIL_HEADS, HEAD_DIM).astype(f32)
    pad = ((0, 0), (W_MAX, 0), (0, 0), (0, 0))
    kp = jnp.pad(k, pad)
    vp = jnp.pad(v, pad)
    scale = HEAD_DIM ** -0.5
    table = t5_table.astype(f32)

    def block(i):
        t0 = i * Q_BLOCK
        q_b = lax.dynamic_slice_in_dim(q, t0, Q_BLOCK, axis=1)
        outs, lses = [], []
        for (w, r) in DIL_BRANCHES:
            L = w + Q_BLOCK
            nq, nk = Q_BLOCK // r, L // r
            k_s = lax.dynamic_slice_in_dim(kp, t0 + W_MAX - w, L, axis=1).reshape(B, nk, r, DIL_HEADS, HEAD_DIM)
            v_s = lax.dynamic_slice_in_dim(vp, t0 + W_MAX - w, L, axis=1).reshape(B, nk, r, DIL_HEADS, HEAD_DIM)
            q_s = q_b.reshape(B, nq, r, DIL_HEADS, HEAD_DIM)
            sc = jnp.einsum('bqchd,bkchd->bhcqk', q_s, k_s) * scale
            dist = w + (jnp.arange(nq)[:, None] - jnp.arange(nk)[None, :]) * r
            keypos = t0 - w + jnp.arange(nk)[None, :] * r + jnp.arange(r)[:, None]
            valid = ((dist >= 0) & (dist <= w))[None] & (keypos >= 0)[:, None, :]
            bias = table[t5_bucket(dist)].transpose(2, 0, 1)[:, None]
            sc = jnp.where(valid, sc + bias, NEG)
            lse = jax.nn.logsumexp(sc, axis=-1, keepdims=True)
            pr = jnp.exp(sc - lse)
            o = jnp.einsum('bhcqk,bkchd->bqchd', pr, v_s).reshape(B, Q_BLOCK, DIL_HEADS, HEAD_DIM)
            lse = lse[..., 0].transpose(0, 3, 2, 1).reshape(B, Q_BLOCK, DIL_HEADS)
            outs.append(o)
            lses.append(lse)
        wts = jax.nn.softmax(jnp.stack(lses, axis=0), axis=0)
        o = jnp.sum(wts[..., None] * jnp.stack(outs, axis=0), axis=0)
        return o.reshape(B, Q_BLOCK, DIL_HEADS * HEAD_DIM)

    o = lax.map(block, jnp.arange(S // Q_BLOCK))
    o = o.transpose(1, 0, 2, 3).reshape(B, S, DIL_HEADS * HEAD_DIM).astype(h.dtype)
    return o @ w_out


def setup_inputs(seed: int = 0) -> dict:
    key = jax.random.key(seed)
    ks = jax.random.split(key, 24)
    nrm = jax.random.normal
    f32 = jnp.float32
    D = D_MODEL
    return {
        "x": nrm(ks[0], (BATCH, SEQ, D), f32),
        "p": nrm(ks[1], (DEPTH, BATCH, SEQ, PLE_DIM), f32),
        "t5_table": 0.5 * nrm(ks[2], (NUM_BUCKETS, N_SLOTS), f32),
        "w_in_even": nrm(ks[3], (N_EVEN, D, EVEN_IN), f32) * D ** -0.5,
        "w_out_even": nrm(ks[4], (N_EVEN, SB_WIDTH + DIFF_WIDTH, D), f32) * (SB_WIDTH + DIFF_WIDTH) ** -0.5,
        "lambda_q1": 0.1 * nrm(ks[5], (N_EVEN, HEAD_DIM), f32),
        "lambda_k1": 0.1 * nrm(ks[6], (N_EVEN, HEAD_DIM), f32),
        "lambda_q2": 0.1 * nrm(ks[7], (N_EVEN, HEAD_DIM), f32),
        "lambda_k2": 0.1 * nrm(ks[8], (N_EVEN, HEAD_DIM), f32),
        "subln_g": 1.0 + 0.02 * nrm(ks[9], (N_EVEN, DIFF_VDIM), f32),
        "w_in_odd": nrm(ks[10], (N_ODD, D, ODD_IN), f32) * D ** -0.5,
        "w_out_odd": nrm(ks[11], (N_ODD, DIL_HEADS * HEAD_DIM, D), f32) * (DIL_HEADS * HEAD_DIM) ** -0.5,
        "norm_mix_g": 1.0 + 0.02 * nrm(ks[12], (DEPTH, D), f32),
        "norm_mlp_g": 1.0 + 0.02 * nrm(ks[13], (DEPTH, D), f32),
        "w_mlp_up": nrm(ks[14], (DEPTH, D, D_FF), f32) * D ** -0.5,
        "w_mlp_down": nrm(ks[15], (DEPTH, D_FF, D), f32) * D_FF ** -0.5,
        "norm_ple_g": 1.0 + 0.02 * nrm(ks[16], (DEPTH, D), f32),
        "w_ple_gate": nrm(ks[17], (DEPTH, D, D), f32) * D ** -0.5,
        "w_ple_proj": nrm(ks[18], (DEPTH, PLE_DIM, D), f32) * PLE_DIM ** -0.5,
        "final_norm_g": 1.0 + 0.02 * nrm(ks[19], (D,), f32),
    }


def reference(x, p, t5_table, w_in_even, w_out_even, lambda_q1, lambda_k1, lambda_q2, lambda_k2,
              subln_g, w_in_odd, w_out_odd, norm_mix_g, norm_mlp_g, w_mlp_up, w_mlp_down,
              norm_ple_g, w_ple_gate, w_ple_proj, final_norm_g):
    h = x
    for i in range(DEPTH):
        hn = rmsnorm(h, norm_mix_g[i])
        if i % 2 == 0:
            e = i // 2
            h = h + even_mixer(hn, w_in_even[e], w_out_even[e], lambda_q1[e], lambda_k1[e],
                               lambda_q2[e], lambda_k2[e], subln_g[e], t5_table, i)
        else:
            o = i // 2
            h = h + odd_mixer(hn, w_in_odd[o], w_out_odd[o], t5_table)
        hn = rmsnorm(h, norm_mlp_g[i])
        u = jnp.square(jax.nn.relu(hn @ w_mlp_up[i]))
        h = h + u @ w_mlp_down[i]
        gate = jax.nn.sigmoid(rmsnorm(h, norm_ple_g[i]) @ w_ple_gate[i])
        h = h + (p[i] @ w_ple_proj[i]) * gate
    return rmsnorm(h, final_norm_g)
```

```python
import functools
import math

import numpy as np
import jax
import jax.numpy as jnp
from jax import lax
from jax.experimental import pallas as pl
from jax.experimental.pallas import tpu as pltpu

F32 = jnp.float32
BF16 = jnp.bfloat16

D_MODEL = 1024
HEAD_DIM = 64
LANES = 128
SB_HEADS = 8
DIFF_HEADS = 4
DIL_HEADS = 16
SB_WIDTH = SB_HEADS * HEAD_DIM
DIFF_WIDTH = DIFF_HEADS * 2 * HEAD_DIM
DIL_BRANCHES = ((128, 1), (512, 4), (2048, 16))
DIL_WINDOW = 128
NUM_BUCKETS = 32
MAX_DISTANCE = 128
D_FF = 4 * D_MODEL
PLE_DIM = 256
NORM_EPS = 1e-6
SUBLN_EPS = 1e-5
NEG = -1e30
SB_DEAD_LOG = 100.0

VMEM_LIMIT_BYTES = 56 * 1024 * 1024

TM_PROJ = 512
TM_POST = 512
FF_CHUNK = 1024
TQ = 256
TK = 256


def _cparams(sem):
    return pltpu.CompilerParams(dimension_semantics=sem, vmem_limit_bytes=VMEM_LIMIT_BYTES)


def _rms(x, g, eps):
    return x * lax.rsqrt(jnp.mean(x * x, axis=-1, keepdims=True) + eps) * g


def _dot(a, b):
    return jnp.dot(a, b, preferred_element_type=F32)


def _dot_nt(a, b):
    return lax.dot_general(a, b, (((1,), (1,)), ((), ())), preferred_element_type=F32)


def _norm_matmul_kernel(h_ref, g_ref, w_ref, o_ref):
    hn = _rms(h_ref[...], g_ref[...], NORM_EPS)
    o_ref[...] = _dot(hn.astype(BF16), w_ref[...]).astype(o_ref.dtype)


def _norm_matmul(h2d, g, w):
    m, k = h2d.shape
    n = w.shape[1]
    return pl.pallas_call(
        _norm_matmul_kernel,
        grid=(m // TM_PROJ,),
        in_specs=[pl.BlockSpec((TM_PROJ, k), lambda i: (i, 0)),
                  pl.BlockSpec((1, k), lambda i: (0, 0)),
                  pl.BlockSpec((k, n), lambda i: (0, 0))],
        out_specs=pl.BlockSpec((TM_PROJ, n), lambda i: (i, 0)),
        out_shape=jax.ShapeDtypeStruct((m, n), BF16),
        compiler_params=_cparams(("parallel",)),
        name="norm_in_proj",
    )(h2d, g.reshape(1, k), w)


def _t5_bucket_np(dist):
    n = np.maximum(dist, 0)
    max_exact = NUM_BUCKETS // 2
    nf = np.maximum(n, 1).astype(np.float32)
    large = max_exact + (np.log(nf / np.float32(max_exact)) / np.float32(math.log(MAX_DISTANCE / max_exact))
                         * np.float32(NUM_BUCKETS - max_exact)).astype(np.int32)
    large = np.minimum(large, NUM_BUCKETS - 1)
    return np.where(n < max_exact, n, large).astype(np.int32)


def _bias_tile(table_cols, dist, valid):
    bucket = jnp.asarray(_t5_bucket_np(dist))
    b = jnp.take(table_cols, bucket, axis=0)
    b = jnp.where(jnp.asarray(valid)[..., None], b, NEG)
    return jnp.transpose(b, (2, 0, 1))


def _sb_kernel(q_ref, k_ref, v_ref, o_ref, acc_ref, c_ref):
    i = pl.program_id(2)
    q = q_ref[0]
    lane = lax.broadcasted_iota(jnp.int32, (1, LANES), 1)
    head0 = lane < HEAD_DIM
    zero = jnp.zeros_like(q)
    qs = (jnp.where(head0, q, zero), jnp.where(head0, zero, q))
    row = lax.broadcasted_iota(jnp.int32, (TK, TK), 0)
    col = lax.broadcasted_iota(jnp.int32, (TK, TK), 1)
    tri = (row >= col).astype(BF16)
    strict = col < row

    acc_ref[...] = jnp.zeros_like(acc_ref)
    c_ref[...] = jnp.zeros_like(c_ref)

    def tile(j, diag):
        ks = k_ref[0, pl.ds(pl.multiple_of(j * TK, TK), TK), :]
        vs = v_ref[0, pl.ds(pl.multiple_of(j * TK, TK), TK), :]
        vzero = jnp.zeros_like(vs)
        vsel = (jnp.where(head0, vs, vzero), jnp.where(head0, vzero, vs))
        upd = jnp.zeros((TQ, LANES), F32)
        alive = jnp.float32(-jnp.inf)
        for hh in range(2):
            z = _dot_nt(qs[hh], ks)
            lk = -(jnp.maximum(z, 0.0) + jnp.log1p(jnp.exp(-jnp.abs(z))))
            if diag:
                lk = jnp.where(strict, lk, 0.0)
            hi = lk.astype(BF16)
            lo = (lk - hi.astype(F32)).astype(BF16)
            incl = _dot(hi, tri) + _dot(lo, tri)
            c_old = c_ref[hh]
            w = jnp.exp(z + incl + c_old)
            if diag:
                w = jnp.where(strict, w, 0.0)
            upd = upd + _dot(w.astype(BF16), vsel[hh])
            c_new = c_old + incl[:, 0:1]
            c_ref[hh] = c_new
            alive = jnp.maximum(alive, jnp.max(c_new))
        acc_ref[...] += upd
        return alive

    alive0 = tile(i, True)

    def cond(state):
        j, alive = state
        return jnp.logical_and(j >= 0, alive > -SB_DEAD_LOG)

    def body(state):
        j, _ = state
        return j - 1, tile(j, False)

    lax.while_loop(cond, body, (i - 1, alive0))
    o_ref[0] = acc_ref[...].astype(o_ref.dtype)


def _sb_attention(proj3):
    b, s, _ = proj3.shape
    pairs = SB_WIDTH // LANES
    return pl.pallas_call(
        _sb_kernel,
        grid=(b, pairs, s // TQ),
        in_specs=[pl.BlockSpec((1, TQ, LANES), lambda bb, hp, i: (bb, i, hp)),
                  pl.BlockSpec((1, s, LANES), lambda bb, hp, i: (bb, 0, pairs + hp)),
                  pl.BlockSpec((1, s, LANES), lambda bb, hp, i: (bb, 0, 2 * pairs + hp))],
        out_specs=pl.BlockSpec((1, TQ, LANES), lambda bb, hp, i: (bb, i, hp)),
        out_shape=jax.ShapeDtypeStruct((b, s, SB_WIDTH), BF16),
        scratch_shapes=[pltpu.VMEM((TQ, LANES), F32), pltpu.VMEM((2, TQ, 1), F32)],
        compiler_params=_cparams(("parallel", "parallel", "arbitrary")),
        name="sb_attention",
    )(proj3, proj3, proj3)


def _diff_kernel(q_ref, k_ref, v_ref, bias_ref, lam_ref, g_ref, o_ref, m_ref, l_ref, acc_ref, *, lambda_init):
    i = pl.program_id(2)
    q = q_ref[0]
    lane = lax.broadcasted_iota(jnp.int32, (1, LANES), 1)
    map0 = lane < HEAD_DIM
    zero = jnp.zeros_like(q)
    qs = (jnp.where(map0, q, zero), jnp.where(map0, zero, q))

    m_ref[...] = jnp.full_like(m_ref, NEG)
    l_ref[...] = jnp.zeros_like(l_ref)
    acc_ref[...] = jnp.zeros_like(acc_ref)

    def tile(j, bias_idx):
        ks = k_ref[0, pl.ds(pl.multiple_of(j * TK, TK), TK), :]
        vs = v_ref[0, pl.ds(pl.multiple_of(j * TK, TK), TK), :]
        for mm in range(2):
            s = _dot_nt(qs[mm], ks)
            if bias_idx is not None:
                s = s + bias_ref[0, mm, bias_idx]
            m_old = m_ref[mm]
            m_new = jnp.maximum(m_old, jnp.max(s, axis=1, keepdims=True))
            alpha = jnp.exp(m_old - m_new)
            p = jnp.exp(s - m_new)
            l_ref[mm] = alpha * l_ref[mm] + jnp.sum(p, axis=1, keepdims=True)
            acc_ref[mm] = alpha * acc_ref[mm] + _dot(p.astype(BF16), vs)
            m_ref[mm] = m_new

    tile(i, 0)

    @pl.when(i >= 1)
    def _():
        tile(i - 1, 1)

    def body(t, carry):
        tile(i - 2 - t, None)
        return carry

    lax.fori_loop(0, jnp.maximum(i - 1, 0), body, 0)

    lam = (jnp.exp(jnp.sum(lam_ref[0:1, :] * lam_ref[1:2, :], axis=1, keepdims=True))
           - jnp.exp(jnp.sum(lam_ref[2:3, :] * lam_ref[3:4, :], axis=1, keepdims=True)) + lambda_init)
    o = acc_ref[0] / l_ref[0] - lam * (acc_ref[1] / l_ref[1])
    o_ref[0] = (_rms(o, g_ref[...], SUBLN_EPS) * (1.0 - lambda_init)).astype(o_ref.dtype)


def _diff_attention(proj3, bias, lam_params, subln_g, lambda_init):
    b, s, _ = proj3.shape
    base = 3 * SB_WIDTH // LANES
    return pl.pallas_call(
        functools.partial(_diff_kernel, lambda_init=lambda_init),
        grid=(b, DIFF_HEADS, s // TQ),
        in_specs=[pl.BlockSpec((1, TQ, LANES), lambda bb, h, i: (bb, i, base + h)),
                  pl.BlockSpec((1, s, LANES), lambda bb, h, i: (bb, 0, base + DIFF_HEADS + h)),
                  pl.BlockSpec((1, s, LANES), lambda bb, h, i: (bb, 0, base + 2 * DIFF_HEADS + h)),
                  pl.BlockSpec((1, 2, 2, TQ, TK), lambda bb, h, i: (h, 0, 0, 0, 0)),
                  pl.BlockSpec((4, HEAD_DIM), lambda bb, h, i: (0, 0)),
                  pl.BlockSpec((1, LANES), lambda bb, h, i: (0, 0))],
        out_specs=pl.BlockSpec((1, TQ, LANES), lambda bb, h, i: (bb, i, h)),
        out_shape=jax.ShapeDtypeStruct((b, s, DIFF_WIDTH), BF16),
        scratch_shapes=[pltpu.VMEM((2, TQ, 1), F32), pltpu.VMEM((2, TQ, 1), F32),
                        pltpu.VMEM((2, TQ, LANES), F32)],
        compiler_params=_cparams(("parallel", "parallel", "arbitrary")),
        name="diff_attention",
    )(proj3, proj3, proj3, bias, lam_params, subln_g.reshape(1, LANES))


def _diff_bias(t5_table):
    cols = t5_table.astype(F32)[:, SB_HEADS:]
    cols = cols - cols[NUM_BUCKETS - 1:NUM_BUCKETS, :]
    qi = np.arange(TQ)[:, None]
    ki = np.arange(TK)[None, :]
    d0 = qi - ki
    t0 = _bias_tile(cols, d0, d0 >= 0)
    t1 = _bias_tile(cols, d0 + TK, np.ones_like(d0, bool))
    both = jnp.stack([t0, t1], axis=1)
    return both.reshape(DIFF_HEADS, 2, 2, TQ, TK)


def _dil_kernel(*refs, chained, last):
    if chained:
        q_ref, kp_ref, kc_ref, vp_ref, vc_ref, bias_ref, oin_ref, lin_ref = refs[:8]
        outs = refs[8:]
    else:
        q_ref, kp_ref, kc_ref, vp_ref, vc_ref, bias_ref = refs[:6]
        outs = refs[6:]
    i = pl.program_id(3)
    q = q_ref[0]
    lane = lax.broadcasted_iota(jnp.int32, (1, LANES), 1)
    head0 = lane < HEAD_DIM
    zero = jnp.zeros_like(q)
    qs = (jnp.where(head0, q, zero), jnp.where(head0, zero, q))
    kp, kc, vp, vc = kp_ref[0], kc_ref[0], vp_ref[0], vc_ref[0]
    o_heads, lse_heads = [], []
    for hh in range(2):
        s_prev = _dot_nt(qs[hh], kp) + bias_ref[hh, :, 0:DIL_WINDOW]
        s_prev = jnp.where(i > 0, s_prev, NEG)
        s_cur = _dot_nt(qs[hh], kc) + bias_ref[hh, :, DIL_WINDOW:]
        m = jnp.maximum(jnp.max(s_prev, axis=1, keepdims=True), jnp.max(s_cur, axis=1, keepdims=True))
        if chained:
            m_in = lin_ref[0][:, hh * HEAD_DIM:hh * HEAD_DIM + 1]
            m = jnp.maximum(m, m_in)
        p_prev = jnp.exp(s_prev - m)
        p_cur = jnp.exp(s_cur - m)
        l = jnp.sum(p_prev, axis=1, keepdims=True) + jnp.sum(p_cur, axis=1, keepdims=True)
        acc = _dot(p_prev.astype(BF16), vp) + _dot(p_cur.astype(BF16), vc)
        if chained:
            a_in = jnp.exp(m_in - m)
            l = l + a_in
            acc = acc + a_in * oin_ref[0].astype(F32)
        o_heads.append(acc / l)
        lse_heads.append(jnp.broadcast_to(m + jnp.log(l), (TQ, LANES)))
    outs[0][0] = jnp.where(head0, o_heads[0], o_heads[1]).astype(outs[0].dtype)
    if not last:
        outs[1][0] = jnp.where(head0, lse_heads[0], lse_heads[1])


def _dil_bias(t5_table, r):
    qi = np.arange(TQ)[:, None]
    ki = np.arange(DIL_WINDOW + TQ)[None, :] - DIL_WINDOW
    steps = qi - ki
    valid = (steps >= 0) & (steps <= DIL_WINDOW)
    return _bias_tile(t5_table.astype(F32), steps * r, valid)


def _dil_branch(proj3, bias, r, state, last):
    b, s, _ = proj3.shape
    sr = s // r
    cb = 3 * DIL_HEADS * HEAD_DIM // LANES
    ob = D_MODEL // LANES
    pairs = DIL_HEADS // 2
    half = TQ // DIL_WINDOW
    view = proj3.reshape(b, sr, r * 3 * DIL_HEADS * HEAD_DIM)
    chained = state is not None

    def qmap(off):
        return lambda bb, c, hp, i: (bb, i, c * cb + off * pairs + hp)

    def pmap(off):
        return lambda bb, c, hp, i: (bb, jnp.maximum(half * i - 1, 0), c * cb + off * pairs + hp)

    omap = lambda bb, c, hp, i: (bb, i, c * ob + hp)
    in_specs = [pl.BlockSpec((1, TQ, LANES), qmap(0)),
                pl.BlockSpec((1, DIL_WINDOW, LANES), pmap(1)),
                pl.BlockSpec((1, TQ, LANES), qmap(1)),
                pl.BlockSpec((1, DIL_WINDOW, LANES), pmap(2)),
                pl.BlockSpec((1, TQ, LANES), qmap(2)),
                pl.BlockSpec((2, TQ, DIL_WINDOW + TQ), lambda bb, c, hp, i: (hp, 0, 0))]
    args = [view, view, view, view, view, bias]
    if chained:
        in_specs += [pl.BlockSpec((1, TQ, LANES), omap), pl.BlockSpec((1, TQ, LANES), omap)]
        args += [state[0].reshape(b, sr, r * D_MODEL), state[1].reshape(b, sr, r * D_MODEL)]
    out_specs = [pl.BlockSpec((1, TQ, LANES), omap)]
    out_shape = [jax.ShapeDtypeStruct((b, sr, r * D_MODEL), BF16)]
    if not last:
        out_specs.append(pl.BlockSpec((1, TQ, LANES), omap))
        out_shape.append(jax.ShapeDtypeStruct((b, sr, r * D_MODEL), F32))
    res = pl.pallas_call(
        functools.partial(_dil_kernel, chained=chained, last=last),
        grid=(b, r, pairs, sr // TQ),
        in_specs=in_specs, out_specs=out_specs, out_shape=out_shape,
        compiler_params=_cparams(("parallel", "parallel", "parallel", "arbitrary")),
        name=f"dilated_r{r}",
    )(*args)
    return tuple(x.reshape(b, s, D_MODEL) for x in res)


def _dilated_attention(proj3, t5_table):
    state = None
    order = sorted(DIL_BRANCHES, key=lambda wr: -wr[1])
    for n, (w, r) in enumerate(order):
        assert w == DIL_WINDOW * r
        last = n == len(order) - 1
        state = _dil_branch(proj3, _dil_bias(t5_table, r), r, state, last)
    return state[0]


def _post_kernel(*refs, n_o, final):
    h_ref = refs[0]
    o_refs = refs[1:1 + n_o]
    (wo_ref, p_ref, g_mlp_ref, w_up_ref, w_down_ref, g_ple_ref, w_gate_ref, w_proj_ref, g_fin_ref,
     out_ref, acc_ref) = refs[1 + n_o:]
    o = o_refs[0][...] if n_o == 1 else jnp.concatenate([r[...] for r in o_refs], axis=1)
    h = h_ref[...] + _dot(o, wo_ref[...])
    hn = _rms(h, g_mlp_ref[...], NORM_EPS).astype(BF16)
    acc_ref[...] = h
    for c in range(D_FF // FF_CHUNK):
        u = jnp.maximum(_dot(hn, w_up_ref[:, c * FF_CHUNK:(c + 1) * FF_CHUNK]), 0.0)
        acc_ref[...] += _dot((u * u).astype(BF16), w_down_ref[c * FF_CHUNK:(c + 1) * FF_CHUNK, :])
    h = acc_ref[...]
    gate = jax.nn.sigmoid(_dot(_rms(h, g_ple_ref[...], NORM_EPS).astype(BF16), w_gate_ref[...]))
    h = h + _dot(p_ref[...].astype(BF16), w_proj_ref[...]) * gate
    if final:
        h = _rms(h, g_fin_ref[...], NORM_EPS)
    out_ref[...] = h


def _post(h2d, o_list, w_out, p2d, g_mlp, w_up, w_down, g_ple, w_gate, w_proj, g_fin, final):
    m, d = h2d.shape
    n_o = len(o_list)
    row = lambda width: pl.BlockSpec((TM_POST, width), lambda i: (i, 0))
    const = lambda a: pl.BlockSpec(a.shape, lambda i: (0,) * a.ndim, pipeline_mode=pl.Buffered(1))
    vec = lambda g: g.reshape(1, d).astype(F32)
    weights = [w_out, vec(g_mlp), w_up, w_down, vec(g_ple), w_gate, w_proj, vec(g_fin)]
    return pl.pallas_call(
        functools.partial(_post_kernel, n_o=n_o, final=final),
        grid=(m // TM_POST,),
        in_specs=([row(d)] + [row(o.shape[1]) for o in o_list] + [const(weights[0])]
                  + [row(p2d.shape[1])] + [const(w) for w in weights[1:]]),
        out_specs=row(d),
        out_shape=jax.ShapeDtypeStruct((m, d), F32),
        scratch_shapes=[pltpu.VMEM((TM_POST, d), F32)],
        compiler_params=_cparams(("parallel",)),
        name="out_mlp_ple",
    )(h2d, *o_list, weights[0], p2d, *weights[1:])


def _scale_q(w_in, q_cols):
    scale = jnp.where(jnp.asarray(q_cols), HEAD_DIM ** -0.5, 1.0).astype(F32)
    return (w_in * scale[None, :]).astype(BF16)


def kernel(x, p, t5_table, w_in_even, w_out_even, lambda_q1, lambda_k1, lambda_q2, lambda_k2, subln_g,
           w_in_odd, w_out_odd, norm_mix_g, norm_mlp_g, w_mlp_up, w_mlp_down, norm_ple_g, w_ple_gate,
           w_ple_proj, final_norm_g):
    b, s, d = x.shape
    depth = p.shape[0]
    assert d == D_MODEL and s % (TQ * max(r for _, r in DIL_BRANCHES)) == 0 and (b * s) % TM_POST == 0
    col = np.arange(3 * D_MODEL)
    even_q = (col < SB_WIDTH) | ((col >= 3 * SB_WIDTH) & (col < 3 * SB_WIDTH + DIFF_WIDTH))
    odd_q = col < DIL_HEADS * HEAD_DIM
    h = x.reshape(b * s, d)
    for i in range(depth):
        if i % 2 == 0:
            e = i // 2
            proj = _norm_matmul(h, norm_mix_g[i], _scale_q(w_in_even[e], even_q)).reshape(b, s, -1)
            lambda_init = 0.8 - 0.6 * math.exp(-0.3 * i)
            lam_params = jnp.stack([lambda_q1[e], lambda_k1[e], lambda_q2[e], lambda_k2[e]]).astype(F32)
            o_sb = _sb_attention(proj)
            o_d = _diff_attention(proj, _diff_bias(t5_table), lam_params, subln_g[e].astype(F32), lambda_init)
            w_out = w_out_even[e].astype(BF16)
            o_list = [o_sb.reshape(b * s, SB_WIDTH), o_d.reshape(b * s, DIFF_WIDTH)]
        else:
            o = i // 2
            proj = _norm_matmul(h, norm_mix_g[i], _scale_q(w_in_odd[o], odd_q)).reshape(b, s, -1)
            o_list = [_dilated_attention(proj, t5_table).reshape(b * s, d)]
            w_out = w_out_odd[o].astype(BF16)
        h = _post(h, o_list, w_out, p[i].reshape(b * s, PLE_DIM), norm_mlp_g[i],
                  w_mlp_up[i].astype(BF16), w_mlp_down[i].astype(BF16), norm_ple_g[i],
                  w_ple_gate[i].astype(BF16), w_ple_proj[i].astype(BF16), final_norm_g,
                  final=(i == depth - 1))
    return h.reshape(b, s, d)
```

```python
import functools
import math

import numpy as np
import jax
import jax.numpy as jnp
from jax import lax
from jax.experimental import pallas as pl
from jax.experimental.pallas import tpu as pltpu

F32 = jnp.float32
BF16 = jnp.bfloat16

D_MODEL = 1024
HEAD_DIM = 64
LANES = 128
SB_HEADS = 8
DIFF_HEADS = 4
DIL_HEADS = 16
SB_WIDTH = SB_HEADS * HEAD_DIM
DIFF_WIDTH = DIFF_HEADS * 2 * HEAD_DIM
DIL_BRANCHES = ((128, 1), (512, 4), (2048, 16))
DIL_WINDOW = 128
NUM_BUCKETS = 32
MAX_DISTANCE = 128
D_FF = 4 * D_MODEL
PLE_DIM = 256
NORM_EPS = 1e-6
SUBLN_EPS = 1e-5
NEG = -1e30
SB_DEAD_LOG = 100.0

VMEM_LIMIT_BYTES = 56 * 1024 * 1024

TM_PROJ = 512
TM_POST = 512
FF_CHUNK = 1024
TQ = 256
TK = 256
TD = TM_PROJ


def _cparams(sem):
    return pltpu.CompilerParams(dimension_semantics=sem, vmem_limit_bytes=VMEM_LIMIT_BYTES)


def _rms(x, g, eps):
    return x * lax.rsqrt(jnp.mean(x * x, axis=-1, keepdims=True) + eps) * g


def _dot(a, b):
    return jnp.dot(a, b, preferred_element_type=F32)


def _dot_nt(a, b):
    return lax.dot_general(a, b, (((1,), (1,)), ((), ())), preferred_element_type=F32)


def _norm_matmul_kernel(h_ref, g_ref, w_ref, *rest):
    hn = _rms(h_ref[...], g_ref[...], NORM_EPS).astype(BF16)
    if len(rest) == 1:
        (o_ref,) = rest
    else:
        wt_ref, o_ref, ot_ref = rest
        ot_ref[0] = _dot_nt(wt_ref[...], hn).astype(ot_ref.dtype)
    o_ref[...] = _dot(hn, w_ref[...]).astype(o_ref.dtype)


def _norm_matmul(h2d, g, w, wt=None):
    m, k = h2d.shape
    n = w.shape[1]
    const = lambda a: pl.BlockSpec(a.shape, lambda i: (0,) * a.ndim, pipeline_mode=pl.Buffered(1))
    in_specs = [pl.BlockSpec((TM_PROJ, k), lambda i: (i, 0)), pl.BlockSpec((1, k), lambda i: (0, 0)), const(w)]
    out_specs = [pl.BlockSpec((TM_PROJ, n), lambda i: (i, 0))]
    out_shape = [jax.ShapeDtypeStruct((m, n), BF16)]
    args = [h2d, g.reshape(1, k), w]
    if wt is not None:
        nt = wt.shape[0]
        in_specs.append(const(wt))
        args.append(wt)
        out_specs.append(pl.BlockSpec((1, nt, TM_PROJ), lambda i: (i, 0, 0)))
        out_shape.append(jax.ShapeDtypeStruct((m // TM_PROJ, nt, TM_PROJ), BF16))
    return pl.pallas_call(
        _norm_matmul_kernel,
        grid=(m // TM_PROJ,),
        in_specs=in_specs, out_specs=out_specs, out_shape=out_shape,
        compiler_params=_cparams(("parallel",)),
        name="norm_in_proj",
    )(*args)


def _t5_bucket_np(dist):
    n = np.maximum(dist, 0)
    max_exact = NUM_BUCKETS // 2
    nf = np.maximum(n, 1).astype(np.float32)
    large = max_exact + (np.log(nf / np.float32(max_exact)) / np.float32(math.log(MAX_DISTANCE / max_exact))
                         * np.float32(NUM_BUCKETS - max_exact)).astype(np.int32)
    large = np.minimum(large, NUM_BUCKETS - 1)
    return np.where(n < max_exact, n, large).astype(np.int32)


def _bias_tile(table_cols, dist, valid):
    bucket = jnp.asarray(_t5_bucket_np(dist))
    b = jnp.take(table_cols, bucket, axis=0)
    b = jnp.where(jnp.asarray(valid)[..., None], b, NEG)
    return jnp.transpose(b, (2, 0, 1))


def _sb_kernel(q_ref, k_ref, v_ref, o_ref, acc_ref, c_ref):
    i = pl.program_id(2)
    q = q_ref[0]
    lane = lax.broadcasted_iota(jnp.int32, (1, LANES), 1)
    head0 = lane < HEAD_DIM
    zero = jnp.zeros_like(q)
    qs = (jnp.where(head0, q, zero), jnp.where(head0, zero, q))
    row = lax.broadcasted_iota(jnp.int32, (TK, TK), 0)
    col = lax.broadcasted_iota(jnp.int32, (TK, TK), 1)
    tri = (row >= col).astype(BF16)
    strict = col < row

    acc_ref[...] = jnp.zeros_like(acc_ref)
    c_ref[...] = jnp.zeros_like(c_ref)

    def tile(j, diag):
        ks = k_ref[0, pl.ds(pl.multiple_of(j * TK, TK), TK), :]
        vs = v_ref[0, pl.ds(pl.multiple_of(j * TK, TK), TK), :]
        vzero = jnp.zeros_like(vs)
        vsel = (jnp.where(head0, vs, vzero), jnp.where(head0, vzero, vs))
        upd = jnp.zeros((TQ, LANES), F32)
        alive = jnp.float32(-jnp.inf)
        for hh in range(2):
            z = _dot_nt(qs[hh], ks)
            lk = -(jnp.maximum(z, 0.0) + jnp.log1p(jnp.exp(-jnp.abs(z))))
            if diag:
                lk = jnp.where(strict, lk, 0.0)
            hi = lk.astype(BF16)
            lo = (lk - hi.astype(F32)).astype(BF16)
            incl = _dot(hi, tri) + _dot(lo, tri)
            c_old = c_ref[hh]
            w = jnp.exp(z + incl + c_old)
            if diag:
                w = jnp.where(strict, w, 0.0)
            upd = upd + _dot(w.astype(BF16), vsel[hh])
            c_new = c_old + incl[:, 0:1]
            c_ref[hh] = c_new
            alive = jnp.maximum(alive, jnp.max(c_new))
        acc_ref[...] += upd
        return alive

    alive0 = tile(i, True)

    def cond(state):
        j, alive = state
        return jnp.logical_and(j >= 0, alive > -SB_DEAD_LOG)

    def body(state):
        j, _ = state
        return j - 1, tile(j, False)

    lax.while_loop(cond, body, (i - 1, alive0))
    o_ref[0] = acc_ref[...].astype(o_ref.dtype)


def _sb_attention(proj3):
    b, s, _ = proj3.shape
    pairs = SB_WIDTH // LANES
    return pl.pallas_call(
        _sb_kernel,
        grid=(b, pairs, s // TQ),
        in_specs=[pl.BlockSpec((1, TQ, LANES), lambda bb, hp, i: (bb, i, hp)),
                  pl.BlockSpec((1, s, LANES), lambda bb, hp, i: (bb, 0, pairs + hp)),
                  pl.BlockSpec((1, s, LANES), lambda bb, hp, i: (bb, 0, 2 * pairs + hp))],
        out_specs=pl.BlockSpec((1, TQ, LANES), lambda bb, hp, i: (bb, i, hp)),
        out_shape=jax.ShapeDtypeStruct((b, s, SB_WIDTH), BF16),
        scratch_shapes=[pltpu.VMEM((TQ, LANES), F32), pltpu.VMEM((2, TQ, 1), F32)],
        compiler_params=_cparams(("parallel", "parallel", "arbitrary")),
        name="sb_attention",
    )(proj3, proj3, proj3)


def _diff_kernel(qt_ref, k_ref, vt_ref, bias_ref, lam_ref, g_ref, o_ref, m_ref, l_ref, acc_ref, *, lambda_init):
    i = pl.program_id(2)
    qt = qt_ref[0, 0]
    row = lax.broadcasted_iota(jnp.int32, (LANES, 1), 0)
    map0 = row < HEAD_DIM
    zero = jnp.zeros_like(qt)
    qts = (jnp.where(map0, qt, zero), jnp.where(map0, zero, qt))

    m_ref[...] = jnp.full_like(m_ref, NEG)
    l_ref[...] = jnp.zeros_like(l_ref)
    acc_ref[...] = jnp.zeros_like(acc_ref)

    def tile(j, bias_idx):
        ks = k_ref[0, pl.ds(pl.multiple_of(j * TD, TD), TD), :]
        vts = vt_ref[0, j]
        for mm in range(2):
            s = _dot(ks, qts[mm])
            if bias_idx is not None:
                s = s + bias_ref[0, mm, bias_idx]
            m_old = m_ref[mm]
            m_new = jnp.maximum(m_old, jnp.max(s, axis=0, keepdims=True))
            alpha = jnp.exp(m_old - m_new)
            p = jnp.exp(s - m_new)
            l_ref[mm] = alpha * l_ref[mm] + jnp.sum(p, axis=0, keepdims=True)
            acc_ref[mm] = alpha * acc_ref[mm] + _dot(vts, p.astype(BF16))
            m_ref[mm] = m_new

    tile(i, 0)

    @pl.when(i >= 1)
    def _():
        tile(i - 1, 1)

    def body(t, carry):
        tile(i - 2 - t, None)
        return carry

    lax.fori_loop(0, jnp.maximum(i - 1, 0), body, 0)

    lam = (jnp.exp(jnp.sum(lam_ref[0:1, :] * lam_ref[1:2, :], axis=1, keepdims=True))
           - jnp.exp(jnp.sum(lam_ref[2:3, :] * lam_ref[3:4, :], axis=1, keepdims=True)) + lambda_init)
    ot = acc_ref[0] / l_ref[0] - lam * (acc_ref[1] / l_ref[1])
    o = jnp.transpose(ot)
    o_ref[0] = (_rms(o, g_ref[...], SUBLN_EPS) * (1.0 - lambda_init)).astype(o_ref.dtype)


def _diff_attention(proj3, proj_t, bias, lam_params, subln_g, lambda_init):
    b, s, _ = proj3.shape
    kbase = (3 * SB_WIDTH + DIFF_WIDTH) // LANES
    nt = s // TD
    return pl.pallas_call(
        functools.partial(_diff_kernel, lambda_init=lambda_init),
        grid=(b, DIFF_HEADS, nt),
        in_specs=[pl.BlockSpec((1, 1, LANES, TD), lambda bb, h, i: (bb, i, h, 0)),
                  pl.BlockSpec((1, s, LANES), lambda bb, h, i: (bb, 0, kbase + h)),
                  pl.BlockSpec((1, nt, LANES, TD), lambda bb, h, i: (bb, 0, DIFF_HEADS + h, 0)),
                  pl.BlockSpec((1, 2, 2, TD, TD), lambda bb, h, i: (h, 0, 0, 0, 0)),
                  pl.BlockSpec((4, HEAD_DIM), lambda bb, h, i: (0, 0)),
                  pl.BlockSpec((1, LANES), lambda bb, h, i: (0, 0))],
        out_specs=pl.BlockSpec((1, TD, LANES), lambda bb, h, i: (bb, i, h)),
        out_shape=jax.ShapeDtypeStruct((b, s, DIFF_WIDTH), BF16),
        scratch_shapes=[pltpu.VMEM((2, 1, TD), F32), pltpu.VMEM((2, 1, TD), F32),
                        pltpu.VMEM((2, LANES, TD), F32)],
        compiler_params=_cparams(("parallel", "parallel", "arbitrary")),
        name="diff_attention",
    )(proj_t, proj3, proj_t, bias, lam_params, subln_g.reshape(1, LANES))


def _diff_bias(t5_table):
    cols = t5_table.astype(F32)[:, SB_HEADS:]
    cols = cols - cols[NUM_BUCKETS - 1:NUM_BUCKETS, :]
    ki = np.arange(TD)[:, None]
    qi = np.arange(TD)[None, :]
    d0 = qi - ki
    t0 = _bias_tile(cols, d0, d0 >= 0)
    t1 = _bias_tile(cols, d0 + TD, np.ones_like(d0, bool))
    both = jnp.stack([t0, t1], axis=1)
    return both.reshape(DIFF_HEADS, 2, 2, TD, TD)


def _dil_kernel(*refs, chained, last):
    if chained:
        q_ref, kp_ref, kc_ref, vp_ref, vc_ref, bias_ref, oin_ref, lin_ref = refs[:8]
        outs = refs[8:]
    else:
        q_ref, kp_ref, kc_ref, vp_ref, vc_ref, bias_ref = refs[:6]
        outs = refs[6:]
    i = pl.program_id(3)
    q = q_ref[0]
    lane = lax.broadcasted_iota(jnp.int32, (1, LANES), 1)
    head0 = lane < HEAD_DIM
    zero = jnp.zeros_like(q)
    qs = (jnp.where(head0, q, zero), jnp.where(head0, zero, q))
    kp, kc, vp, vc = kp_ref[0], kc_ref[0], vp_ref[0], vc_ref[0]
    o_heads, lse_heads = [], []
    for hh in range(2):
        s_prev = _dot_nt(qs[hh], kp) + bias_ref[hh, :, 0:DIL_WINDOW]
        s_prev = jnp.where(i > 0, s_prev, NEG)
        s_cur = _dot_nt(qs[hh], kc) + bias_ref[hh, :, DIL_WINDOW:]
        m = jnp.maximum(jnp.max(s_prev, axis=1, keepdims=True), jnp.max(s_cur, axis=1, keepdims=True))
        if chained:
            m_in = lin_ref[0][:, hh * HEAD_DIM:hh * HEAD_DIM + 1]
            m = jnp.maximum(m, m_in)
        p_prev = jnp.exp(s_prev - m)
        p_cur = jnp.exp(s_cur - m)
        l = jnp.sum(p_prev, axis=1, keepdims=True) + jnp.sum(p_cur, axis=1, keepdims=True)
        acc = _dot(p_prev.astype(BF16), vp) + _dot(p_cur.astype(BF16), vc)
        if chained:
            a_in = jnp.exp(m_in - m)
            l = l + a_in
            acc = acc + a_in * oin_ref[0].astype(F32)
        o_heads.append(acc / l)
        lse_heads.append(jnp.broadcast_to(m + jnp.log(l), (TQ, LANES)))
    outs[0][0] = jnp.where(head0, o_heads[0], o_heads[1]).astype(outs[0].dtype)
    if not last:
        outs[1][0] = jnp.where(head0, lse_heads[0], lse_heads[1])


def _dil_bias(t5_table, r):
    qi = np.arange(TQ)[:, None]
    ki = np.arange(DIL_WINDOW + TQ)[None, :] - DIL_WINDOW
    steps = qi - ki
    valid = (steps >= 0) & (steps <= DIL_WINDOW)
    return _bias_tile(t5_table.astype(F32), steps * r, valid)


def _dil_branch(proj3, bias, r, state, last):
    b, s, _ = proj3.shape
    sr = s // r
    cb = 3 * DIL_HEADS * HEAD_DIM // LANES
    ob = D_MODEL // LANES
    pairs = DIL_HEADS // 2
    half = TQ // DIL_WINDOW
    view = proj3.reshape(b, sr, r * 3 * DIL_HEADS * HEAD_DIM)
    chained = state is not None

    def qmap(off):
        return lambda bb, c, hp, i: (bb, i, c * cb + off * pairs + hp)

    def pmap(off):
        return lambda bb, c, hp, i: (bb, jnp.maximum(half * i - 1, 0), c * cb + off * pairs + hp)

    omap = lambda bb, c, hp, i: (bb, i, c * ob + hp)
    in_specs = [pl.BlockSpec((1, TQ, LANES), qmap(0)),
                pl.BlockSpec((1, DIL_WINDOW, LANES), pmap(1)),
                pl.BlockSpec((1, TQ, LANES), qmap(1)),
                pl.BlockSpec((1, DIL_WINDOW, LANES), pmap(2)),
                pl.BlockSpec((1, TQ, LANES), qmap(2)),
                pl.BlockSpec((2, TQ, DIL_WINDOW + TQ), lambda bb, c, hp, i: (hp, 0, 0))]
    args = [view, view, view, view, view, bias]
    if chained:
        in_specs += [pl.BlockSpec((1, TQ, LANES), omap), pl.BlockSpec((1, TQ, LANES), omap)]
        args += [state[0].reshape(b, sr, r * D_MODEL), state[1].reshape(b, sr, r * D_MODEL)]
    out_specs = [pl.BlockSpec((1, TQ, LANES), omap)]
    out_shape = [jax.ShapeDtypeStruct((b, sr, r * D_MODEL), BF16)]
    if not last:
        out_specs.append(pl.BlockSpec((1, TQ, LANES), omap))
        out_shape.append(jax.ShapeDtypeStruct((b, sr, r * D_MODEL), F32))
    res = pl.pallas_call(
        functools.partial(_dil_kernel, chained=chained, last=last),
        grid=(b, r, pairs, sr // TQ),
        in_specs=in_specs, out_specs=out_specs, out_shape=out_shape,
        compiler_params=_cparams(("parallel", "parallel", "parallel", "arbitrary")),
        name=f"dilated_r{r}",
    )(*args)
    return tuple(x.reshape(b, s, D_MODEL) for x in res)


def _dilated_attention(proj3, t5_table):
    state = None
    order = sorted(DIL_BRANCHES, key=lambda wr: -wr[1])
    for n, (w, r) in enumerate(order):
        assert w == DIL_WINDOW * r
        last = n == len(order) - 1
        state = _dil_branch(proj3, _dil_bias(t5_table, r), r, state, last)
    return state[0]


def _post_kernel(*refs, n_o, final):
    h_ref = refs[0]
    o_refs = refs[1:1 + n_o]
    (wo_ref, p_ref, g_mlp_ref, w_up_ref, w_down_ref, g_ple_ref, w_gate_ref, w_proj_ref, g_fin_ref,
     out_ref, acc_ref) = refs[1 + n_o:]
    o = o_refs[0][...] if n_o == 1 else jnp.concatenate([r[...] for r in o_refs], axis=1)
    h = h_ref[...] + _dot(o, wo_ref[...])
    hn = _rms(h, g_mlp_ref[...], NORM_EPS).astype(BF16)
    acc_ref[...] = h
    for c in range(D_FF // FF_CHUNK):
        u = jnp.maximum(_dot(hn, w_up_ref[:, c * FF_CHUNK:(c + 1) * FF_CHUNK]), 0.0)
        acc_ref[...] += _dot((u * u).astype(BF16), w_down_ref[c * FF_CHUNK:(c + 1) * FF_CHUNK, :])
    h = acc_ref[...]
    gate = jax.nn.sigmoid(_dot(_rms(h, g_ple_ref[...], NORM_EPS).astype(BF16), w_gate_ref[...]))
    h = h + _dot(p_ref[...].astype(BF16), w_proj_ref[...]) * gate
    if final:
        h = _rms(h, g_fin_ref[...], NORM_EPS)
    out_ref[...] = h


def _post(h2d, o_list, w_out, p2d, g_mlp, w_up, w_down, g_ple, w_gate, w_proj, g_fin, final):
    m, d = h2d.shape
    n_o = len(o_list)
    row = lambda width: pl.BlockSpec((TM_POST, width), lambda i: (i, 0))
    const = lambda a: pl.BlockSpec(a.shape, lambda i: (0,) * a.ndim, pipeline_mode=pl.Buffered(1))
    vec = lambda g: g.reshape(1, d).astype(F32)
    weights = [w_out, vec(g_mlp), w_up, w_down, vec(g_ple), w_gate, w_proj, vec(g_fin)]
    return pl.pallas_call(
        functools.partial(_post_kernel, n_o=n_o, final=final),
        grid=(m // TM_POST,),
        in_specs=([row(d)] + [row(o.shape[1]) for o in o_list] + [const(weights[0])]
                  + [row(p2d.shape[1])] + [const(w) for w in weights[1:]]),
        out_specs=row(d),
        out_shape=jax.ShapeDtypeStruct((m, d), F32),
        scratch_shapes=[pltpu.VMEM((TM_POST, d), F32)],
        compiler_params=_cparams(("parallel",)),
        name="out_mlp_ple",
    )(h2d, *o_list, weights[0], p2d, *weights[1:])


def _scale_q(w_in, q_cols):
    scale = jnp.where(jnp.asarray(q_cols), HEAD_DIM ** -0.5, 1.0).astype(F32)
    return (w_in * scale[None, :]).astype(BF16)


def kernel(x, p, t5_table, w_in_even, w_out_even, lambda_q1, lambda_k1, lambda_q2, lambda_k2, subln_g,
           w_in_odd, w_out_odd, norm_mix_g, norm_mlp_g, w_mlp_up, w_mlp_down, norm_ple_g, w_ple_gate,
           w_ple_proj, final_norm_g):
    b, s, d = x.shape
    depth = p.shape[0]
    assert d == D_MODEL and s % (TQ * max(r for _, r in DIL_BRANCHES)) == 0 and (b * s) % TM_POST == 0
    col = np.arange(3 * D_MODEL)
    even_q = (col < SB_WIDTH) | ((col >= 3 * SB_WIDTH) & (col < 3 * SB_WIDTH + DIFF_WIDTH))
    odd_q = col < DIL_HEADS * HEAD_DIM
    h = x.reshape(b * s, d)
    for i in range(depth):
        if i % 2 == 0:
            e = i // 2
            w_in = _scale_q(w_in_even[e], even_q)
            qb0, vb0 = 3 * SB_WIDTH, 3 * SB_WIDTH + 2 * DIFF_WIDTH
            w_t = jnp.concatenate([w_in[:, qb0:qb0 + DIFF_WIDTH], w_in[:, vb0:vb0 + DIFF_WIDTH]], axis=1).T
            proj, proj_t = _norm_matmul(h, norm_mix_g[i], w_in, w_t)
            proj = proj.reshape(b, s, -1)
            proj_t = proj_t.reshape(b, s // TD, 2 * DIFF_WIDTH, TD)
            lambda_init = 0.8 - 0.6 * math.exp(-0.3 * i)
            lam_params = jnp.stack([lambda_q1[e], lambda_k1[e], lambda_q2[e], lambda_k2[e]]).astype(F32)
            o_sb = _sb_attention(proj)
            o_d = _diff_attention(proj, proj_t, _diff_bias(t5_table), lam_params, subln_g[e].astype(F32),
                                  lambda_init)
            w_out = w_out_even[e].astype(BF16)
            o_list = [o_sb.reshape(b * s, SB_WIDTH), o_d.reshape(b * s, DIFF_WIDTH)]
        else:
            o = i // 2
            (proj,) = _norm_matmul(h, norm_mix_g[i], _scale_q(w_in_odd[o], odd_q))
            proj = proj.reshape(b, s, -1)
            o_list = [_dilated_attention(proj, t5_table).reshape(b * s, d)]
            w_out = w_out_odd[o].astype(BF16)
        h = _post(h, o_list, w_out, p[i].reshape(b * s, PLE_DIM), norm_mlp_g[i],
                  w_mlp_up[i].astype(BF16), w_mlp_down[i].astype(BF16), norm_ple_g[i],
                  w_ple_gate[i].astype(BF16), w_ple_proj[i].astype(BF16), final_norm_g,
                  final=(i == depth - 1))
    return h.reshape(b, s, d)
```

```python
import functools
import math

import numpy as np
import jax
import jax.numpy as jnp
from jax import lax
from jax.experimental import pallas as pl
from jax.experimental.pallas import tpu as pltpu

F32 = jnp.float32
BF16 = jnp.bfloat16

D_MODEL = 1024
HEAD_DIM = 64
LANES = 128
SB_HEADS = 8
DIFF_HEADS = 4
DIL_HEADS = 16
SB_WIDTH = SB_HEADS * HEAD_DIM
DIFF_WIDTH = DIFF_HEADS * 2 * HEAD_DIM
DIL_BRANCHES = ((128, 1), (512, 4), (2048, 16))
DIL_WINDOW = 128
NUM_BUCKETS = 32
MAX_DISTANCE = 128
D_FF = 4 * D_MODEL
PLE_DIM = 256
NORM_EPS = 1e-6
SUBLN_EPS = 1e-5
NEG = -1e30
SB_DEAD_LOG = 100.0

VMEM_LIMIT_BYTES = 56 * 1024 * 1024

TM_PROJ = 512
TM_POST = 512
FF_CHUNK = 1024
TQ = 256
TK = 256
TD = TM_PROJ
DT = 2048
DQ = DIL_WINDOW
DIL_UNROLL = 8


def _cparams(sem):
    return pltpu.CompilerParams(dimension_semantics=sem, vmem_limit_bytes=VMEM_LIMIT_BYTES)


def _rms(x, g, eps):
    return x * lax.rsqrt(jnp.mean(x * x, axis=-1, keepdims=True) + eps) * g


def _dot(a, b):
    return jnp.dot(a, b, preferred_element_type=F32)


def _dot_nt(a, b):
    return lax.dot_general(a, b, (((1,), (1,)), ((), ())), preferred_element_type=F32)


def _norm_matmul_kernel(h_ref, g_ref, w_ref, *rest):
    hn = _rms(h_ref[...], g_ref[...], NORM_EPS).astype(BF16)
    if len(rest) == 1:
        (o_ref,) = rest
    else:
        wt_ref, o_ref, ot_ref = rest
        ot_ref[0] = _dot_nt(wt_ref[...], hn).astype(ot_ref.dtype)
    o_ref[...] = _dot(hn, w_ref[...]).astype(o_ref.dtype)


def _norm_matmul(h2d, g, w, wt=None):
    m, k = h2d.shape
    n = w.shape[1]
    const = lambda a: pl.BlockSpec(a.shape, lambda i: (0,) * a.ndim, pipeline_mode=pl.Buffered(1))
    in_specs = [pl.BlockSpec((TM_PROJ, k), lambda i: (i, 0)), pl.BlockSpec((1, k), lambda i: (0, 0)), const(w)]
    out_specs = [pl.BlockSpec((TM_PROJ, n), lambda i: (i, 0))]
    out_shape = [jax.ShapeDtypeStruct((m, n), BF16)]
    args = [h2d, g.reshape(1, k), w]
    if wt is not None:
        nt = wt.shape[0]
        in_specs.append(const(wt))
        args.append(wt)
        out_specs.append(pl.BlockSpec((1, nt, TM_PROJ), lambda i: (i, 0, 0)))
        out_shape.append(jax.ShapeDtypeStruct((m // TM_PROJ, nt, TM_PROJ), BF16))
    return pl.pallas_call(
        _norm_matmul_kernel,
        grid=(m // TM_PROJ,),
        in_specs=in_specs, out_specs=out_specs, out_shape=out_shape,
        compiler_params=_cparams(("parallel",)),
        name="norm_in_proj",
    )(*args)


def _t5_bucket_np(dist):
    n = np.maximum(dist, 0)
    max_exact = NUM_BUCKETS // 2
    nf = np.maximum(n, 1).astype(np.float32)
    large = max_exact + (np.log(nf / np.float32(max_exact)) / np.float32(math.log(MAX_DISTANCE / max_exact))
                         * np.float32(NUM_BUCKETS - max_exact)).astype(np.int32)
    large = np.minimum(large, NUM_BUCKETS - 1)
    return np.where(n < max_exact, n, large).astype(np.int32)


def _bias_tile(table_cols, dist, valid):
    bucket = _t5_bucket_np(dist)
    out = jnp.zeros((table_cols.shape[1],) + dist.shape, F32)
    for bkt in np.unique(bucket[valid]):
        out = jnp.where(jnp.asarray(bucket == bkt)[None], table_cols[bkt][:, None, None], out)
    return jnp.where(jnp.asarray(valid)[None], out, NEG)


def _sb_kernel(q_ref, k_ref, v_ref, o_ref, acc_ref, c_ref):
    i = pl.program_id(2)
    q = q_ref[0]
    lane = lax.broadcasted_iota(jnp.int32, (1, LANES), 1)
    head0 = lane < HEAD_DIM
    zero = jnp.zeros_like(q)
    qs = (jnp.where(head0, q, zero), jnp.where(head0, zero, q))
    row = lax.broadcasted_iota(jnp.int32, (TK, TK), 0)
    col = lax.broadcasted_iota(jnp.int32, (TK, TK), 1)
    tri = (row >= col).astype(BF16)
    strict = col < row

    acc_ref[...] = jnp.zeros_like(acc_ref)
    c_ref[...] = jnp.zeros_like(c_ref)

    def tile(j, diag):
        ks = k_ref[0, pl.ds(pl.multiple_of(j * TK, TK), TK), :]
        vs = v_ref[0, pl.ds(pl.multiple_of(j * TK, TK), TK), :]
        vzero = jnp.zeros_like(vs)
        vsel = (jnp.where(head0, vs, vzero), jnp.where(head0, vzero, vs))
        upd = jnp.zeros((TQ, LANES), F32)
        alive = jnp.float32(-jnp.inf)
        for hh in range(2):
            z = _dot_nt(qs[hh], ks)
            lk = -(jnp.maximum(z, 0.0) + jnp.log1p(jnp.exp(-jnp.abs(z))))
            if diag:
                lk = jnp.where(strict, lk, 0.0)
            hi = lk.astype(BF16)
            lo = (lk - hi.astype(F32)).astype(BF16)
            incl = _dot(hi, tri) + _dot(lo, tri)
            c_old = c_ref[hh]
            w = jnp.exp(z + incl + c_old)
            if diag:
                w = jnp.where(strict, w, 0.0)
            upd = upd + _dot(w.astype(BF16), vsel[hh])
            c_new = c_old + incl[:, 0:1]
            c_ref[hh] = c_new
            alive = jnp.maximum(alive, jnp.max(c_new))
        acc_ref[...] += upd
        return alive

    alive0 = tile(i, True)

    def cond(state):
        j, alive = state
        return jnp.logical_and(j >= 0, alive > -SB_DEAD_LOG)

    def body(state):
        j, _ = state
        return j - 1, tile(j, False)

    lax.while_loop(cond, body, (i - 1, alive0))
    o_ref[0] = acc_ref[...].astype(o_ref.dtype)


def _sb_attention(proj3):
    b, s, _ = proj3.shape
    pairs = SB_WIDTH // LANES
    return pl.pallas_call(
        _sb_kernel,
        grid=(b, pairs, s // TQ),
        in_specs=[pl.BlockSpec((1, TQ, LANES), lambda bb, hp, i: (bb, i, hp)),
                  pl.BlockSpec((1, s, LANES), lambda bb, hp, i: (bb, 0, pairs + hp)),
                  pl.BlockSpec((1, s, LANES), lambda bb, hp, i: (bb, 0, 2 * pairs + hp))],
        out_specs=pl.BlockSpec((1, TQ, LANES), lambda bb, hp, i: (bb, i, hp)),
        out_shape=jax.ShapeDtypeStruct((b, s, SB_WIDTH), BF16),
        scratch_shapes=[pltpu.VMEM((TQ, LANES), F32), pltpu.VMEM((2, TQ, 1), F32)],
        compiler_params=_cparams(("parallel", "parallel", "arbitrary")),
        name="sb_attention",
    )(proj3, proj3, proj3)


def _diff_kernel(qt_ref, k_ref, vt_ref, bias_ref, lam_ref, g_ref, o_ref, m_ref, l_ref, acc_ref, *, lambda_init):
    i = pl.program_id(2)
    qt = qt_ref[0, 0]
    row = lax.broadcasted_iota(jnp.int32, (LANES, 1), 0)
    map0 = row < HEAD_DIM
    zero = jnp.zeros_like(qt)
    qts = (jnp.where(map0, qt, zero), jnp.where(map0, zero, qt))

    m_ref[...] = jnp.full_like(m_ref, NEG)
    l_ref[...] = jnp.zeros_like(l_ref)
    acc_ref[...] = jnp.zeros_like(acc_ref)

    def tile(j, bias_idx):
        ks = k_ref[0, pl.ds(pl.multiple_of(j * TD, TD), TD), :]
        vts = vt_ref[0, j]
        for mm in range(2):
            s = _dot(ks, qts[mm])
            if bias_idx is not None:
                s = s + bias_ref[0, mm, bias_idx]
            m_old = m_ref[mm]
            m_new = jnp.maximum(m_old, jnp.max(s, axis=0, keepdims=True))
            alpha = jnp.exp(m_old - m_new)
            p = jnp.exp(s - m_new)
            l_ref[mm] = alpha * l_ref[mm] + jnp.sum(p, axis=0, keepdims=True)
            acc_ref[mm] = alpha * acc_ref[mm] + _dot(vts, p.astype(BF16))
            m_ref[mm] = m_new

    tile(i, 0)

    @pl.when(i >= 1)
    def _():
        tile(i - 1, 1)

    def body(t, carry):
        tile(i - 2 - t, None)
        return carry

    lax.fori_loop(0, jnp.maximum(i - 1, 0), body, 0)

    lam = (jnp.exp(jnp.sum(lam_ref[0:1, :] * lam_ref[1:2, :], axis=1, keepdims=True))
           - jnp.exp(jnp.sum(lam_ref[2:3, :] * lam_ref[3:4, :], axis=1, keepdims=True)) + lambda_init)
    ot = acc_ref[0] / l_ref[0] - lam * (acc_ref[1] / l_ref[1])
    o = jnp.transpose(ot)
    o_ref[0] = (_rms(o, g_ref[...], SUBLN_EPS) * (1.0 - lambda_init)).astype(o_ref.dtype)


def _diff_attention(proj3, proj_t, bias, lam_params, subln_g, lambda_init):
    b, s, _ = proj3.shape
    kbase = (3 * SB_WIDTH + DIFF_WIDTH) // LANES
    nt = s // TD
    return pl.pallas_call(
        functools.partial(_diff_kernel, lambda_init=lambda_init),
        grid=(b, DIFF_HEADS, nt),
        in_specs=[pl.BlockSpec((1, 1, LANES, TD), lambda bb, h, i: (bb, i, h, 0)),
                  pl.BlockSpec((1, s, LANES), lambda bb, h, i: (bb, 0, kbase + h)),
                  pl.BlockSpec((1, nt, LANES, TD), lambda bb, h, i: (bb, 0, DIFF_HEADS + h, 0)),
                  pl.BlockSpec((1, 2, 2, TD, TD), lambda bb, h, i: (h, 0, 0, 0, 0)),
                  pl.BlockSpec((4, HEAD_DIM), lambda bb, h, i: (0, 0)),
                  pl.BlockSpec((1, LANES), lambda bb, h, i: (0, 0))],
        out_specs=pl.BlockSpec((1, TD, LANES), lambda bb, h, i: (bb, i, h)),
        out_shape=jax.ShapeDtypeStruct((b, s, DIFF_WIDTH), BF16),
        scratch_shapes=[pltpu.VMEM((2, 1, TD), F32), pltpu.VMEM((2, 1, TD), F32),
                        pltpu.VMEM((2, LANES, TD), F32)],
        compiler_params=_cparams(("parallel", "parallel", "arbitrary")),
        name="diff_attention",
    )(proj_t, proj3, proj_t, bias, lam_params, subln_g.reshape(1, LANES))


def _diff_bias(t5_table):
    cols = t5_table.astype(F32)[:, SB_HEADS:]
    cols = cols - cols[NUM_BUCKETS - 1:NUM_BUCKETS, :]
    nb = TD // MAX_DISTANCE
    ki = np.arange(MAX_DISTANCE)[:, None]
    qi = np.arange(MAX_DISTANCE)[None, :]
    d0 = qi - ki
    diag = _bias_tile(cols, d0, d0 >= 0)
    prev = _bias_tile(cols, d0 + MAX_DISTANCE, np.ones_like(d0, bool))
    zero = jnp.zeros_like(diag)
    masked = jnp.full_like(diag, NEG)

    def block0(kb, qb):
        return masked if kb > qb else diag if kb == qb else prev if kb == qb - 1 else zero

    t0 = jnp.concatenate([jnp.concatenate([block0(kb, qb) for qb in range(nb)], axis=2) for kb in range(nb)], axis=1)
    t1 = jnp.concatenate([jnp.concatenate([prev if (kb == nb - 1 and qb == 0) else zero for qb in range(nb)], axis=2)
                          for kb in range(nb)], axis=1)
    both = jnp.stack([t0, t1], axis=1)
    return both.reshape(DIFF_HEADS, 2, 2, TD, TD)


def _dil_kernel(q_ref, kp_ref, kc_ref, vp_ref, vc_ref, bias_ref, o_ref, qf, kf, vf, acc_s, m_s, l_s):
    t_idx = pl.program_id(2)
    qf[...] = q_ref[0].astype(F32)
    kf[0:DT, :] = kp_ref[0].astype(F32)
    kf[DT:2 * DT, :] = kc_ref[0].astype(F32)
    vf[0:DT, :] = vp_ref[0].astype(F32)
    vf[DT:2 * DT, :] = vc_ref[0].astype(F32)
    lane = lax.broadcasted_iota(jnp.int32, (1, LANES), 1)
    head0 = lane < HEAD_DIM
    heads = (head0, jnp.logical_not(head0))
    before_tile = lax.broadcasted_iota(jnp.int32, (1, 2 * DQ), 1) < DQ

    for g, (_, r) in enumerate(DIL_BRANCHES):
        nblk = DT // (r * DQ)
        shift = nblk.bit_length() - 1

        def block(t, carry, g=g, r=r, nblk=nblk, shift=shift):
            c = lax.shift_right_logical(t, shift)
            n = jnp.bitwise_and(t, nblk - 1)
            q0 = c + n * (DQ * r)
            k0 = DT + q0 - DQ * r
            qg = qf[pl.ds(q0, DQ, stride=r), :].astype(BF16)
            kg = kf[pl.ds(k0, 2 * DQ, stride=r), :].astype(BF16)
            vg = vf[pl.ds(k0, 2 * DQ, stride=r), :]
            off = jnp.where(jnp.logical_and(t_idx == 0, n == 0), NEG, 0.0)
            pre = jnp.where(before_tile, off, 0.0)
            res, mx = [], []
            for hh in range(2):
                qh = jnp.where(heads[hh], qg, jnp.zeros_like(qg))
                s = _dot_nt(qh, kg) + bias_ref[0, hh, g] + pre
                m = jnp.max(s, axis=1, keepdims=True)
                p = jnp.exp(s - m)
                vh = jnp.where(heads[hh], vg, 1.0).astype(BF16)
                res.append(_dot(p.astype(BF16), vh))
                mx.append(jnp.broadcast_to(m, (DQ, LANES)))
            rows = pl.ds(q0, DQ, stride=r)
            acc_s[g, rows, :] = jnp.where(head0, res[0], res[1])
            m_s[g, rows, :] = jnp.where(head0, mx[0], mx[1])
            l_s[g, rows, :] = pltpu.roll(jnp.where(head0, res[1], res[0]), HEAD_DIM, 1)
            return carry

        lax.fori_loop(0, r * nblk, block, 0, unroll=DIL_UNROLL)

    def merge(ci, carry):
        rows = pl.ds(pl.multiple_of(ci * TQ, TQ), TQ)
        ms = [m_s[g, rows, :] for g in range(len(DIL_BRANCHES))]
        m = functools.reduce(jnp.maximum, ms)
        num = jnp.zeros((TQ, LANES), F32)
        den = jnp.zeros((TQ, LANES), F32)
        for g in range(len(DIL_BRANCHES)):
            wgt = jnp.exp(ms[g] - m)
            num = num + wgt * acc_s[g, rows, :]
            den = den + wgt * l_s[g, rows, :]
        o_ref[0, rows, :] = (num / den).astype(o_ref.dtype)
        return carry

    lax.fori_loop(0, DT // TQ, merge, 0)


def _dil_bias(t5_table):
    qi = np.arange(DQ)[:, None]
    ki = np.arange(2 * DQ)[None, :] - DQ
    steps = qi - ki
    valid = (steps >= 0) & (steps <= DIL_WINDOW)
    tiles = [_bias_tile(t5_table.astype(F32), steps * r, valid) for _, r in DIL_BRANCHES]
    return jnp.stack(tiles, axis=1).reshape(DIL_HEADS // 2, 2, len(DIL_BRANCHES), DQ, 2 * DQ)


def _dilated_attention(proj3, t5_table):
    b, s, _ = proj3.shape
    pairs = DIL_HEADS // 2
    bias = _dil_bias(t5_table)
    cur = lambda off: (lambda bb, hp, t: (bb, t, off * pairs + hp))
    prev = lambda off: (lambda bb, hp, t: (bb, jnp.maximum(t - 1, 0), off * pairs + hp))
    tile = lambda imap: pl.BlockSpec((1, DT, LANES), imap)
    slab = lambda rows: pltpu.VMEM((rows, LANES), F32)
    per_branch = pltpu.VMEM((len(DIL_BRANCHES), DT, LANES), F32)
    return pl.pallas_call(
        _dil_kernel,
        grid=(b, pairs, s // DT),
        in_specs=[tile(cur(0)), tile(prev(1)), tile(cur(1)), tile(prev(2)), tile(cur(2)),
                  pl.BlockSpec((1, 2, len(DIL_BRANCHES), DQ, 2 * DQ), lambda bb, hp, t: (hp, 0, 0, 0, 0))],
        out_specs=tile(cur(0)),
        out_shape=jax.ShapeDtypeStruct((b, s, D_MODEL), BF16),
        scratch_shapes=[slab(DT), slab(2 * DT), slab(2 * DT), per_branch, per_branch, per_branch],
        compiler_params=_cparams(("parallel", "parallel", "arbitrary")),
        name="dilated_attention",
    )(proj3, proj3, proj3, proj3, proj3, bias)


def _post_kernel(*refs, n_o, final):
    h_ref = refs[0]
    o_refs = refs[1:1 + n_o]
    (wo_ref, p_ref, g_mlp_ref, w_up_ref, w_down_ref, g_ple_ref, w_gate_ref, w_proj_ref, g_fin_ref,
     out_ref, acc_ref) = refs[1 + n_o:]
    o = o_refs[0][...] if n_o == 1 else jnp.concatenate([r[...] for r in o_refs], axis=1)
    h = h_ref[...] + _dot(o, wo_ref[...])
    hn = _rms(h, g_mlp_ref[...], NORM_EPS).astype(BF16)
    acc_ref[...] = h
    for c in range(D_FF // FF_CHUNK):
        u = jnp.maximum(_dot(hn, w_up_ref[:, c * FF_CHUNK:(c + 1) * FF_CHUNK]), 0.0)
        acc_ref[...] += _dot((u * u).astype(BF16), w_down_ref[c * FF_CHUNK:(c + 1) * FF_CHUNK, :])
    h = acc_ref[...]
    gate = jax.nn.sigmoid(_dot(_rms(h, g_ple_ref[...], NORM_EPS).astype(BF16), w_gate_ref[...]))
    h = h + _dot(p_ref[...].astype(BF16), w_proj_ref[...]) * gate
    if final:
        h = _rms(h, g_fin_ref[...], NORM_EPS)
    out_ref[...] = h


def _post(h2d, o_list, w_out, p2d, g_mlp, w_up, w_down, g_ple, w_gate, w_proj, g_fin, final):
    m, d = h2d.shape
    n_o = len(o_list)
    row = lambda width: pl.BlockSpec((TM_POST, width), lambda i: (i, 0))
    const = lambda a: pl.BlockSpec(a.shape, lambda i: (0,) * a.ndim, pipeline_mode=pl.Buffered(1))
    vec = lambda g: g.reshape(1, d).astype(F32)
    weights = [w_out, vec(g_mlp), w_up, w_down, vec(g_ple), w_gate, w_proj, vec(g_fin)]
    return pl.pallas_call(
        functools.partial(_post_kernel, n_o=n_o, final=final),
        grid=(m // TM_POST,),
        in_specs=([row(d)] + [row(o.shape[1]) for o in o_list] + [const(weights[0])]
                  + [row(p2d.shape[1])] + [const(w) for w in weights[1:]]),
        out_specs=row(d),
        out_shape=jax.ShapeDtypeStruct((m, d), F32),
        scratch_shapes=[pltpu.VMEM((TM_POST, d), F32)],
        compiler_params=_cparams(("parallel",)),
        name="out_mlp_ple",
    )(h2d, *o_list, weights[0], p2d, *weights[1:])


def _scale_q(w_in, q_cols):
    scale = jnp.where(jnp.asarray(q_cols), HEAD_DIM ** -0.5, 1.0).astype(F32)
    return (w_in * scale[None, :]).astype(BF16)


def kernel(x, p, t5_table, w_in_even, w_out_even, lambda_q1, lambda_k1, lambda_q2, lambda_k2, subln_g,
           w_in_odd, w_out_odd, norm_mix_g, norm_mlp_g, w_mlp_up, w_mlp_down, norm_ple_g, w_ple_gate,
           w_ple_proj, final_norm_g):
    b, s, d = x.shape
    depth = p.shape[0]
    assert d == D_MODEL and s % DT == 0 and (b * s) % TM_POST == 0
    assert all(w == DIL_WINDOW * r and w <= DT for w, r in DIL_BRANCHES)
    col = np.arange(3 * D_MODEL)
    even_q = (col < SB_WIDTH) | ((col >= 3 * SB_WIDTH) & (col < 3 * SB_WIDTH + DIFF_WIDTH))
    odd_q = col < DIL_HEADS * HEAD_DIM
    h = x.reshape(b * s, d)
    for i in range(depth):
        if i % 2 == 0:
            e = i // 2
            w_in = _scale_q(w_in_even[e], even_q)
            qb0, vb0 = 3 * SB_WIDTH, 3 * SB_WIDTH + 2 * DIFF_WIDTH
            w_t = jnp.concatenate([w_in[:, qb0:qb0 + DIFF_WIDTH], w_in[:, vb0:vb0 + DIFF_WIDTH]], axis=1).T
            proj, proj_t = _norm_matmul(h, norm_mix_g[i], w_in, w_t)
            proj = proj.reshape(b, s, -1)
            proj_t = proj_t.reshape(b, s // TD, 2 * DIFF_WIDTH, TD)
            lambda_init = 0.8 - 0.6 * math.exp(-0.3 * i)
            lam_params = jnp.stack([lambda_q1[e], lambda_k1[e], lambda_q2[e], lambda_k2[e]]).astype(F32)
            o_sb = _sb_attention(proj)
            o_d = _diff_attention(proj, proj_t, _diff_bias(t5_table), lam_params, subln_g[e].astype(F32),
                                  lambda_init)
            w_out = w_out_even[e].astype(BF16)
            o_list = [o_sb.reshape(b * s, SB_WIDTH), o_d.reshape(b * s, DIFF_WIDTH)]
        else:
            o = i // 2
            (proj,) = _norm_matmul(h, norm_mix_g[i], _scale_q(w_in_odd[o], odd_q))
            proj = proj.reshape(b, s, -1)
            o_list = [_dilated_attention(proj, t5_table).reshape(b * s, d)]
            w_out = w_out_odd[o].astype(BF16)
        h = _post(h, o_list, w_out, p[i].reshape(b * s, PLE_DIM), norm_mlp_g[i],
                  w_mlp_up[i].astype(BF16), w_mlp_down[i].astype(BF16), norm_ple_g[i],
                  w_ple_gate[i].astype(BF16), w_ple_proj[i].astype(BF16), final_norm_g,
                  final=(i == depth - 1))
    return h.reshape(b, s, d)
```

```python
import functools
import math

import numpy as np
import jax
import jax.numpy as jnp
from jax import lax
from jax.experimental import pallas as pl
from jax.experimental.pallas import tpu as pltpu

F32 = jnp.float32
BF16 = jnp.bfloat16

D_MODEL = 1024
HEAD_DIM = 64
LANES = 128
SB_HEADS = 8
DIFF_HEADS = 4
DIL_HEADS = 16
SB_WIDTH = SB_HEADS * HEAD_DIM
DIFF_WIDTH = DIFF_HEADS * 2 * HEAD_DIM
DIL_BRANCHES = ((128, 1), (512, 4), (2048, 16))
DIL_WINDOW = 128
NUM_BUCKETS = 32
MAX_DISTANCE = 128
D_FF = 4 * D_MODEL
PLE_DIM = 256
NORM_EPS = 1e-6
SUBLN_EPS = 1e-5
NEG = -1e30
LOG2E = math.log2(math.e)
SB_DEAD_LOG = 100.0

VMEM_LIMIT_BYTES = 56 * 1024 * 1024

TM_PROJ = 512
TM_POST = 512
FF_CHUNK = 1024
TQ = 256
TK = 256
TD = TM_PROJ
DT = 2048
DQ = DIL_WINDOW
DIL_UNROLL = 8


def _cparams(sem):
    return pltpu.CompilerParams(dimension_semantics=sem, vmem_limit_bytes=VMEM_LIMIT_BYTES)


def _rms(x, g, eps):
    return x * lax.rsqrt(jnp.mean(x * x, axis=-1, keepdims=True) + eps) * g


def _dot(a, b):
    return jnp.dot(a, b, preferred_element_type=F32)


def _dot_nt(a, b):
    return lax.dot_general(a, b, (((1,), (1,)), ((), ())), preferred_element_type=F32)


def _norm_matmul_kernel(h_ref, g_ref, w_ref, *rest):
    hn = _rms(h_ref[...], g_ref[...], NORM_EPS).astype(BF16)
    if len(rest) == 1:
        (o_ref,) = rest
    else:
        wt_ref, o_ref, ot_ref = rest
        ot_ref[0] = _dot_nt(wt_ref[...], hn).astype(ot_ref.dtype)
    o_ref[...] = _dot(hn, w_ref[...]).astype(o_ref.dtype)


def _norm_matmul(h2d, g, w, wt=None):
    m, k = h2d.shape
    n = w.shape[1]
    const = lambda a: pl.BlockSpec(a.shape, lambda i: (0,) * a.ndim, pipeline_mode=pl.Buffered(1))
    in_specs = [pl.BlockSpec((TM_PROJ, k), lambda i: (i, 0)), pl.BlockSpec((1, k), lambda i: (0, 0)), const(w)]
    out_specs = [pl.BlockSpec((TM_PROJ, n), lambda i: (i, 0))]
    out_shape = [jax.ShapeDtypeStruct((m, n), BF16)]
    args = [h2d, g.reshape(1, k), w]
    if wt is not None:
        nt = wt.shape[0]
        in_specs.append(const(wt))
        args.append(wt)
        out_specs.append(pl.BlockSpec((1, nt, TM_PROJ), lambda i: (i, 0, 0)))
        out_shape.append(jax.ShapeDtypeStruct((m // TM_PROJ, nt, TM_PROJ), BF16))
    return pl.pallas_call(
        _norm_matmul_kernel,
        grid=(m // TM_PROJ,),
        in_specs=in_specs, out_specs=out_specs, out_shape=out_shape,
        compiler_params=_cparams(("parallel",)),
        name="norm_in_proj",
    )(*args)


def _t5_bucket_np(dist):
    n = np.maximum(dist, 0)
    max_exact = NUM_BUCKETS // 2
    nf = np.maximum(n, 1).astype(np.float32)
    large = max_exact + (np.log(nf / np.float32(max_exact)) / np.float32(math.log(MAX_DISTANCE / max_exact))
                         * np.float32(NUM_BUCKETS - max_exact)).astype(np.int32)
    large = np.minimum(large, NUM_BUCKETS - 1)
    return np.where(n < max_exact, n, large).astype(np.int32)


def _bias_tile(table_cols, dist, valid):
    bucket = _t5_bucket_np(dist)
    out = jnp.zeros((table_cols.shape[1],) + dist.shape, F32)
    for bkt in np.unique(bucket[valid]):
        out = jnp.where(jnp.asarray(bucket == bkt)[None], table_cols[bkt][:, None, None], out)
    return jnp.where(jnp.asarray(valid)[None], out, NEG)


def _sb_kernel(q_ref, k_ref, v_ref, o_ref, acc_ref, c_ref):
    i = pl.program_id(2)
    q = q_ref[0]
    lane = lax.broadcasted_iota(jnp.int32, (1, LANES), 1)
    head0 = lane < HEAD_DIM
    zero = jnp.zeros_like(q)
    qs = (jnp.where(head0, q, zero), jnp.where(head0, zero, q))
    row = lax.broadcasted_iota(jnp.int32, (TK, TK), 0)
    col = lax.broadcasted_iota(jnp.int32, (TK, TK), 1)
    tri = (row >= col).astype(BF16)
    strict = col < row

    acc_ref[...] = jnp.zeros_like(acc_ref)
    c_ref[...] = jnp.zeros_like(c_ref)

    def tile(j, diag):
        ks = k_ref[0, pl.ds(pl.multiple_of(j * TK, TK), TK), :]
        vs = v_ref[0, pl.ds(pl.multiple_of(j * TK, TK), TK), :]
        vzero = jnp.zeros_like(vs)
        vsel = (jnp.where(head0, vs, vzero), jnp.where(head0, vzero, vs))
        upd = jnp.zeros((TQ, LANES), F32)
        alive = jnp.float32(-jnp.inf)
        for hh in range(2):
            z = _dot_nt(qs[hh], ks)
            lk = -(jnp.maximum(z, 0.0) + jnp.log1p(jnp.exp(-jnp.abs(z))))
            if diag:
                lk = jnp.where(strict, lk, 0.0)
            hi = lk.astype(BF16)
            lo = (lk - hi.astype(F32)).astype(BF16)
            incl = _dot(hi, tri) + _dot(lo, tri)
            c_old = c_ref[hh]
            w = jnp.exp(z + incl + c_old)
            if diag:
                w = jnp.where(strict, w, 0.0)
            upd = upd + _dot(w.astype(BF16), vsel[hh])
            c_new = c_old + incl[:, 0:1]
            c_ref[hh] = c_new
            alive = jnp.maximum(alive, jnp.max(c_new))
        acc_ref[...] += upd
        return alive

    alive0 = tile(i, True)

    def cond(state):
        j, alive = state
        return jnp.logical_and(j >= 0, alive > -SB_DEAD_LOG)

    def body(state):
        j, _ = state
        return j - 1, tile(j, False)

    lax.while_loop(cond, body, (i - 1, alive0))
    o_ref[0] = acc_ref[...].astype(o_ref.dtype)


def _sb_attention(proj3):
    b, s, _ = proj3.shape
    pairs = SB_WIDTH // LANES
    return pl.pallas_call(
        _sb_kernel,
        grid=(b, pairs, s // TQ),
        in_specs=[pl.BlockSpec((1, TQ, LANES), lambda bb, hp, i: (bb, i, hp)),
                  pl.BlockSpec((1, s, LANES), lambda bb, hp, i: (bb, 0, pairs + hp)),
                  pl.BlockSpec((1, s, LANES), lambda bb, hp, i: (bb, 0, 2 * pairs + hp))],
        out_specs=pl.BlockSpec((1, TQ, LANES), lambda bb, hp, i: (bb, i, hp)),
        out_shape=jax.ShapeDtypeStruct((b, s, SB_WIDTH), BF16),
        scratch_shapes=[pltpu.VMEM((TQ, LANES), F32), pltpu.VMEM((2, TQ, 1), F32)],
        compiler_params=_cparams(("parallel", "parallel", "arbitrary")),
        name="sb_attention",
    )(proj3, proj3, proj3)


def _diff_kernel(qt_ref, k_ref, vt_ref, bias_ref, lam_ref, g_ref, o_ref, m_ref, l_ref, acc_ref, sa_ref, sb_ref,
                 *, lambda_init):
    i = pl.program_id(2)
    qt = qt_ref[0, 0]
    row = lax.broadcasted_iota(jnp.int32, (LANES, 1), 0)
    map0 = row < HEAD_DIM
    zero = jnp.zeros_like(qt)
    qts = (jnp.where(map0, qt, zero), jnp.where(map0, zero, qt))

    m_ref[...] = jnp.full_like(m_ref, NEG)
    l_ref[...] = jnp.zeros_like(l_ref)
    acc_ref[...] = jnp.zeros_like(acc_ref)

    def scores(j):
        ks = k_ref[0, pl.ds(pl.multiple_of(j * TD, TD), TD), :]
        return tuple(_dot(ks, qts[mm]) for mm in range(2))

    def absorb(j, s_ref, bias_idx, offset=None):
        vts = vt_ref[0, j]
        for mm in range(2):
            s = s_ref[mm]
            if bias_idx is not None:
                s = s + bias_ref[0, mm, bias_idx]
            if offset is not None:
                s = s + offset
            m_old = m_ref[mm]
            m_new = jnp.maximum(m_old, jnp.max(s, axis=0, keepdims=True))
            alpha = jnp.exp2(m_old - m_new)
            p = jnp.exp2(s - m_new)
            l_ref[mm] = alpha * l_ref[mm] + jnp.sum(p, axis=0, keepdims=True)
            acc_ref[mm] = alpha * acc_ref[mm] + _dot(vts, p.astype(BF16))
            m_ref[mm] = m_new

    def stage(s_ref, j):
        s_pair = scores(j)
        for mm in range(2):
            s_ref[mm] = s_pair[mm]

    stage(sa_ref, i)
    stage(sb_ref, jnp.maximum(i - 1, 0))
    absorb(i, sa_ref, 0)
    stage(sa_ref, jnp.maximum(i - 2, 0))
    absorb(jnp.maximum(i - 1, 0), sb_ref, 1, offset=jnp.where(i >= 1, 0.0, NEG))

    @pl.when(i >= 2)
    def _():
        def body(u, carry):
            ja = i - 2 - 2 * u
            stage(sb_ref, ja - 1)
            absorb(ja, sa_ref, None)
            stage(sa_ref, jnp.maximum(ja - 2, 0))
            absorb(ja - 1, sb_ref, None)
            return carry

        lax.fori_loop(0, (i - 1) // 2, body, 0)

        @pl.when((i - 1) % 2 == 1)
        def _():
            absorb(0, sa_ref, None)

    lam = (jnp.exp(jnp.sum(lam_ref[0:1, :] * lam_ref[1:2, :], axis=1, keepdims=True))
           - jnp.exp(jnp.sum(lam_ref[2:3, :] * lam_ref[3:4, :], axis=1, keepdims=True)) + lambda_init)
    ot = acc_ref[0] / l_ref[0] - lam * (acc_ref[1] / l_ref[1])
    o = jnp.transpose(ot)
    o_ref[0] = (_rms(o, g_ref[...], SUBLN_EPS) * (1.0 - lambda_init)).astype(o_ref.dtype)


def _diff_attention(proj3, proj_t, bias, lam_params, subln_g, lambda_init):
    b, s, _ = proj3.shape
    kbase = (3 * SB_WIDTH + DIFF_WIDTH) // LANES
    nt = s // TD
    return pl.pallas_call(
        functools.partial(_diff_kernel, lambda_init=lambda_init),
        grid=(b, DIFF_HEADS, nt),
        in_specs=[pl.BlockSpec((1, 1, LANES, TD), lambda bb, h, i: (bb, i, h, 0)),
                  pl.BlockSpec((1, s, LANES), lambda bb, h, i: (bb, 0, kbase + h)),
                  pl.BlockSpec((1, nt, LANES, TD), lambda bb, h, i: (bb, 0, DIFF_HEADS + h, 0)),
                  pl.BlockSpec((1, 2, 2, TD, TD), lambda bb, h, i: (h, 0, 0, 0, 0)),
                  pl.BlockSpec((4, HEAD_DIM), lambda bb, h, i: (0, 0)),
                  pl.BlockSpec((1, LANES), lambda bb, h, i: (0, 0))],
        out_specs=pl.BlockSpec((1, TD, LANES), lambda bb, h, i: (bb, i, h)),
        out_shape=jax.ShapeDtypeStruct((b, s, DIFF_WIDTH), BF16),
        scratch_shapes=[pltpu.VMEM((2, 1, TD), F32), pltpu.VMEM((2, 1, TD), F32),
                        pltpu.VMEM((2, LANES, TD), F32),
                        pltpu.VMEM((2, TD, TD), F32), pltpu.VMEM((2, TD, TD), F32)],
        compiler_params=_cparams(("parallel", "parallel", "arbitrary")),
        name="diff_attention",
    )(proj_t, proj3, proj_t, bias, lam_params, subln_g.reshape(1, LANES))


def _diff_bias(t5_table):
    cols = t5_table.astype(F32)[:, SB_HEADS:]
    cols = (cols - cols[NUM_BUCKETS - 1:NUM_BUCKETS, :]) * LOG2E
    nb = TD // MAX_DISTANCE
    ki = np.arange(MAX_DISTANCE)[:, None]
    qi = np.arange(MAX_DISTANCE)[None, :]
    d0 = qi - ki
    diag = _bias_tile(cols, d0, d0 >= 0)
    prev = _bias_tile(cols, d0 + MAX_DISTANCE, np.ones_like(d0, bool))
    zero = jnp.zeros_like(diag)
    masked = jnp.full_like(diag, NEG)

    def block0(kb, qb):
        return masked if kb > qb else diag if kb == qb else prev if kb == qb - 1 else zero

    t0 = jnp.concatenate([jnp.concatenate([block0(kb, qb) for qb in range(nb)], axis=2) for kb in range(nb)], axis=1)
    t1 = jnp.concatenate([jnp.concatenate([prev if (kb == nb - 1 and qb == 0) else zero for qb in range(nb)], axis=2)
                          for kb in range(nb)], axis=1)
    both = jnp.stack([t0, t1], axis=1)
    return both.reshape(DIFF_HEADS, 2, 2, TD, TD)


def _dil_kernel(q_ref, kp_ref, kc_ref, vp_ref, vc_ref, bias_ref, o_ref, qf, kf, vf, acc_s, m_s, l_s):
    t_idx = pl.program_id(2)
    qf[...] = q_ref[0].astype(F32)
    kf[0:DT, :] = kp_ref[0].astype(F32)
    kf[DT:2 * DT, :] = kc_ref[0].astype(F32)
    vf[0:DT, :] = vp_ref[0].astype(F32)
    vf[DT:2 * DT, :] = vc_ref[0].astype(F32)
    lane = lax.broadcasted_iota(jnp.int32, (1, LANES), 1)
    head0 = lane < HEAD_DIM
    heads = (head0, jnp.logical_not(head0))
    before_tile = lax.broadcasted_iota(jnp.int32, (1, 2 * DQ), 1) < DQ

    for g, (_, r) in enumerate(DIL_BRANCHES):
        nblk = DT // (r * DQ)
        shift = nblk.bit_length() - 1

        def block(t, carry, g=g, r=r, nblk=nblk, shift=shift):
            c = lax.shift_right_logical(t, shift)
            n = jnp.bitwise_and(t, nblk - 1)
            q0 = c + n * (DQ * r)
            k0 = DT + q0 - DQ * r
            qg = qf[pl.ds(q0, DQ, stride=r), :].astype(BF16)
            kg = kf[pl.ds(k0, 2 * DQ, stride=r), :].astype(BF16)
            vg = vf[pl.ds(k0, 2 * DQ, stride=r), :]
            off = jnp.where(jnp.logical_and(t_idx == 0, n == 0), NEG, 0.0)
            pre = jnp.where(before_tile, off, 0.0)
            res, mx = [], []
            for hh in range(2):
                qh = jnp.where(heads[hh], qg, jnp.zeros_like(qg))
                s = _dot_nt(qh, kg) + bias_ref[0, hh, g] + pre
                m = jnp.max(s, axis=1, keepdims=True)
                p = jnp.exp(s - m)
                vh = jnp.where(heads[hh], vg, 1.0).astype(BF16)
                res.append(_dot(p.astype(BF16), vh))
                mx.append(jnp.broadcast_to(m, (DQ, LANES)))
            rows = pl.ds(q0, DQ, stride=r)
            acc_s[g, rows, :] = jnp.where(head0, res[0], res[1])
            m_s[g, rows, :] = jnp.where(head0, mx[0], mx[1])
            l_s[g, rows, :] = pltpu.roll(jnp.where(head0, res[1], res[0]), HEAD_DIM, 1)
            return carry

        lax.fori_loop(0, r * nblk, block, 0, unroll=DIL_UNROLL)

    def merge(ci, carry):
        rows = pl.ds(pl.multiple_of(ci * TQ, TQ), TQ)
        ms = [m_s[g, rows, :] for g in range(len(DIL_BRANCHES))]
        m = functools.reduce(jnp.maximum, ms)
        num = jnp.zeros((TQ, LANES), F32)
        den = jnp.zeros((TQ, LANES), F32)
        for g in range(len(DIL_BRANCHES)):
            wgt = jnp.exp(ms[g] - m)
            num = num + wgt * acc_s[g, rows, :]
            den = den + wgt * l_s[g, rows, :]
        o_ref[0, rows, :] = (num / den).astype(o_ref.dtype)
        return carry

    lax.fori_loop(0, DT // TQ, merge, 0)


def _dil_bias(t5_table):
    qi = np.arange(DQ)[:, None]
    ki = np.arange(2 * DQ)[None, :] - DQ
    steps = qi - ki
    valid = (steps >= 0) & (steps <= DIL_WINDOW)
    tiles = [_bias_tile(t5_table.astype(F32), steps * r, valid) for _, r in DIL_BRANCHES]
    return jnp.stack(tiles, axis=1).reshape(DIL_HEADS // 2, 2, len(DIL_BRANCHES), DQ, 2 * DQ)


def _dilated_attention(proj3, t5_table):
    b, s, _ = proj3.shape
    pairs = DIL_HEADS // 2
    bias = _dil_bias(t5_table)
    cur = lambda off: (lambda bb, hp, t: (bb, t, off * pairs + hp))
    prev = lambda off: (lambda bb, hp, t: (bb, jnp.maximum(t - 1, 0), off * pairs + hp))
    tile = lambda imap: pl.BlockSpec((1, DT, LANES), imap)
    slab = lambda rows: pltpu.VMEM((rows, LANES), F32)
    per_branch = pltpu.VMEM((len(DIL_BRANCHES), DT, LANES), F32)
    return pl.pallas_call(
        _dil_kernel,
        grid=(b, pairs, s // DT),
        in_specs=[tile(cur(0)), tile(prev(1)), tile(cur(1)), tile(prev(2)), tile(cur(2)),
                  pl.BlockSpec((1, 2, len(DIL_BRANCHES), DQ, 2 * DQ), lambda bb, hp, t: (hp, 0, 0, 0, 0))],
        out_specs=tile(cur(0)),
        out_shape=jax.ShapeDtypeStruct((b, s, D_MODEL), BF16),
        scratch_shapes=[slab(DT), slab(2 * DT), slab(2 * DT), per_branch, per_branch, per_branch],
        compiler_params=_cparams(("parallel", "parallel", "arbitrary")),
        name="dilated_attention",
    )(proj3, proj3, proj3, proj3, proj3, bias)


def _post_kernel(*refs, n_o, final):
    h_ref = refs[0]
    o_refs = refs[1:1 + n_o]
    (wo_ref, p_ref, g_mlp_ref, w_up_ref, w_down_ref, g_ple_ref, w_gate_ref, w_proj_ref, g_fin_ref,
     out_ref, acc_ref) = refs[1 + n_o:]
    o = o_refs[0][...] if n_o == 1 else jnp.concatenate([r[...] for r in o_refs], axis=1)
    h = h_ref[...] + _dot(o, wo_ref[...])
    hn = _rms(h, g_mlp_ref[...], NORM_EPS).astype(BF16)
    acc_ref[...] = h
    for c in range(D_FF // FF_CHUNK):
        u = jnp.maximum(_dot(hn, w_up_ref[:, c * FF_CHUNK:(c + 1) * FF_CHUNK]), 0.0)
        acc_ref[...] += _dot((u * u).astype(BF16), w_down_ref[c * FF_CHUNK:(c + 1) * FF_CHUNK, :])
    h = acc_ref[...]
    gate = jax.nn.sigmoid(_dot(_rms(h, g_ple_ref[...], NORM_EPS).astype(BF16), w_gate_ref[...]))
    h = h + _dot(p_ref[...].astype(BF16), w_proj_ref[...]) * gate
    if final:
        h = _rms(h, g_fin_ref[...], NORM_EPS)
    out_ref[...] = h


def _post(h2d, o_list, w_out, p2d, g_mlp, w_up, w_down, g_ple, w_gate, w_proj, g_fin, final):
    m, d = h2d.shape
    n_o = len(o_list)
    row = lambda width: pl.BlockSpec((TM_POST, width), lambda i: (i, 0))
    const = lambda a: pl.BlockSpec(a.shape, lambda i: (0,) * a.ndim, pipeline_mode=pl.Buffered(1))
    vec = lambda g: g.reshape(1, d).astype(F32)
    weights = [w_out, vec(g_mlp), w_up, w_down, vec(g_ple), w_gate, w_proj, vec(g_fin)]
    return pl.pallas_call(
        functools.partial(_post_kernel, n_o=n_o, final=final),
        grid=(m // TM_POST,),
        in_specs=([row(d)] + [row(o.shape[1]) for o in o_list] + [const(weights[0])]
                  + [row(p2d.shape[1])] + [const(w) for w in weights[1:]]),
        out_specs=row(d),
        out_shape=jax.ShapeDtypeStruct((m, d), F32),
        scratch_shapes=[pltpu.VMEM((TM_POST, d), F32)],
        compiler_params=_cparams(("parallel",)),
        name="out_mlp_ple",
    )(h2d, *o_list, weights[0], p2d, *weights[1:])


def _scale_cols(w_in, col_scale):
    return (w_in * jnp.asarray(col_scale, F32)[None, :]).astype(BF16)


def kernel(x, p, t5_table, w_in_even, w_out_even, lambda_q1, lambda_k1, lambda_q2, lambda_k2, subln_g,
           w_in_odd, w_out_odd, norm_mix_g, norm_mlp_g, w_mlp_up, w_mlp_down, norm_ple_g, w_ple_gate,
           w_ple_proj, final_norm_g):
    b, s, d = x.shape
    depth = p.shape[0]
    assert d == D_MODEL and s % DT == 0 and (b * s) % TM_POST == 0
    assert all(w == DIL_WINDOW * r and w <= DT for w, r in DIL_BRANCHES)
    col = np.arange(3 * D_MODEL)
    qk_scale = HEAD_DIM ** -0.5
    even_scale = np.where(col < SB_WIDTH, qk_scale, 1.0)
    even_scale = np.where((col >= 3 * SB_WIDTH) & (col < 3 * SB_WIDTH + DIFF_WIDTH), qk_scale * LOG2E, even_scale)
    odd_scale = np.where(col < DIL_HEADS * HEAD_DIM, qk_scale, 1.0)
    h = x.reshape(b * s, d)
    for i in range(depth):
        if i % 2 == 0:
            e = i // 2
            w_in = _scale_cols(w_in_even[e], even_scale)
            qb0, vb0 = 3 * SB_WIDTH, 3 * SB_WIDTH + 2 * DIFF_WIDTH
            w_t = jnp.concatenate([w_in[:, qb0:qb0 + DIFF_WIDTH], w_in[:, vb0:vb0 + DIFF_WIDTH]], axis=1).T
            proj, proj_t = _norm_matmul(h, norm_mix_g[i], w_in, w_t)
            proj = proj.reshape(b, s, -1)
            proj_t = proj_t.reshape(b, s // TD, 2 * DIFF_WIDTH, TD)
            lambda_init = 0.8 - 0.6 * math.exp(-0.3 * i)
            lam_params = jnp.stack([lambda_q1[e], lambda_k1[e], lambda_q2[e], lambda_k2[e]]).astype(F32)
            o_sb = _sb_attention(proj)
            o_d = _diff_attention(proj, proj_t, _diff_bias(t5_table), lam_params, subln_g[e].astype(F32),
                                  lambda_init)
            w_out = w_out_even[e].astype(BF16)
            o_list = [o_sb.reshape(b * s, SB_WIDTH), o_d.reshape(b * s, DIFF_WIDTH)]
        else:
            o = i // 2
            (proj,) = _norm_matmul(h, norm_mix_g[i], _scale_cols(w_in_odd[o], odd_scale))
            proj = proj.reshape(b, s, -1)
            o_list = [_dilated_attention(proj, t5_table).reshape(b * s, d)]
            w_out = w_out_odd[o].astype(BF16)
        h = _post(h, o_list, w_out, p[i].reshape(b * s, PLE_DIM), norm_mlp_g[i],
                  w_mlp_up[i].astype(BF16), w_mlp_down[i].astype(BF16), norm_ple_g[i],
                  w_ple_gate[i].astype(BF16), w_ple_proj[i].astype(BF16), final_norm_g,
                  final=(i == depth - 1))
    return h.reshape(b, s, d)
```

```python
import functools
import math

import numpy as np
import jax
import jax.numpy as jnp
from jax import lax
from jax.experimental import pallas as pl
from jax.experimental.pallas import tpu as pltpu

F32 = jnp.float32
BF16 = jnp.bfloat16

D_MODEL = 1024
HEAD_DIM = 64
LANES = 128
SB_HEADS = 8
DIFF_HEADS = 4
DIL_HEADS = 16
SB_WIDTH = SB_HEADS * HEAD_DIM
DIFF_WIDTH = DIFF_HEADS * 2 * HEAD_DIM
DIL_BRANCHES = ((128, 1), (512, 4), (2048, 16))
DIL_WINDOW = 128
NUM_BUCKETS = 32
MAX_DISTANCE = 128
D_FF = 4 * D_MODEL
PLE_DIM = 256
NORM_EPS = 1e-6
SUBLN_EPS = 1e-5
NEG = -1e30
LOG2E = math.log2(math.e)
SB_DEAD_LOG = 100.0

VMEM_LIMIT_BYTES = 56 * 1024 * 1024

TM_PROJ = 512
TM_POST = 512
FF_CHUNK = 1024
TQ = 256
TS = 256
TD = TM_PROJ
DT = 2048
DQ = DIL_WINDOW
DIL_UNROLL = 8


def _cparams(sem):
    return pltpu.CompilerParams(dimension_semantics=sem, vmem_limit_bytes=VMEM_LIMIT_BYTES)


def _rms(x, g, eps):
    return x * lax.rsqrt(jnp.mean(x * x, axis=-1, keepdims=True) + eps) * g


def _dot(a, b):
    return jnp.dot(a, b, preferred_element_type=F32)


def _dot_nt(a, b):
    return lax.dot_general(a, b, (((1,), (1,)), ((), ())), preferred_element_type=F32)


def _norm_matmul_kernel(h_ref, g_ref, w_ref, *rest, slabs):
    hn = _rms(h_ref[...], g_ref[...], NORM_EPS).astype(BF16)
    n_t = len(slabs)
    wt_refs, o_ref, ot_refs = rest[:n_t], rest[n_t], rest[n_t + 1:]
    for wt_ref, ot_ref, width in zip(wt_refs, ot_refs, slabs):
        res = _dot_nt(wt_ref[...], hn).astype(ot_ref.dtype)
        for c in range(TM_PROJ // width):
            ot_ref[c] = res[:, c * width:(c + 1) * width]
    o_ref[...] = _dot(hn, w_ref[...]).astype(o_ref.dtype)


def _norm_matmul(h2d, g, w, wts=()):
    m, k = h2d.shape
    n = w.shape[1]
    const = lambda a: pl.BlockSpec(a.shape, lambda i: (0,) * a.ndim, pipeline_mode=pl.Buffered(1))
    in_specs = [pl.BlockSpec((TM_PROJ, k), lambda i: (i, 0)), pl.BlockSpec((1, k), lambda i: (0, 0)), const(w)]
    in_specs += [const(wt) for wt, _ in wts]
    out_specs = [pl.BlockSpec((TM_PROJ, n), lambda i: (i, 0))]
    out_shape = [jax.ShapeDtypeStruct((m, n), BF16)]
    for wt, width in wts:
        per_step = TM_PROJ // width
        out_specs.append(pl.BlockSpec((per_step, wt.shape[0], width), lambda i: (i, 0, 0)))
        out_shape.append(jax.ShapeDtypeStruct((m // width, wt.shape[0], width), BF16))
    return pl.pallas_call(
        functools.partial(_norm_matmul_kernel, slabs=tuple(width for _, width in wts)),
        grid=(m // TM_PROJ,),
        in_specs=in_specs, out_specs=out_specs, out_shape=out_shape,
        compiler_params=_cparams(("parallel",)),
        name="norm_in_proj",
    )(h2d, g.reshape(1, k), w, *[wt for wt, _ in wts])


def _t5_bucket_np(dist):
    n = np.maximum(dist, 0)
    max_exact = NUM_BUCKETS // 2
    nf = np.maximum(n, 1).astype(np.float32)
    large = max_exact + (np.log(nf / np.float32(max_exact)) / np.float32(math.log(MAX_DISTANCE / max_exact))
                         * np.float32(NUM_BUCKETS - max_exact)).astype(np.int32)
    large = np.minimum(large, NUM_BUCKETS - 1)
    return np.where(n < max_exact, n, large).astype(np.int32)


def _bias_tile(table_cols, dist, valid):
    bucket = _t5_bucket_np(dist)
    out = jnp.zeros((table_cols.shape[1],) + dist.shape, F32)
    for bkt in np.unique(bucket[valid]):
        out = jnp.where(jnp.asarray(bucket == bkt)[None], table_cols[bkt][:, None, None], out)
    return jnp.where(jnp.asarray(valid)[None], out, NEG)


def _sb_kernel(qt_ref, k_ref, vt_ref, o_ref, acc_ref, c_ref):
    i = pl.program_id(2)
    qt = qt_ref[0, 0]
    row = lax.broadcasted_iota(jnp.int32, (LANES, 1), 0)
    head_rows = (row < HEAD_DIM, row >= HEAD_DIM)
    qts = tuple(jnp.where(hr, qt, jnp.zeros_like(qt)) for hr in head_rows)
    kk = lax.broadcasted_iota(jnp.int32, (TS, TS), 0)
    qq = lax.broadcasted_iota(jnp.int32, (TS, TS), 1)
    tri = (kk <= qq).astype(BF16)
    strict = kk < qq

    def tiles(js, keeps, c_in):
        items = [(t, hh) for t in range(len(js)) for hh in range(2)]
        ks = [k_ref[0, pl.ds(pl.multiple_of(j * TS, TS), TS), :] for j in js]
        z = {it: _dot(ks[it[0]], qts[it[1]]) for it in items}
        parts = {}
        for it in items:
            lk = -(jnp.maximum(z[it], 0.0) + jnp.log2(1.0 + jnp.exp2(-jnp.abs(z[it]))))
            if keeps[it[0]] is not None:
                lk = jnp.where(keeps[it[0]], lk, 0.0)
            hi = lk.astype(BF16)
            parts[it] = (hi, (lk - hi.astype(F32)).astype(BF16))
        incl = {it: _dot(tri, parts[it][0]) + _dot(tri, parts[it][1]) for it in items}
        c = list(c_in)
        w = {}
        for it in items:
            t, hh = it
            wt = jnp.exp2(z[it] + incl[it] + c[hh])
            if keeps[t] is not None:
                wt = jnp.where(keeps[t], wt, 0.0)
            w[it] = wt.astype(BF16)
            c[hh] = c[hh] + incl[it][0:1, :]
        upd = jnp.zeros((LANES, TS), F32)
        for t, hh in items:
            vts = vt_ref[0, js[t]]
            vth = jnp.where(head_rows[hh], vts, jnp.zeros_like(vts))
            upd = upd + _dot(vth, w[(t, hh)])
        return upd, c

    zero_c = jnp.zeros((1, TS), F32)
    upd, c1 = tiles([i, jnp.maximum(i - 1, 0)], [strict, kk < jnp.where(i >= 1, TS, 0)], [zero_c, zero_c])
    acc_ref[...] = upd
    for hh in range(2):
        c_ref[hh] = c1[hh]
    alive1 = jnp.maximum(jnp.max(c1[0]), jnp.max(c1[1]))

    def cond(state):
        j, alive = state
        return jnp.logical_and(j >= 0, alive > -SB_DEAD_LOG * LOG2E)

    def body(state):
        j, _ = state
        upd, c_new = tiles([j], [None], [c_ref[0], c_ref[1]])
        acc_ref[...] += upd
        for hh in range(2):
            c_ref[hh] = c_new[hh]
        return j - 1, jnp.maximum(jnp.max(c_new[0]), jnp.max(c_new[1]))

    lax.while_loop(cond, body, (i - 2, alive1))
    o_ref[0] = jnp.transpose(acc_ref[...]).astype(o_ref.dtype)


def _sb_attention(proj3, proj_t):
    b, s, _ = proj3.shape
    pairs = SB_WIDTH // LANES
    nt = s // TS
    return pl.pallas_call(
        _sb_kernel,
        grid=(b, pairs, nt),
        in_specs=[pl.BlockSpec((1, 1, LANES, TS), lambda bb, hp, i: (bb, i, hp, 0)),
                  pl.BlockSpec((1, s, LANES), lambda bb, hp, i: (bb, 0, hp)),
                  pl.BlockSpec((1, nt, LANES, TS), lambda bb, hp, i: (bb, 0, pairs + hp, 0))],
        out_specs=pl.BlockSpec((1, TS, LANES), lambda bb, hp, i: (bb, i, hp)),
        out_shape=jax.ShapeDtypeStruct((b, s, SB_WIDTH), BF16),
        scratch_shapes=[pltpu.VMEM((LANES, TS), F32), pltpu.VMEM((2, 1, TS), F32)],
        compiler_params=_cparams(("parallel", "parallel", "arbitrary")),
        name="sb_attention",
    )(proj_t, proj3, proj_t)


def _diff_kernel(qt_ref, k_ref, vt_ref, bias_ref, lam_ref, g_ref, o_ref, m_ref, l_ref, acc_ref, sa_ref, sb_ref,
                 *, lambda_init):
    i = pl.program_id(2)
    qt = qt_ref[0, 0]
    row = lax.broadcasted_iota(jnp.int32, (LANES, 1), 0)
    map0 = row < HEAD_DIM
    zero = jnp.zeros_like(qt)
    qts = (jnp.where(map0, qt, zero), jnp.where(map0, zero, qt))

    m_ref[...] = jnp.full_like(m_ref, NEG)
    l_ref[...] = jnp.zeros_like(l_ref)
    acc_ref[...] = jnp.zeros_like(acc_ref)

    def scores(j):
        ks = k_ref[0, pl.ds(pl.multiple_of(j * TD, TD), TD), :]
        return tuple(_dot(ks, qts[mm]) for mm in range(2))

    def absorb(j, s_ref, bias_idx, offset=None):
        vts = vt_ref[0, j]
        for mm in range(2):
            s = s_ref[mm]
            if bias_idx is not None:
                s = s + bias_ref[0, mm, bias_idx]
            if offset is not None:
                s = s + offset
            m_old = m_ref[mm]
            m_new = jnp.maximum(m_old, jnp.max(s, axis=0, keepdims=True))
            alpha = jnp.exp2(m_old - m_new)
            p = jnp.exp2(s - m_new)
            l_ref[mm] = alpha * l_ref[mm] + jnp.sum(p, axis=0, keepdims=True)
            acc_ref[mm] = alpha * acc_ref[mm] + _dot(vts, p.astype(BF16))
            m_ref[mm] = m_new

    def stage(s_ref, j):
        s_pair = scores(j)
        for mm in range(2):
            s_ref[mm] = s_pair[mm]

    stage(sa_ref, i)
    stage(sb_ref, jnp.maximum(i - 1, 0))
    absorb(i, sa_ref, 0)
    stage(sa_ref, jnp.maximum(i - 2, 0))
    absorb(jnp.maximum(i - 1, 0), sb_ref, 1, offset=jnp.where(i >= 1, 0.0, NEG))

    @pl.when(i >= 2)
    def _():
        def body(u, carry):
            ja = i - 2 - 2 * u
            stage(sb_ref, ja - 1)
            absorb(ja, sa_ref, None)
            stage(sa_ref, jnp.maximum(ja - 2, 0))
            absorb(ja - 1, sb_ref, None)
            return carry

        lax.fori_loop(0, (i - 1) // 2, body, 0)

        @pl.when((i - 1) % 2 == 1)
        def _():
            absorb(0, sa_ref, None)

    lam = (jnp.exp(jnp.sum(lam_ref[0:1, :] * lam_ref[1:2, :], axis=1, keepdims=True))
           - jnp.exp(jnp.sum(lam_ref[2:3, :] * lam_ref[3:4, :], axis=1, keepdims=True)) + lambda_init)
    ot = acc_ref[0] / l_ref[0] - lam * (acc_ref[1] / l_ref[1])
    o = jnp.transpose(ot)
    o_ref[0] = (_rms(o, g_ref[...], SUBLN_EPS) * (1.0 - lambda_init)).astype(o_ref.dtype)


def _diff_attention(proj3, proj_t, bias, lam_params, subln_g, lambda_init):
    b, s, _ = proj3.shape
    kbase = SB_WIDTH // LANES
    nt = s // TD
    return pl.pallas_call(
        functools.partial(_diff_kernel, lambda_init=lambda_init),
        grid=(b, DIFF_HEADS, nt),
        in_specs=[pl.BlockSpec((1, 1, LANES, TD), lambda bb, h, i: (bb, i, h, 0)),
                  pl.BlockSpec((1, s, LANES), lambda bb, h, i: (bb, 0, kbase + h)),
                  pl.BlockSpec((1, nt, LANES, TD), lambda bb, h, i: (bb, 0, DIFF_HEADS + h, 0)),
                  pl.BlockSpec((1, 2, 2, TD, TD), lambda bb, h, i: (h, 0, 0, 0, 0)),
                  pl.BlockSpec((4, HEAD_DIM), lambda bb, h, i: (0, 0)),
                  pl.BlockSpec((1, LANES), lambda bb, h, i: (0, 0))],
        out_specs=pl.BlockSpec((1, TD, LANES), lambda bb, h, i: (bb, i, h)),
        out_shape=jax.ShapeDtypeStruct((b, s, DIFF_WIDTH), BF16),
        scratch_shapes=[pltpu.VMEM((2, 1, TD), F32), pltpu.VMEM((2, 1, TD), F32),
                        pltpu.VMEM((2, LANES, TD), F32),
                        pltpu.VMEM((2, TD, TD), F32), pltpu.VMEM((2, TD, TD), F32)],
        compiler_params=_cparams(("parallel", "parallel", "arbitrary")),
        name="diff_attention",
    )(proj_t, proj3, proj_t, bias, lam_params, subln_g.reshape(1, LANES))


def _diff_bias(t5_table):
    cols = t5_table.astype(F32)[:, SB_HEADS:]
    cols = (cols - cols[NUM_BUCKETS - 1:NUM_BUCKETS, :]) * LOG2E
    nb = TD // MAX_DISTANCE
    ki = np.arange(MAX_DISTANCE)[:, None]
    qi = np.arange(MAX_DISTANCE)[None, :]
    d0 = qi - ki
    diag = _bias_tile(cols, d0, d0 >= 0)
    prev = _bias_tile(cols, d0 + MAX_DISTANCE, np.ones_like(d0, bool))
    zero = jnp.zeros_like(diag)
    masked = jnp.full_like(diag, NEG)

    def block0(kb, qb):
        return masked if kb > qb else diag if kb == qb else prev if kb == qb - 1 else zero

    t0 = jnp.concatenate([jnp.concatenate([block0(kb, qb) for qb in range(nb)], axis=2) for kb in range(nb)], axis=1)
    t1 = jnp.concatenate([jnp.concatenate([prev if (kb == nb - 1 and qb == 0) else zero for qb in range(nb)], axis=2)
                          for kb in range(nb)], axis=1)
    both = jnp.stack([t0, t1], axis=1)
    return both.reshape(DIFF_HEADS, 2, 2, TD, TD)


def _dil_kernel(q_ref, kp_ref, kc_ref, vp_ref, vc_ref, bias_ref, o_ref, qf, kf, vf, acc_s, m_s, l_s):
    t_idx = pl.program_id(2)
    qf[...] = q_ref[0].astype(F32)
    kf[0:DT, :] = kp_ref[0].astype(F32)
    kf[DT:2 * DT, :] = kc_ref[0].astype(F32)
    vf[0:DT, :] = vp_ref[0].astype(F32)
    vf[DT:2 * DT, :] = vc_ref[0].astype(F32)
    lane = lax.broadcasted_iota(jnp.int32, (1, LANES), 1)
    head0 = lane < HEAD_DIM
    heads = (head0, jnp.logical_not(head0))
    before_tile = lax.broadcasted_iota(jnp.int32, (1, 2 * DQ), 1) < DQ

    for g, (_, r) in enumerate(DIL_BRANCHES):
        nblk = DT // (r * DQ)
        shift = nblk.bit_length() - 1

        def group(t0, carry, g=g, r=r, nblk=nblk, shift=shift):
            rows, qh, kg, vh, pre = [], [], [], [], []
            for u in range(DIL_UNROLL):
                t = t0 * DIL_UNROLL + u
                c = lax.shift_right_logical(t, shift)
                n = jnp.bitwise_and(t, nblk - 1)
                q0 = c + n * (DQ * r)
                k0 = DT + q0 - DQ * r
                rows.append(pl.ds(q0, DQ, stride=r))
                qg = qf[rows[u], :].astype(BF16)
                qh.append([jnp.where(hd, qg, jnp.zeros_like(qg)) for hd in heads])
                kg.append(kf[pl.ds(k0, 2 * DQ, stride=r), :].astype(BF16))
                vg = vf[pl.ds(k0, 2 * DQ, stride=r), :].astype(BF16)
                vh.append([jnp.where(hd, vg, jnp.ones_like(vg)) for hd in heads])
                off = jnp.where(jnp.logical_and(t_idx == 0, n == 0), NEG, 0.0)
                pre.append(jnp.where(before_tile, off, 0.0))
            items = [(u, hh) for u in range(DIL_UNROLL) for hh in range(2)]
            s = {it: _dot_nt(qh[it[0]][it[1]], kg[it[0]]) for it in items}
            p, mx = {}, {}
            for it in items:
                sb = s[it] + bias_ref[0, it[1], g] + pre[it[0]]
                m = jnp.max(sb, axis=1, keepdims=True)
                p[it] = jnp.exp2(sb - m).astype(BF16)
                mx[it] = jnp.broadcast_to(m, (DQ, LANES))
            res = {it: _dot(p[it], vh[it[0]][it[1]]) for it in items}
            for u in range(DIL_UNROLL):
                acc_s[g, rows[u], :] = jnp.where(head0, res[(u, 0)], res[(u, 1)])
                m_s[g, rows[u], :] = jnp.where(head0, mx[(u, 0)], mx[(u, 1)])
                l_s[g, rows[u], :] = pltpu.roll(jnp.where(head0, res[(u, 1)], res[(u, 0)]), HEAD_DIM, 1)
            return carry

        lax.fori_loop(0, r * nblk // DIL_UNROLL, group, 0)

    def merge(ci, carry):
        rows = pl.ds(pl.multiple_of(ci * TQ, TQ), TQ)
        ms = [m_s[g, rows, :] for g in range(len(DIL_BRANCHES))]
        m = functools.reduce(jnp.maximum, ms)
        num = jnp.zeros((TQ, LANES), F32)
        den = jnp.zeros((TQ, LANES), F32)
        for g in range(len(DIL_BRANCHES)):
            wgt = jnp.exp2(ms[g] - m)
            num = num + wgt * acc_s[g, rows, :]
            den = den + wgt * l_s[g, rows, :]
        o_ref[0, rows, :] = (num / den).astype(o_ref.dtype)
        return carry

    lax.fori_loop(0, DT // TQ, merge, 0)


def _dil_bias(t5_table):
    qi = np.arange(DQ)[:, None]
    ki = np.arange(2 * DQ)[None, :] - DQ
    steps = qi - ki
    valid = (steps >= 0) & (steps <= DIL_WINDOW)
    tiles = [_bias_tile(t5_table.astype(F32) * LOG2E, steps * r, valid) for _, r in DIL_BRANCHES]
    return jnp.stack(tiles, axis=1).reshape(DIL_HEADS // 2, 2, len(DIL_BRANCHES), DQ, 2 * DQ)


def _dilated_attention(proj3, t5_table):
    b, s, _ = proj3.shape
    pairs = DIL_HEADS // 2
    bias = _dil_bias(t5_table)
    cur = lambda off: (lambda bb, hp, t: (bb, t, off * pairs + hp))
    prev = lambda off: (lambda bb, hp, t: (bb, jnp.maximum(t - 1, 0), off * pairs + hp))
    tile = lambda imap: pl.BlockSpec((1, DT, LANES), imap)
    slab = lambda rows: pltpu.VMEM((rows, LANES), F32)
    per_branch = pltpu.VMEM((len(DIL_BRANCHES), DT, LANES), F32)
    return pl.pallas_call(
        _dil_kernel,
        grid=(b, pairs, s // DT),
        in_specs=[tile(cur(0)), tile(prev(1)), tile(cur(1)), tile(prev(2)), tile(cur(2)),
                  pl.BlockSpec((1, 2, len(DIL_BRANCHES), DQ, 2 * DQ), lambda bb, hp, t: (hp, 0, 0, 0, 0))],
        out_specs=tile(cur(0)),
        out_shape=jax.ShapeDtypeStruct((b, s, D_MODEL), BF16),
        scratch_shapes=[slab(DT), slab(2 * DT), slab(2 * DT), per_branch, per_branch, per_branch],
        compiler_params=_cparams(("parallel", "parallel", "arbitrary")),
        name="dilated_attention",
    )(proj3, proj3, proj3, proj3, proj3, bias)


def _post_kernel(*refs, n_o, final):
    h_ref = refs[0]
    o_refs = refs[1:1 + n_o]
    (wo_ref, p_ref, g_mlp_ref, w_up_ref, w_down_ref, g_ple_ref, w_gate_ref, w_proj_ref, g_fin_ref,
     out_ref, acc_ref) = refs[1 + n_o:]
    o = o_refs[0][...] if n_o == 1 else jnp.concatenate([r[...] for r in o_refs], axis=1)
    h = h_ref[...] + _dot(o, wo_ref[...])
    hn = _rms(h, g_mlp_ref[...], NORM_EPS).astype(BF16)
    acc_ref[...] = h
    for c in range(D_FF // FF_CHUNK):
        u = jnp.maximum(_dot(hn, w_up_ref[:, c * FF_CHUNK:(c + 1) * FF_CHUNK]), 0.0)
        acc_ref[...] += _dot((u * u).astype(BF16), w_down_ref[c * FF_CHUNK:(c + 1) * FF_CHUNK, :])
    h = acc_ref[...]
    gate = jax.nn.sigmoid(_dot(_rms(h, g_ple_ref[...], NORM_EPS).astype(BF16), w_gate_ref[...]))
    h = h + _dot(p_ref[...].astype(BF16), w_proj_ref[...]) * gate
    if final:
        h = _rms(h, g_fin_ref[...], NORM_EPS)
    out_ref[...] = h


def _post(h2d, o_list, w_out, p2d, g_mlp, w_up, w_down, g_ple, w_gate, w_proj, g_fin, final):
    m, d = h2d.shape
    n_o = len(o_list)
    row = lambda width: pl.BlockSpec((TM_POST, width), lambda i: (i, 0))
    const = lambda a: pl.BlockSpec(a.shape, lambda i: (0,) * a.ndim, pipeline_mode=pl.Buffered(1))
    vec = lambda g: g.reshape(1, d).astype(F32)
    weights = [w_out, vec(g_mlp), w_up, w_down, vec(g_ple), w_gate, w_proj, vec(g_fin)]
    return pl.pallas_call(
        functools.partial(_post_kernel, n_o=n_o, final=final),
        grid=(m // TM_POST,),
        in_specs=([row(d)] + [row(o.shape[1]) for o in o_list] + [const(weights[0])]
                  + [row(p2d.shape[1])] + [const(w) for w in weights[1:]]),
        out_specs=row(d),
        out_shape=jax.ShapeDtypeStruct((m, d), F32),
        scratch_shapes=[pltpu.VMEM((TM_POST, d), F32)],
        compiler_params=_cparams(("parallel",)),
        name="out_mlp_ple",
    )(h2d, *o_list, weights[0], p2d, *weights[1:])


def _scale_cols(w_in, col_scale):
    return (w_in * jnp.asarray(col_scale, F32)[None, :]).astype(BF16)


def kernel(x, p, t5_table, w_in_even, w_out_even, lambda_q1, lambda_k1, lambda_q2, lambda_k2, subln_g,
           w_in_odd, w_out_odd, norm_mix_g, norm_mlp_g, w_mlp_up, w_mlp_down, norm_ple_g, w_ple_gate,
           w_ple_proj, final_norm_g):
    b, s, d = x.shape
    depth = p.shape[0]
    assert d == D_MODEL and s % DT == 0 and (b * s) % TM_POST == 0
    assert all(w == DIL_WINDOW * r and w <= DT for w, r in DIL_BRANCHES)
    col = np.arange(3 * D_MODEL)
    qk_scale = HEAD_DIM ** -0.5
    even_q = (col < SB_WIDTH) | ((col >= 3 * SB_WIDTH) & (col < 3 * SB_WIDTH + DIFF_WIDTH))
    even_scale = np.where(even_q, qk_scale * LOG2E, 1.0)
    odd_scale = np.where(col < DIL_HEADS * HEAD_DIM, qk_scale * LOG2E, 1.0)
    h = x.reshape(b * s, d)
    for i in range(depth):
        if i % 2 == 0:
            e = i // 2
            w_in = _scale_cols(w_in_even[e], even_scale)
            qa, ka, va, qb, kb, vb = (w_in[:, lo:lo + SB_WIDTH] for lo in range(0, 6 * SB_WIDTH, SB_WIDTH))
            proj, t_sb, t_d = _norm_matmul(
                h, norm_mix_g[i], jnp.concatenate([ka, kb], axis=1),
                ((jnp.concatenate([qa, va], axis=1).T, TS), (jnp.concatenate([qb, vb], axis=1).T, TD)))
            proj = proj.reshape(b, s, -1)
            t_sb = t_sb.reshape(b, s // TS, 2 * SB_WIDTH, TS)
            t_d = t_d.reshape(b, s // TD, 2 * DIFF_WIDTH, TD)
            lambda_init = 0.8 - 0.6 * math.exp(-0.3 * i)
            lam_params = jnp.stack([lambda_q1[e], lambda_k1[e], lambda_q2[e], lambda_k2[e]]).astype(F32)
            o_sb = _sb_attention(proj, t_sb)
            o_d = _diff_attention(proj, t_d, _diff_bias(t5_table), lam_params, subln_g[e].astype(F32),
                                  lambda_init)
            w_out = w_out_even[e].astype(BF16)
            o_list = [o_sb.reshape(b * s, SB_WIDTH), o_d.reshape(b * s, DIFF_WIDTH)]
        else:
            o = i // 2
            (proj,) = _norm_matmul(h, norm_mix_g[i], _scale_cols(w_in_odd[o], odd_scale))
            proj = proj.reshape(b, s, -1)
            o_list = [_dilated_attention(proj, t5_table).reshape(b * s, d)]
            w_out = w_out_odd[o].astype(BF16)
        h = _post(h, o_list, w_out, p[i].reshape(b * s, PLE_DIM), norm_mlp_g[i],
                  w_mlp_up[i].astype(BF16), w_mlp_down[i].astype(BF16), norm_ple_g[i],
                  w_ple_gate[i].astype(BF16), w_ple_proj[i].astype(BF16), final_norm_g,
                  final=(i == depth - 1))
    return h.reshape(b, s, d)
```

```python
import functools
import math

import numpy as np
import jax
import jax.numpy as jnp
from jax import lax
from jax.experimental import pallas as pl
from jax.experimental.pallas import tpu as pltpu

F32 = jnp.float32
BF16 = jnp.bfloat16

D_MODEL = 1024
HEAD_DIM = 64
LANES = 128
SB_HEADS = 8
DIFF_HEADS = 4
DIL_HEADS = 16
SB_WIDTH = SB_HEADS * HEAD_DIM
DIFF_WIDTH = DIFF_HEADS * 2 * HEAD_DIM
DIL_BRANCHES = ((128, 1), (512, 4), (2048, 16))
DIL_WINDOW = 128
NUM_BUCKETS = 32
MAX_DISTANCE = 128
D_FF = 4 * D_MODEL
PLE_DIM = 256
NORM_EPS = 1e-6
SUBLN_EPS = 1e-5
NEG = -1e30
LOG2E = math.log2(math.e)
SB_DEAD_LOG = 100.0

VMEM_LIMIT_BYTES = 56 * 1024 * 1024

TM_PROJ = 512
TM_POST = 512
FF_CHUNK = 1024
TQ = 256
TS = 256
TD = TM_PROJ
DT = 2048
DQ = DIL_WINDOW
DIL_UNROLL = 16
DIL_SKEW = 3


def _cparams(sem):
    return pltpu.CompilerParams(dimension_semantics=sem, vmem_limit_bytes=VMEM_LIMIT_BYTES)


def _rms(x, g, eps):
    return x * lax.rsqrt(jnp.mean(x * x, axis=-1, keepdims=True) + eps) * g


def _dot(a, b):
    return jnp.dot(a, b, preferred_element_type=F32)


def _dot_nt(a, b):
    return lax.dot_general(a, b, (((1,), (1,)), ((), ())), preferred_element_type=F32)


def _norm_matmul_kernel(h_ref, g_ref, w_ref, *rest, slabs):
    hn = _rms(h_ref[...], g_ref[...], NORM_EPS).astype(BF16)
    n_t = len(slabs)
    wt_refs, o_ref, ot_refs = rest[:n_t], rest[n_t], rest[n_t + 1:]
    for wt_ref, ot_ref, width in zip(wt_refs, ot_refs, slabs):
        res = _dot_nt(wt_ref[...], hn).astype(ot_ref.dtype)
        for c in range(TM_PROJ // width):
            ot_ref[c] = res[:, c * width:(c + 1) * width]
    o_ref[...] = _dot(hn, w_ref[...]).astype(o_ref.dtype)


def _norm_matmul(h2d, g, w, wts=()):
    m, k = h2d.shape
    n = w.shape[1]
    const = lambda a: pl.BlockSpec(a.shape, lambda i: (0,) * a.ndim, pipeline_mode=pl.Buffered(1))
    in_specs = [pl.BlockSpec((TM_PROJ, k), lambda i: (i, 0)), pl.BlockSpec((1, k), lambda i: (0, 0)), const(w)]
    in_specs += [const(wt) for wt, _ in wts]
    out_specs = [pl.BlockSpec((TM_PROJ, n), lambda i: (i, 0))]
    out_shape = [jax.ShapeDtypeStruct((m, n), BF16)]
    for wt, width in wts:
        per_step = TM_PROJ // width
        out_specs.append(pl.BlockSpec((per_step, wt.shape[0], width), lambda i: (i, 0, 0)))
        out_shape.append(jax.ShapeDtypeStruct((m // width, wt.shape[0], width), BF16))
    return pl.pallas_call(
        functools.partial(_norm_matmul_kernel, slabs=tuple(width for _, width in wts)),
        grid=(m // TM_PROJ,),
        in_specs=in_specs, out_specs=out_specs, out_shape=out_shape,
        compiler_params=_cparams(("parallel",)),
        name="norm_in_proj",
    )(h2d, g.reshape(1, k), w, *[wt for wt, _ in wts])


def _t5_bucket_np(dist):
    n = np.maximum(dist, 0)
    max_exact = NUM_BUCKETS // 2
    nf = np.maximum(n, 1).astype(np.float32)
    large = max_exact + (np.log(nf / np.float32(max_exact)) / np.float32(math.log(MAX_DISTANCE / max_exact))
                         * np.float32(NUM_BUCKETS - max_exact)).astype(np.int32)
    large = np.minimum(large, NUM_BUCKETS - 1)
    return np.where(n < max_exact, n, large).astype(np.int32)


def _bias_tile(table_cols, dist, valid):
    bucket = _t5_bucket_np(dist)
    out = jnp.zeros((table_cols.shape[1],) + dist.shape, F32)
    for bkt in np.unique(bucket[valid]):
        out = jnp.where(jnp.asarray(bucket == bkt)[None], table_cols[bkt][:, None, None], out)
    return jnp.where(jnp.asarray(valid)[None], out, NEG)


def _sb_kernel(qt_ref, k_ref, vt_ref, o_ref, acc_ref, c_ref):
    i = pl.program_id(2)
    qt = qt_ref[0, 0]
    row = lax.broadcasted_iota(jnp.int32, (LANES, 1), 0)
    head_rows = (row < HEAD_DIM, row >= HEAD_DIM)
    qts = tuple(jnp.where(hr, qt, jnp.zeros_like(qt)) for hr in head_rows)
    kk = lax.broadcasted_iota(jnp.int32, (TS, TS), 0)
    qq = lax.broadcasted_iota(jnp.int32, (TS, TS), 1)
    tri = (kk <= qq).astype(BF16)
    strict = kk < qq

    def tiles(js, keeps, c_in):
        items = [(t, hh) for t in range(len(js)) for hh in range(2)]
        ks = [k_ref[0, pl.ds(pl.multiple_of(j * TS, TS), TS), :] for j in js]
        z = {it: _dot(ks[it[0]], qts[it[1]]) for it in items}
        parts = {}
        for it in items:
            lk = -(jnp.maximum(z[it], 0.0) + jnp.log2(1.0 + jnp.exp2(-jnp.abs(z[it]))))
            if keeps[it[0]] is not None:
                lk = jnp.where(keeps[it[0]], lk, 0.0)
            hi = lk.astype(BF16)
            parts[it] = (hi, (lk - hi.astype(F32)).astype(BF16))
        incl = {it: _dot(tri, parts[it][0]) + _dot(tri, parts[it][1]) for it in items}
        c = list(c_in)
        w = {}
        for it in items:
            t, hh = it
            wt = jnp.exp2(z[it] + incl[it] + c[hh])
            if keeps[t] is not None:
                wt = jnp.where(keeps[t], wt, 0.0)
            w[it] = wt.astype(BF16)
            c[hh] = c[hh] + incl[it][0:1, :]
        upd = jnp.zeros((LANES, TS), F32)
        for t, hh in items:
            vts = vt_ref[0, js[t]]
            vth = jnp.where(head_rows[hh], vts, jnp.zeros_like(vts))
            upd = upd + _dot(vth, w[(t, hh)])
        return upd, c

    zero_c = jnp.zeros((1, TS), F32)
    upd, c1 = tiles([i, jnp.maximum(i - 1, 0)], [strict, kk < jnp.where(i >= 1, TS, 0)], [zero_c, zero_c])
    acc_ref[...] = upd
    for hh in range(2):
        c_ref[hh] = c1[hh]
    alive1 = jnp.maximum(jnp.max(c1[0]), jnp.max(c1[1]))

    def cond(state):
        j, alive = state
        return jnp.logical_and(j >= 0, alive > -SB_DEAD_LOG * LOG2E)

    def body(state):
        j, _ = state
        upd, c_new = tiles([j], [None], [c_ref[0], c_ref[1]])
        acc_ref[...] += upd
        for hh in range(2):
            c_ref[hh] = c_new[hh]
        return j - 1, jnp.maximum(jnp.max(c_new[0]), jnp.max(c_new[1]))

    lax.while_loop(cond, body, (i - 2, alive1))
    o_ref[0] = jnp.transpose(acc_ref[...]).astype(o_ref.dtype)


def _sb_attention(proj3, proj_t):
    b, s, _ = proj3.shape
    pairs = SB_WIDTH // LANES
    nt = s // TS
    return pl.pallas_call(
        _sb_kernel,
        grid=(b, pairs, nt),
        in_specs=[pl.BlockSpec((1, 1, LANES, TS), lambda bb, hp, i: (bb, i, hp, 0)),
                  pl.BlockSpec((1, s, LANES), lambda bb, hp, i: (bb, 0, hp)),
                  pl.BlockSpec((1, nt, LANES, TS), lambda bb, hp, i: (bb, 0, pairs + hp, 0))],
        out_specs=pl.BlockSpec((1, TS, LANES), lambda bb, hp, i: (bb, i, hp)),
        out_shape=jax.ShapeDtypeStruct((b, s, SB_WIDTH), BF16),
        scratch_shapes=[pltpu.VMEM((LANES, TS), F32), pltpu.VMEM((2, 1, TS), F32)],
        compiler_params=_cparams(("parallel", "parallel", "arbitrary")),
        name="sb_attention",
    )(proj_t, proj3, proj_t)


def _diff_kernel(qt_ref, k_ref, vt_ref, bias_ref, lam_ref, g_ref, o_ref, m_ref, l_ref, acc_ref, sa_ref, sb_ref,
                 *, lambda_init):
    i = pl.program_id(2)
    qt = qt_ref[0, 0]
    row = lax.broadcasted_iota(jnp.int32, (LANES, 1), 0)
    map0 = row < HEAD_DIM
    zero = jnp.zeros_like(qt)
    qts = (jnp.where(map0, qt, zero), jnp.where(map0, zero, qt))

    m_ref[...] = jnp.full_like(m_ref, NEG)
    l_ref[...] = jnp.zeros_like(l_ref)
    acc_ref[...] = jnp.zeros_like(acc_ref)

    def scores(j):
        ks = k_ref[0, pl.ds(pl.multiple_of(j * TD, TD), TD), :]
        return tuple(_dot(ks, qts[mm]) for mm in range(2))

    def absorb(j, s_ref, bias_idx, offset=None):
        vts = vt_ref[0, j]
        s, alpha, p = [], [], []
        for mm in range(2):
            sm = s_ref[mm]
            if bias_idx is not None:
                sm = sm + bias_ref[0, mm, bias_idx]
            if offset is not None:
                sm = sm + offset
            m_old = m_ref[mm]
            m_new = jnp.maximum(m_old, jnp.max(sm, axis=0, keepdims=True))
            m_ref[mm] = m_new
            alpha.append(jnp.exp2(m_old - m_new))
            s.append(sm - m_new)
        for mm in range(2):
            pm = jnp.exp2(s[mm])
            l_ref[mm] = alpha[mm] * l_ref[mm] + jnp.sum(pm, axis=0, keepdims=True)
            p.append(pm.astype(BF16))
        pv = [_dot(vts, p[mm]) for mm in range(2)]
        for mm in range(2):
            acc_ref[mm] = alpha[mm] * acc_ref[mm] + pv[mm]

    def stage(s_ref, j):
        s_pair = scores(j)
        for mm in range(2):
            s_ref[mm] = s_pair[mm]

    stage(sa_ref, i)
    stage(sb_ref, jnp.maximum(i - 1, 0))
    absorb(i, sa_ref, 0)
    stage(sa_ref, jnp.maximum(i - 2, 0))
    absorb(jnp.maximum(i - 1, 0), sb_ref, 1, offset=jnp.where(i >= 1, 0.0, NEG))

    @pl.when(i >= 2)
    def _():
        def body(u, carry):
            ja = i - 2 - 2 * u
            stage(sb_ref, ja - 1)
            absorb(ja, sa_ref, None)
            stage(sa_ref, jnp.maximum(ja - 2, 0))
            absorb(ja - 1, sb_ref, None)
            return carry

        lax.fori_loop(0, (i - 1) // 2, body, 0)

        @pl.when((i - 1) % 2 == 1)
        def _():
            absorb(0, sa_ref, None)

    lam = (jnp.exp(jnp.sum(lam_ref[0:1, :] * lam_ref[1:2, :], axis=1, keepdims=True))
           - jnp.exp(jnp.sum(lam_ref[2:3, :] * lam_ref[3:4, :], axis=1, keepdims=True)) + lambda_init)
    ot = acc_ref[0] / l_ref[0] - lam * (acc_ref[1] / l_ref[1])
    o = jnp.transpose(ot)
    o_ref[0] = (_rms(o, g_ref[...], SUBLN_EPS) * (1.0 - lambda_init)).astype(o_ref.dtype)


def _diff_attention(proj3, proj_t, bias, lam_params, subln_g, lambda_init):
    b, s, _ = proj3.shape
    kbase = SB_WIDTH // LANES
    nt = s // TD
    return pl.pallas_call(
        functools.partial(_diff_kernel, lambda_init=lambda_init),
        grid=(b, DIFF_HEADS, nt),
        in_specs=[pl.BlockSpec((1, 1, LANES, TD), lambda bb, h, i: (bb, i, h, 0)),
                  pl.BlockSpec((1, s, LANES), lambda bb, h, i: (bb, 0, kbase + h)),
                  pl.BlockSpec((1, nt, LANES, TD), lambda bb, h, i: (bb, 0, DIFF_HEADS + h, 0)),
                  pl.BlockSpec((1, 2, 2, TD, TD), lambda bb, h, i: (h, 0, 0, 0, 0)),
                  pl.BlockSpec((4, HEAD_DIM), lambda bb, h, i: (0, 0)),
                  pl.BlockSpec((1, LANES), lambda bb, h, i: (0, 0))],
        out_specs=pl.BlockSpec((1, TD, LANES), lambda bb, h, i: (bb, i, h)),
        out_shape=jax.ShapeDtypeStruct((b, s, DIFF_WIDTH), BF16),
        scratch_shapes=[pltpu.VMEM((2, 1, TD), F32), pltpu.VMEM((2, 1, TD), F32),
                        pltpu.VMEM((2, LANES, TD), F32),
                        pltpu.VMEM((2, TD, TD), F32), pltpu.VMEM((2, TD, TD), F32)],
        compiler_params=_cparams(("parallel", "parallel", "arbitrary")),
        name="diff_attention",
    )(proj_t, proj3, proj_t, bias, lam_params, subln_g.reshape(1, LANES))


def _diff_bias(t5_table):
    cols = t5_table.astype(F32)[:, SB_HEADS:]
    cols = (cols - cols[NUM_BUCKETS - 1:NUM_BUCKETS, :]) * LOG2E
    nb = TD // MAX_DISTANCE
    ki = np.arange(MAX_DISTANCE)[:, None]
    qi = np.arange(MAX_DISTANCE)[None, :]
    d0 = qi - ki
    diag = _bias_tile(cols, d0, d0 >= 0)
    prev = _bias_tile(cols, d0 + MAX_DISTANCE, np.ones_like(d0, bool))
    zero = jnp.zeros_like(diag)
    masked = jnp.full_like(diag, NEG)

    def block0(kb, qb):
        return masked if kb > qb else diag if kb == qb else prev if kb == qb - 1 else zero

    t0 = jnp.concatenate([jnp.concatenate([block0(kb, qb) for qb in range(nb)], axis=2) for kb in range(nb)], axis=1)
    t1 = jnp.concatenate([jnp.concatenate([prev if (kb == nb - 1 and qb == 0) else zero for qb in range(nb)], axis=2)
                          for kb in range(nb)], axis=1)
    both = jnp.stack([t0, t1], axis=1)
    return both.reshape(DIFF_HEADS, 2, 2, TD, TD)


def _dil_kernel(q_ref, kp_ref, kc_ref, vp_ref, vc_ref, bias_ref, o_ref, qf, kf, vf, acc_s, m_s, l_s):
    t_idx = pl.program_id(2)
    qf[...] = q_ref[0].astype(F32)
    kf[0:DT, :] = kp_ref[0].astype(F32)
    kf[DT:2 * DT, :] = kc_ref[0].astype(F32)
    vf[0:DT, :] = vp_ref[0].astype(F32)
    vf[DT:2 * DT, :] = vc_ref[0].astype(F32)
    lane = lax.broadcasted_iota(jnp.int32, (1, LANES), 1)
    head0 = lane < HEAD_DIM
    heads = (head0, jnp.logical_not(head0))

    for g, (_, r) in enumerate(DIL_BRANCHES):
        nblk = DT // (r * DQ)
        shift = nblk.bit_length() - 1

        def group(t0, carry, g=g, r=r, nblk=nblk, shift=shift):
            rows, k0s, pre, s, p, mx = {}, {}, {}, {}, {}, {}

            def scores(u):
                t = t0 * DIL_UNROLL + u
                c = lax.shift_right_logical(t, shift)
                n = jnp.bitwise_and(t, nblk - 1)
                q0 = c + n * (DQ * r)
                k0s[u] = DT + q0 - DQ * r
                rows[u] = pl.ds(q0, DQ, stride=r)
                qg = qf[rows[u], :].astype(BF16)
                kg = kf[pl.ds(k0s[u], 2 * DQ, stride=r), :].astype(BF16)
                for hh in range(2):
                    s[(u, hh)] = _dot_nt(jnp.where(heads[hh], qg, jnp.zeros_like(qg)), kg)
                pre[u] = jnp.where(jnp.logical_and(t_idx == 0, n == 0), 1, 0)

            def softmax(u):
                for hh in range(2):
                    sb = s.pop((u, hh)) + bias_ref[0, hh, g, pre[u]]
                    m = jnp.max(sb, axis=1, keepdims=True)
                    p[(u, hh)] = jnp.exp2(sb - m).astype(BF16)
                    mx[(u, hh)] = jnp.broadcast_to(m, (DQ, LANES))

            def values(u):
                vg = vf[pl.ds(k0s[u], 2 * DQ, stride=r), :].astype(BF16)
                res = [_dot(p.pop((u, hh)), jnp.where(heads[hh], vg, jnp.ones_like(vg))) for hh in range(2)]
                acc_s[g, rows[u], :] = jnp.where(head0, res[0], res[1])
                m_s[g, rows[u], :] = jnp.where(head0, mx.pop((u, 0)), mx.pop((u, 1)))
                l_s[g, rows[u], :] = pltpu.roll(jnp.where(head0, res[1], res[0]), HEAD_DIM, 1)

            for step in range(DIL_UNROLL + 2 * DIL_SKEW):
                if step < DIL_UNROLL:
                    scores(step)
                if 0 <= step - DIL_SKEW < DIL_UNROLL:
                    softmax(step - DIL_SKEW)
                if step >= 2 * DIL_SKEW:
                    values(step - 2 * DIL_SKEW)
            return carry

        lax.fori_loop(0, r * nblk // DIL_UNROLL, group, 0)

    def merge(ci, carry):
        rows = pl.ds(pl.multiple_of(ci * TQ, TQ), TQ)
        ms = [m_s[g, rows, :] for g in range(len(DIL_BRANCHES))]
        m = functools.reduce(jnp.maximum, ms)
        num = jnp.zeros((TQ, LANES), F32)
        den = jnp.zeros((TQ, LANES), F32)
        for g in range(len(DIL_BRANCHES)):
            wgt = jnp.exp2(ms[g] - m)
            num = num + wgt * acc_s[g, rows, :]
            den = den + wgt * l_s[g, rows, :]
        o_ref[0, rows, :] = (num / den).astype(o_ref.dtype)
        return carry

    lax.fori_loop(0, DT // TQ, merge, 0)


def _dil_bias(t5_table):
    qi = np.arange(DQ)[:, None]
    ki = np.arange(2 * DQ)[None, :] - DQ
    steps = qi - ki
    valid = (steps >= 0) & (steps <= DIL_WINDOW)
    table = t5_table.astype(F32) * LOG2E
    tiles = [jnp.stack([_bias_tile(table, steps * r, v) for v in (valid, valid & (ki >= 0))], axis=1)
             for _, r in DIL_BRANCHES]
    return jnp.stack(tiles, axis=1).reshape(DIL_HEADS // 2, 2, len(DIL_BRANCHES), 2, DQ, 2 * DQ)


def _dilated_attention(proj3, t5_table):
    b, s, _ = proj3.shape
    pairs = DIL_HEADS // 2
    bias = _dil_bias(t5_table)
    cur = lambda off: (lambda bb, hp, t: (bb, t, off * pairs + hp))
    prev = lambda off: (lambda bb, hp, t: (bb, jnp.maximum(t - 1, 0), off * pairs + hp))
    tile = lambda imap: pl.BlockSpec((1, DT, LANES), imap)
    slab = lambda rows: pltpu.VMEM((rows, LANES), F32)
    per_branch = pltpu.VMEM((len(DIL_BRANCHES), DT, LANES), F32)
    return pl.pallas_call(
        _dil_kernel,
        grid=(b, pairs, s // DT),
        in_specs=[tile(cur(0)), tile(prev(1)), tile(cur(1)), tile(prev(2)), tile(cur(2)),
                  pl.BlockSpec((1, 2, len(DIL_BRANCHES), 2, DQ, 2 * DQ), lambda bb, hp, t: (hp, 0, 0, 0, 0, 0))],
        out_specs=tile(cur(0)),
        out_shape=jax.ShapeDtypeStruct((b, s, D_MODEL), BF16),
        scratch_shapes=[slab(DT), slab(2 * DT), slab(2 * DT), per_branch, per_branch, per_branch],
        compiler_params=_cparams(("parallel", "parallel", "arbitrary")),
        name="dilated_attention",
    )(proj3, proj3, proj3, proj3, proj3, bias)


def _post_kernel(*refs, n_o, final):
    h_ref = refs[0]
    o_refs = refs[1:1 + n_o]
    (wo_ref, p_ref, g_mlp_ref, w_up_ref, w_down_ref, g_ple_ref, w_gate_ref, w_proj_ref, g_fin_ref,
     out_ref, acc_ref) = refs[1 + n_o:]
    o = o_refs[0][...] if n_o == 1 else jnp.concatenate([r[...] for r in o_refs], axis=1)
    h = h_ref[...] + _dot(o, wo_ref[...])
    hn = _rms(h, g_mlp_ref[...], NORM_EPS).astype(BF16)
    acc_ref[...] = h
    for c in range(D_FF // FF_CHUNK):
        u = jnp.maximum(_dot(hn, w_up_ref[:, c * FF_CHUNK:(c + 1) * FF_CHUNK]), 0.0)
        acc_ref[...] += _dot((u * u).astype(BF16), w_down_ref[c * FF_CHUNK:(c + 1) * FF_CHUNK, :])
    h = acc_ref[...]
    gate = jax.nn.sigmoid(_dot(_rms(h, g_ple_ref[...], NORM_EPS).astype(BF16), w_gate_ref[...]))
    h = h + _dot(p_ref[...].astype(BF16), w_proj_ref[...]) * gate
    if final:
        h = _rms(h, g_fin_ref[...], NORM_EPS)
    out_ref[...] = h


def _post(h2d, o_list, w_out, p2d, g_mlp, w_up, w_down, g_ple, w_gate, w_proj, g_fin, final):
    m, d = h2d.shape
    n_o = len(o_list)
    row = lambda width: pl.BlockSpec((TM_POST, width), lambda i: (i, 0))
    const = lambda a: pl.BlockSpec(a.shape, lambda i: (0,) * a.ndim, pipeline_mode=pl.Buffered(1))
    vec = lambda g: g.reshape(1, d).astype(F32)
    weights = [w_out, vec(g_mlp), w_up, w_down, vec(g_ple), w_gate, w_proj, vec(g_fin)]
    return pl.pallas_call(
        functools.partial(_post_kernel, n_o=n_o, final=final),
        grid=(m // TM_POST,),
        in_specs=([row(d)] + [row(o.shape[1]) for o in o_list] + [const(weights[0])]
                  + [row(p2d.shape[1])] + [const(w) for w in weights[1:]]),
        out_specs=row(d),
        out_shape=jax.ShapeDtypeStruct((m, d), F32),
        scratch_shapes=[pltpu.VMEM((TM_POST, d), F32)],
        compiler_params=_cparams(("parallel",)),
        name="out_mlp_ple",
    )(h2d, *o_list, weights[0], p2d, *weights[1:])


def _scale_cols(w_in, col_scale):
    return (w_in * jnp.asarray(col_scale, F32)[None, :]).astype(BF16)


def kernel(x, p, t5_table, w_in_even, w_out_even, lambda_q1, lambda_k1, lambda_q2, lambda_k2, subln_g,
           w_in_odd, w_out_odd, norm_mix_g, norm_mlp_g, w_mlp_up, w_mlp_down, norm_ple_g, w_ple_gate,
           w_ple_proj, final_norm_g):
    b, s, d = x.shape
    depth = p.shape[0]
    assert d == D_MODEL and s % DT == 0 and (b * s) % TM_POST == 0
    assert all(w == DIL_WINDOW * r and w <= DT for w, r in DIL_BRANCHES)
    col = np.arange(3 * D_MODEL)
    qk_scale = HEAD_DIM ** -0.5
    even_q = (col < SB_WIDTH) | ((col >= 3 * SB_WIDTH) & (col < 3 * SB_WIDTH + DIFF_WIDTH))
    even_scale = np.where(even_q, qk_scale * LOG2E, 1.0)
    odd_scale = np.where(col < DIL_HEADS * HEAD_DIM, qk_scale * LOG2E, 1.0)
    h = x.reshape(b * s, d)
    for i in range(depth):
        if i % 2 == 0:
            e = i // 2
            w_in = _scale_cols(w_in_even[e], even_scale)
            qa, ka, va, qb, kb, vb = (w_in[:, lo:lo + SB_WIDTH] for lo in range(0, 6 * SB_WIDTH, SB_WIDTH))
            proj, t_sb, t_d = _norm_matmul(
                h, norm_mix_g[i], jnp.concatenate([ka, kb], axis=1),
                ((jnp.concatenate([qa, va], axis=1).T, TS), (jnp.concatenate([qb, vb], axis=1).T, TD)))
            proj = proj.reshape(b, s, -1)
            t_sb = t_sb.reshape(b, s // TS, 2 * SB_WIDTH, TS)
            t_d = t_d.reshape(b, s // TD, 2 * DIFF_WIDTH, TD)
            lambda_init = 0.8 - 0.6 * math.exp(-0.3 * i)
            lam_params = jnp.stack([lambda_q1[e], lambda_k1[e], lambda_q2[e], lambda_k2[e]]).astype(F32)
            o_sb = _sb_attention(proj, t_sb)
            o_d = _diff_attention(proj, t_d, _diff_bias(t5_table), lam_params, subln_g[e].astype(F32),
                                  lambda_init)
            w_out = w_out_even[e].astype(BF16)
            o_list = [o_sb.reshape(b * s, SB_WIDTH), o_d.reshape(b * s, DIFF_WIDTH)]
        else:
            o = i // 2
            (proj,) = _norm_matmul(h, norm_mix_g[i], _scale_cols(w_in_odd[o], odd_scale))
            proj = proj.reshape(b, s, -1)
            o_list = [_dilated_attention(proj, t5_table).reshape(b * s, d)]
            w_out = w_out_odd[o].astype(BF16)
        h = _post(h, o_list, w_out, p[i].reshape(b * s, PLE_DIM), norm_mlp_g[i],
                  w_mlp_up[i].astype(BF16), w_mlp_down[i].astype(BF16), norm_ple_g[i],
                  w_ple_gate[i].astype(BF16), w_ple_proj[i].astype(BF16), final_norm_g,
                  final=(i == depth - 1))
    return h.reshape(b, s, d)
```

```python
import functools
import math

import numpy as np
import jax
import jax.numpy as jnp
from jax import lax
from jax.experimental import pallas as pl
from jax.experimental.pallas import tpu as pltpu

F32 = jnp.float32
BF16 = jnp.bfloat16

D_MODEL = 1024
HEAD_DIM = 64
LANES = 128
SB_HEADS = 8
DIFF_HEADS = 4
DIL_HEADS = 16
SB_WIDTH = SB_HEADS * HEAD_DIM
DIFF_WIDTH = DIFF_HEADS * 2 * HEAD_DIM
DIL_BRANCHES = ((128, 1), (512, 4), (2048, 16))
DIL_WINDOW = 128
NUM_BUCKETS = 32
MAX_DISTANCE = 128
D_FF = 4 * D_MODEL
PLE_DIM = 256
NORM_EPS = 1e-6
SUBLN_EPS = 1e-5
NEG = -1e30
LOG2E = math.log2(math.e)
SB_DEAD_LOG = 100.0

VMEM_LIMIT_BYTES = 56 * 1024 * 1024

TM_PROJ = 512
TM_POST = 512
FF_CHUNK = 1024
TQ = 256
TS = 256
SB_QS = 4
TD = TM_PROJ
DT = 2048
DQ = DIL_WINDOW
DIL_UNROLL = 16
DIL_SKEW = 3


def _cparams(sem):
    return pltpu.CompilerParams(dimension_semantics=sem, vmem_limit_bytes=VMEM_LIMIT_BYTES)


def _rms(x, g, eps):
    return x * lax.rsqrt(jnp.mean(x * x, axis=-1, keepdims=True) + eps) * g


def _dot(a, b):
    return jnp.dot(a, b, preferred_element_type=F32)


def _dot_nt(a, b):
    return lax.dot_general(a, b, (((1,), (1,)), ((), ())), preferred_element_type=F32)


def _norm_matmul_kernel(h_ref, g_ref, w_ref, *rest, slabs):
    hn = _rms(h_ref[...], g_ref[...], NORM_EPS).astype(BF16)
    n_t = len(slabs)
    wt_refs, o_ref, ot_refs = rest[:n_t], rest[n_t], rest[n_t + 1:]
    for wt_ref, ot_ref, width in zip(wt_refs, ot_refs, slabs):
        res = _dot_nt(wt_ref[...], hn).astype(ot_ref.dtype)
        for c in range(TM_PROJ // width):
            ot_ref[c] = res[:, c * width:(c + 1) * width]
    o_ref[...] = _dot(hn, w_ref[...]).astype(o_ref.dtype)


def _norm_matmul(h2d, g, w, wts=()):
    m, k = h2d.shape
    n = w.shape[1]
    const = lambda a: pl.BlockSpec(a.shape, lambda i: (0,) * a.ndim, pipeline_mode=pl.Buffered(1))
    in_specs = [pl.BlockSpec((TM_PROJ, k), lambda i: (i, 0)), pl.BlockSpec((1, k), lambda i: (0, 0)), const(w)]
    in_specs += [const(wt) for wt, _ in wts]
    out_specs = [pl.BlockSpec((TM_PROJ, n), lambda i: (i, 0))]
    out_shape = [jax.ShapeDtypeStruct((m, n), BF16)]
    for wt, width in wts:
        per_step = TM_PROJ // width
        out_specs.append(pl.BlockSpec((per_step, wt.shape[0], width), lambda i: (i, 0, 0)))
        out_shape.append(jax.ShapeDtypeStruct((m // width, wt.shape[0], width), BF16))
    return pl.pallas_call(
        functools.partial(_norm_matmul_kernel, slabs=tuple(width for _, width in wts)),
        grid=(m // TM_PROJ,),
        in_specs=in_specs, out_specs=out_specs, out_shape=out_shape,
        compiler_params=_cparams(("parallel",)),
        name="norm_in_proj",
    )(h2d, g.reshape(1, k), w, *[wt for wt, _ in wts])


def _t5_bucket_np(dist):
    n = np.maximum(dist, 0)
    max_exact = NUM_BUCKETS // 2
    nf = np.maximum(n, 1).astype(np.float32)
    large = max_exact + (np.log(nf / np.float32(max_exact)) / np.float32(math.log(MAX_DISTANCE / max_exact))
                         * np.float32(NUM_BUCKETS - max_exact)).astype(np.int32)
    large = np.minimum(large, NUM_BUCKETS - 1)
    return np.where(n < max_exact, n, large).astype(np.int32)


def _bias_tile(table_cols, dist, valid):
    bucket = _t5_bucket_np(dist)
    out = jnp.zeros((table_cols.shape[1],) + dist.shape, F32)
    for bkt in np.unique(bucket[valid]):
        out = jnp.where(jnp.asarray(bucket == bkt)[None], table_cols[bkt][:, None, None], out)
    return jnp.where(jnp.asarray(valid)[None], out, NEG)


def _sb_kernel(qt_ref, k_ref, vt_ref, o_ref, acc_ref, c_ref):
    i = pl.program_id(2)
    row = lax.broadcasted_iota(jnp.int32, (LANES, 1), 0)
    head_rows = (row < HEAD_DIM, row >= HEAD_DIM)
    qts = [tuple(jnp.where(hr, qt_ref[0, a], jnp.zeros((LANES, TS), BF16)) for hr in head_rows)
           for a in range(SB_QS)]
    kk = lax.broadcasted_iota(jnp.int32, (TS, TS), 0)
    qq = lax.broadcasted_iota(jnp.int32, (TS, TS), 1)
    tri = (kk <= qq).astype(BF16)
    tri2 = jnp.concatenate([tri, tri], axis=1)
    strict = kk < qq

    def tiles(work, c_in):
        items = [(n, hh) for n in range(len(work)) for hh in range(2)]
        ks = [k_ref[0, pl.ds(pl.multiple_of(j * TS, TS), TS), :] for _, j, _ in work]
        z = {(n, hh): _dot(ks[n], qts[work[n][0]][hh]) for n, hh in items}
        parts = {}
        for it in items:
            keep = work[it[0]][2]
            if keep is not None:
                z[it] = jnp.where(keep, z[it], NEG)
            lk = -(jnp.maximum(z[it], 0.0) + jnp.log2(1.0 + jnp.exp2(-jnp.abs(z[it]))))
            hi = lk.astype(BF16)
            parts[it] = jnp.concatenate([hi, (lk - hi.astype(F32)).astype(BF16)], axis=0)
        incl = {it: _dot(tri2, parts[it]) for it in items}
        c = dict(c_in)
        w = {}
        for n, hh in items:
            a = work[n][0]
            w[(n, hh)] = jnp.exp2(z[(n, hh)] + incl[(n, hh)] + c[(a, hh)]).astype(BF16)
            c[(a, hh)] = c[(a, hh)] + incl[(n, hh)][0:1, :]
        upd = {}
        for n, hh in items:
            a, j, _ = work[n]
            vts = vt_ref[0, j]
            vth = jnp.where(head_rows[hh], vts, jnp.zeros_like(vts))
            part = _dot(vth, w[(n, hh)])
            upd[a] = part if a not in upd else upd[a] + part
        return upd, c

    zero_c = jnp.zeros((1, TS), F32)
    work = []
    for a in range(SB_QS):
        g = i * SB_QS + a
        work.append((a, g, strict))
        work.append((a, g - 1, None) if a > 0 else (a, jnp.maximum(g - 1, 0), kk < jnp.where(g >= 1, TS, 0)))
    upd, c1 = tiles(work, {(a, hh): zero_c for a in range(SB_QS) for hh in range(2)})
    for a in range(SB_QS):
        acc_ref[a] = upd[a]
        for hh in range(2):
            c_ref[a, hh] = c1[(a, hh)]

    for a in range(SB_QS):
        def cond(state):
            j, alive = state
            return jnp.logical_and(j >= 0, alive > -SB_DEAD_LOG * LOG2E)

        def body(state, a=a):
            j, _ = state
            upd, c_new = tiles([(a, j, None)], {(a, hh): c_ref[a, hh] for hh in range(2)})
            acc_ref[a] += upd[a]
            for hh in range(2):
                c_ref[a, hh] = c_new[(a, hh)]
            return j - 1, jnp.maximum(jnp.max(c_new[(a, 0)]), jnp.max(c_new[(a, 1)]))

        alive1 = jnp.maximum(jnp.max(c1[(a, 0)]), jnp.max(c1[(a, 1)]))
        lax.while_loop(cond, body, (i * SB_QS + a - 2, alive1))
        o_ref[0, a * TS:(a + 1) * TS, :] = jnp.transpose(acc_ref[a]).astype(o_ref.dtype)


def _sb_attention(proj3, proj_t):
    b, s, _ = proj3.shape
    pairs = SB_WIDTH // LANES
    nt = s // TS
    return pl.pallas_call(
        _sb_kernel,
        grid=(b, pairs, nt // SB_QS),
        in_specs=[pl.BlockSpec((1, SB_QS, LANES, TS), lambda bb, hp, i: (bb, i, hp, 0)),
                  pl.BlockSpec((1, s, LANES), lambda bb, hp, i: (bb, 0, hp)),
                  pl.BlockSpec((1, nt, LANES, TS), lambda bb, hp, i: (bb, 0, pairs + hp, 0))],
        out_specs=pl.BlockSpec((1, SB_QS * TS, LANES), lambda bb, hp, i: (bb, i, hp)),
        out_shape=jax.ShapeDtypeStruct((b, s, SB_WIDTH), BF16),
        scratch_shapes=[pltpu.VMEM((SB_QS, LANES, TS), F32), pltpu.VMEM((SB_QS, 2, 1, TS), F32)],
        compiler_params=_cparams(("parallel", "parallel", "arbitrary")),
        name="sb_attention",
    )(proj_t, proj3, proj_t)


def _diff_kernel(qt_ref, k_ref, vt_ref, bias_ref, lam_ref, g_ref, o_ref, m_ref, l_ref, acc_ref, sa_ref, sb_ref,
                 *, lambda_init):
    i = pl.program_id(2)
    qt = qt_ref[0, 0]
    row = lax.broadcasted_iota(jnp.int32, (LANES, 1), 0)
    map0 = row < HEAD_DIM
    zero = jnp.zeros_like(qt)
    qts = (jnp.where(map0, qt, zero), jnp.where(map0, zero, qt))

    m_ref[...] = jnp.full_like(m_ref, NEG)
    l_ref[...] = jnp.zeros_like(l_ref)
    acc_ref[...] = jnp.zeros_like(acc_ref)

    def scores(j):
        ks = k_ref[0, pl.ds(pl.multiple_of(j * TD, TD), TD), :]
        return tuple(_dot(ks, qts[mm]) for mm in range(2))

    def absorb(j, s_ref, bias_idx, offset=None):
        vts = vt_ref[0, j]
        s, alpha, p = [], [], []
        for mm in range(2):
            sm = s_ref[mm]
            if bias_idx is not None:
                sm = sm + bias_ref[0, mm, bias_idx]
            if offset is not None:
                sm = sm + offset
            m_old = m_ref[mm]
            m_new = jnp.maximum(m_old, jnp.max(sm, axis=0, keepdims=True))
            m_ref[mm] = m_new
            alpha.append(jnp.exp2(m_old - m_new))
            s.append(sm - m_new)
        for mm in range(2):
            pm = jnp.exp2(s[mm])
            l_ref[mm] = alpha[mm] * l_ref[mm] + jnp.sum(pm, axis=0, keepdims=True)
            p.append(pm.astype(BF16))
        pv = [_dot(vts, p[mm]) for mm in range(2)]
        for mm in range(2):
            acc_ref[mm] = alpha[mm] * acc_ref[mm] + pv[mm]

    def stage(s_ref, j):
        s_pair = scores(j)
        for mm in range(2):
            s_ref[mm] = s_pair[mm]

    stage(sa_ref, i)
    stage(sb_ref, jnp.maximum(i - 1, 0))
    absorb(i, sa_ref, 0)
    stage(sa_ref, jnp.maximum(i - 2, 0))
    absorb(jnp.maximum(i - 1, 0), sb_ref, 1, offset=jnp.where(i >= 1, 0.0, NEG))

    @pl.when(i >= 2)
    def _():
        def body(u, carry):
            ja = i - 2 - 2 * u
            stage(sb_ref, ja - 1)
            absorb(ja, sa_ref, None)
            stage(sa_ref, jnp.maximum(ja - 2, 0))
            absorb(ja - 1, sb_ref, None)
            return carry

        lax.fori_loop(0, (i - 1) // 2, body, 0)

        @pl.when((i - 1) % 2 == 1)
        def _():
            absorb(0, sa_ref, None)

    lam = (jnp.exp(jnp.sum(lam_ref[0:1, :] * lam_ref[1:2, :], axis=1, keepdims=True))
           - jnp.exp(jnp.sum(lam_ref[2:3, :] * lam_ref[3:4, :], axis=1, keepdims=True)) + lambda_init)
    ot = acc_ref[0] / l_ref[0] - lam * (acc_ref[1] / l_ref[1])
    o = jnp.transpose(ot)
    o_ref[0] = (_rms(o, g_ref[...], SUBLN_EPS) * (1.0 - lambda_init)).astype(o_ref.dtype)


def _diff_attention(proj3, proj_t, bias, lam_params, subln_g, lambda_init):
    b, s, _ = proj3.shape
    kbase = SB_WIDTH // LANES
    nt = s // TD
    return pl.pallas_call(
        functools.partial(_diff_kernel, lambda_init=lambda_init),
        grid=(b, DIFF_HEADS, nt),
        in_specs=[pl.BlockSpec((1, 1, LANES, TD), lambda bb, h, i: (bb, i, h, 0)),
                  pl.BlockSpec((1, s, LANES), lambda bb, h, i: (bb, 0, kbase + h)),
                  pl.BlockSpec((1, nt, LANES, TD), lambda bb, h, i: (bb, 0, DIFF_HEADS + h, 0)),
                  pl.BlockSpec((1, 2, 2, TD, TD), lambda bb, h, i: (h, 0, 0, 0, 0)),
                  pl.BlockSpec((4, HEAD_DIM), lambda bb, h, i: (0, 0)),
                  pl.BlockSpec((1, LANES), lambda bb, h, i: (0, 0))],
        out_specs=pl.BlockSpec((1, TD, LANES), lambda bb, h, i: (bb, i, h)),
        out_shape=jax.ShapeDtypeStruct((b, s, DIFF_WIDTH), BF16),
        scratch_shapes=[pltpu.VMEM((2, 1, TD), F32), pltpu.VMEM((2, 1, TD), F32),
                        pltpu.VMEM((2, LANES, TD), F32),
                        pltpu.VMEM((2, TD, TD), F32), pltpu.VMEM((2, TD, TD), F32)],
        compiler_params=_cparams(("parallel", "parallel", "arbitrary")),
        name="diff_attention",
    )(proj_t, proj3, proj_t, bias, lam_params, subln_g.reshape(1, LANES))


def _diff_bias(t5_table):
    cols = t5_table.astype(F32)[:, SB_HEADS:]
    cols = (cols - cols[NUM_BUCKETS - 1:NUM_BUCKETS, :]) * LOG2E
    nb = TD // MAX_DISTANCE
    ki = np.arange(MAX_DISTANCE)[:, None]
    qi = np.arange(MAX_DISTANCE)[None, :]
    d0 = qi - ki
    diag = _bias_tile(cols, d0, d0 >= 0)
    prev = _bias_tile(cols, d0 + MAX_DISTANCE, np.ones_like(d0, bool))
    zero = jnp.zeros_like(diag)
    masked = jnp.full_like(diag, NEG)

    def block0(kb, qb):
        return masked if kb > qb else diag if kb == qb else prev if kb == qb - 1 else zero

    t0 = jnp.concatenate([jnp.concatenate([block0(kb, qb) for qb in range(nb)], axis=2) for kb in range(nb)], axis=1)
    t1 = jnp.concatenate([jnp.concatenate([prev if (kb == nb - 1 and qb == 0) else zero for qb in range(nb)], axis=2)
                          for kb in range(nb)], axis=1)
    both = jnp.stack([t0, t1], axis=1)
    return both.reshape(DIFF_HEADS, 2, 2, TD, TD)


def _dil_kernel(q_ref, kp_ref, kc_ref, vp_ref, vc_ref, bias_ref, o_ref, qf, kf, vf, acc_s, m_s, l_s):
    t_idx = pl.program_id(2)
    qf[...] = q_ref[0].astype(F32)
    kf[0:DT, :] = kp_ref[0].astype(F32)
    kf[DT:2 * DT, :] = kc_ref[0].astype(F32)
    vf[0:DT, :] = vp_ref[0].astype(F32)
    vf[DT:2 * DT, :] = vc_ref[0].astype(F32)
    lane = lax.broadcasted_iota(jnp.int32, (1, LANES), 1)
    head0 = lane < HEAD_DIM
    heads = (head0, jnp.logical_not(head0))

    for g, (_, r) in enumerate(DIL_BRANCHES):
        nblk = DT // (r * DQ)
        shift = nblk.bit_length() - 1

        def group(t0, carry, g=g, r=r, nblk=nblk, shift=shift):
            rows, k0s, pre, s, p, mx = {}, {}, {}, {}, {}, {}

            def scores(u):
                t = t0 * DIL_UNROLL + u
                c = lax.shift_right_logical(t, shift)
                n = jnp.bitwise_and(t, nblk - 1)
                q0 = c + n * (DQ * r)
                k0s[u] = DT + q0 - DQ * r
                rows[u] = pl.ds(q0, DQ, stride=r)
                qg = qf[rows[u], :].astype(BF16)
                kg = kf[pl.ds(k0s[u], 2 * DQ, stride=r), :].astype(BF16)
                for hh in range(2):
                    s[(u, hh)] = _dot_nt(jnp.where(heads[hh], qg, jnp.zeros_like(qg)), kg)
                pre[u] = jnp.where(jnp.logical_and(t_idx == 0, n == 0), 1, 0)

            def softmax(u):
                for hh in range(2):
                    sb = s.pop((u, hh)) + bias_ref[0, hh, g, pre[u]]
                    m = jnp.max(sb, axis=1, keepdims=True)
                    p[(u, hh)] = jnp.exp2(sb - m).astype(BF16)
                    mx[(u, hh)] = jnp.broadcast_to(m, (DQ, LANES))

            def values(u):
                vg = vf[pl.ds(k0s[u], 2 * DQ, stride=r), :].astype(BF16)
                res = [_dot(p.pop((u, hh)), jnp.where(heads[hh], vg, jnp.ones_like(vg))) for hh in range(2)]
                acc_s[g, rows[u], :] = jnp.where(head0, res[0], res[1])
                m_s[g, rows[u], :] = jnp.where(head0, mx.pop((u, 0)), mx.pop((u, 1)))
                l_s[g, rows[u], :] = pltpu.roll(jnp.where(head0, res[1], res[0]), HEAD_DIM, 1)

            for step in range(DIL_UNROLL + 2 * DIL_SKEW):
                if step < DIL_UNROLL:
                    scores(step)
                if 0 <= step - DIL_SKEW < DIL_UNROLL:
                    softmax(step - DIL_SKEW)
                if step >= 2 * DIL_SKEW:
                    values(step - 2 * DIL_SKEW)
            return carry

        lax.fori_loop(0, r * nblk // DIL_UNROLL, group, 0)

    def merge(ci, carry):
        rows = pl.ds(pl.multiple_of(ci * TQ, TQ), TQ)
        ms = [m_s[g, rows, :] for g in range(len(DIL_BRANCHES))]
        m = functools.reduce(jnp.maximum, ms)
        num = jnp.zeros((TQ, LANES), F32)
        den = jnp.zeros((TQ, LANES), F32)
        for g in range(len(DIL_BRANCHES)):
            wgt = jnp.exp2(ms[g] - m)
            num = num + wgt * acc_s[g, rows, :]
            den = den + wgt * l_s[g, rows, :]
        o_ref[0, rows, :] = (num / den).astype(o_ref.dtype)
        return carry

    lax.fori_loop(0, DT // TQ, merge, 0)


def _dil_bias(t5_table):
    qi = np.arange(DQ)[:, None]
    ki = np.arange(2 * DQ)[None, :] - DQ
    steps = qi - ki
    valid = (steps >= 0) & (steps <= DIL_WINDOW)
    table = t5_table.astype(F32) * LOG2E
    tiles = [jnp.stack([_bias_tile(table, steps * r, v) for v in (valid, valid & (ki >= 0))], axis=1)
             for _, r in DIL_BRANCHES]
    return jnp.stack(tiles, axis=1).reshape(DIL_HEADS // 2, 2, len(DIL_BRANCHES), 2, DQ, 2 * DQ)


def _dilated_attention(proj3, t5_table):
    b, s, _ = proj3.shape
    pairs = DIL_HEADS // 2
    bias = _dil_bias(t5_table)
    cur = lambda off: (lambda bb, hp, t: (bb, t, off * pairs + hp))
    prev = lambda off: (lambda bb, hp, t: (bb, jnp.maximum(t - 1, 0), off * pairs + hp))
    tile = lambda imap: pl.BlockSpec((1, DT, LANES), imap)
    slab = lambda rows: pltpu.VMEM((rows, LANES), F32)
    per_branch = pltpu.VMEM((len(DIL_BRANCHES), DT, LANES), F32)
    return pl.pallas_call(
        _dil_kernel,
        grid=(b, pairs, s // DT),
        in_specs=[tile(cur(0)), tile(prev(1)), tile(cur(1)), tile(prev(2)), tile(cur(2)),
                  pl.BlockSpec((1, 2, len(DIL_BRANCHES), 2, DQ, 2 * DQ), lambda bb, hp, t: (hp, 0, 0, 0, 0, 0))],
        out_specs=tile(cur(0)),
        out_shape=jax.ShapeDtypeStruct((b, s, D_MODEL), BF16),
        scratch_shapes=[slab(DT), slab(2 * DT), slab(2 * DT), per_branch, per_branch, per_branch],
        compiler_params=_cparams(("parallel", "parallel", "arbitrary")),
        name="dilated_attention",
    )(proj3, proj3, proj3, proj3, proj3, bias)


def _post_kernel(*refs, n_o, final):
    h_ref = refs[0]
    o_refs = refs[1:1 + n_o]
    (wo_ref, p_ref, g_mlp_ref, w_up_ref, w_down_ref, g_ple_ref, w_gate_ref, w_proj_ref, g_fin_ref,
     out_ref, acc_ref) = refs[1 + n_o:]
    o = o_refs[0][...] if n_o == 1 else jnp.concatenate([r[...] for r in o_refs], axis=1)
    h = h_ref[...] + _dot(o, wo_ref[...])
    hn = _rms(h, g_mlp_ref[...], NORM_EPS).astype(BF16)
    acc_ref[...] = h
    for c in range(D_FF // FF_CHUNK):
        u = jnp.maximum(_dot(hn, w_up_ref[:, c * FF_CHUNK:(c + 1) * FF_CHUNK]), 0.0)
        acc_ref[...] += _dot((u * u).astype(BF16), w_down_ref[c * FF_CHUNK:(c + 1) * FF_CHUNK, :])
    h = acc_ref[...]
    gate = jax.nn.sigmoid(_dot(_rms(h, g_ple_ref[...], NORM_EPS).astype(BF16), w_gate_ref[...]))
    h = h + _dot(p_ref[...].astype(BF16), w_proj_ref[...]) * gate
    if final:
        h = _rms(h, g_fin_ref[...], NORM_EPS)
    out_ref[...] = h


def _post(h2d, o_list, w_out, p2d, g_mlp, w_up, w_down, g_ple, w_gate, w_proj, g_fin, final):
    m, d = h2d.shape
    n_o = len(o_list)
    row = lambda width: pl.BlockSpec((TM_POST, width), lambda i: (i, 0))
    const = lambda a: pl.BlockSpec(a.shape, lambda i: (0,) * a.ndim, pipeline_mode=pl.Buffered(1))
    vec = lambda g: g.reshape(1, d).astype(F32)
    weights = [w_out, vec(g_mlp), w_up, w_down, vec(g_ple), w_gate, w_proj, vec(g_fin)]
    return pl.pallas_call(
        functools.partial(_post_kernel, n_o=n_o, final=final),
        grid=(m // TM_POST,),
        in_specs=([row(d)] + [row(o.shape[1]) for o in o_list] + [const(weights[0])]
                  + [row(p2d.shape[1])] + [const(w) for w in weights[1:]]),
        out_specs=row(d),
        out_shape=jax.ShapeDtypeStruct((m, d), F32),
        scratch_shapes=[pltpu.VMEM((TM_POST, d), F32)],
        compiler_params=_cparams(("parallel",)),
        name="out_mlp_ple",
    )(h2d, *o_list, weights[0], p2d, *weights[1:])


def _scale_cols(w_in, col_scale):
    return (w_in * jnp.asarray(col_scale, F32)[None, :]).astype(BF16)


def kernel(x, p, t5_table, w_in_even, w_out_even, lambda_q1, lambda_k1, lambda_q2, lambda_k2, subln_g,
           w_in_odd, w_out_odd, norm_mix_g, norm_mlp_g, w_mlp_up, w_mlp_down, norm_ple_g, w_ple_gate,
           w_ple_proj, final_norm_g):
    b, s, d = x.shape
    depth = p.shape[0]
    assert d == D_MODEL and s % DT == 0 and (b * s) % TM_POST == 0
    assert all(w == DIL_WINDOW * r and w <= DT for w, r in DIL_BRANCHES)
    col = np.arange(3 * D_MODEL)
    qk_scale = HEAD_DIM ** -0.5
    even_q = (col < SB_WIDTH) | ((col >= 3 * SB_WIDTH) & (col < 3 * SB_WIDTH + DIFF_WIDTH))
    even_scale = np.where(even_q, qk_scale * LOG2E, 1.0)
    odd_scale = np.where(col < DIL_HEADS * HEAD_DIM, qk_scale * LOG2E, 1.0)
    h = x.reshape(b * s, d)
    for i in range(depth):
        if i % 2 == 0:
            e = i // 2
            w_in = _scale_cols(w_in_even[e], even_scale)
            qa, ka, va, qb, kb, vb = (w_in[:, lo:lo + SB_WIDTH] for lo in range(0, 6 * SB_WIDTH, SB_WIDTH))
            proj, t_sb, t_d = _norm_matmul(
                h, norm_mix_g[i], jnp.concatenate([ka, kb], axis=1),
                ((jnp.concatenate([qa, va], axis=1).T, TS), (jnp.concatenate([qb, vb], axis=1).T, TD)))
            proj = proj.reshape(b, s, -1)
            t_sb = t_sb.reshape(b, s // TS, 2 * SB_WIDTH, TS)
            t_d = t_d.reshape(b, s // TD, 2 * DIFF_WIDTH, TD)
            lambda_init = 0.8 - 0.6 * math.exp(-0.3 * i)
            lam_params = jnp.stack([lambda_q1[e], lambda_k1[e], lambda_q2[e], lambda_k2[e]]).astype(F32)
            o_sb = _sb_attention(proj, t_sb)
            o_d = _diff_attention(proj, t_d, _diff_bias(t5_table), lam_params, subln_g[e].astype(F32),
                                  lambda_init)
            w_out = w_out_even[e].astype(BF16)
            o_list = [o_sb.reshape(b * s, SB_WIDTH), o_d.reshape(b * s, DIFF_WIDTH)]
        else:
            o = i // 2
            (proj,) = _norm_matmul(h, norm_mix_g[i], _scale_cols(w_in_odd[o], odd_scale))
            proj = proj.reshape(b, s, -1)
            o_list = [_dilated_attention(proj, t5_table).reshape(b * s, d)]
            w_out = w_out_odd[o].astype(BF16)
        h = _post(h, o_list, w_out, p[i].reshape(b * s, PLE_DIM), norm_mlp_g[i],
                  w_mlp_up[i].astype(BF16), w_mlp_down[i].astype(BF16), norm_ple_g[i],
                  w_ple_gate[i].astype(BF16), w_ple_proj[i].astype(BF16), final_norm_g,
                  final=(i == depth - 1))
    return h.reshape(b, s, d)
```

```python
import functools
import math

import numpy as np
import jax
import jax.numpy as jnp
from jax import lax
from jax.experimental import pallas as pl
from jax.experimental.pallas import tpu as pltpu

F32 = jnp.float32
BF16 = jnp.bfloat16

D_MODEL = 1024
HEAD_DIM = 64
LANES = 128
SB_HEADS = 8
DIFF_HEADS = 4
DIL_HEADS = 16
SB_WIDTH = SB_HEADS * HEAD_DIM
DIFF_WIDTH = DIFF_HEADS * 2 * HEAD_DIM
DIL_BRANCHES = ((128, 1), (512, 4), (2048, 16))
DIL_WINDOW = 128
NUM_BUCKETS = 32
MAX_DISTANCE = 128
D_FF = 4 * D_MODEL
PLE_DIM = 256
NORM_EPS = 1e-6
SUBLN_EPS = 1e-5
NEG = -1e30
LOG2E = math.log2(math.e)
SB_DEAD_LOG = 100.0

VMEM_LIMIT_BYTES = 56 * 1024 * 1024

TM_PROJ = 512
TM_POST = 512
FF_CHUNK = 1024
TQ = 256
TS = 256
SB_QS = 4
TD = TM_PROJ
DIFF_SUM_ROWS = 16
DIFF_CHUNK = TD
DT = 2048
DQ = DIL_WINDOW
DIL_UNROLL = 16
DIL_SKEW = 3


def _cparams(sem):
    return pltpu.CompilerParams(dimension_semantics=sem, vmem_limit_bytes=VMEM_LIMIT_BYTES)


def _rms(x, g, eps):
    return x * lax.rsqrt(jnp.mean(x * x, axis=-1, keepdims=True) + eps) * g


def _dot(a, b):
    return jnp.dot(a, b, preferred_element_type=F32)


def _dot_nt(a, b):
    return lax.dot_general(a, b, (((1,), (1,)), ((), ())), preferred_element_type=F32)


def _norm_matmul_kernel(h_ref, g_ref, w_ref, *rest, slabs):
    hn = _rms(h_ref[...], g_ref[...], NORM_EPS).astype(BF16)
    n_t = len(slabs)
    wt_refs, o_ref, ot_refs = rest[:n_t], rest[n_t], rest[n_t + 1:]
    for wt_ref, ot_ref, width in zip(wt_refs, ot_refs, slabs):
        res = _dot_nt(wt_ref[...], hn).astype(ot_ref.dtype)
        for c in range(TM_PROJ // width):
            ot_ref[c] = res[:, c * width:(c + 1) * width]
    o_ref[...] = _dot(hn, w_ref[...]).astype(o_ref.dtype)


def _norm_matmul(h2d, g, w, wts=()):
    m, k = h2d.shape
    n = w.shape[1]
    const = lambda a: pl.BlockSpec(a.shape, lambda i: (0,) * a.ndim, pipeline_mode=pl.Buffered(1))
    in_specs = [pl.BlockSpec((TM_PROJ, k), lambda i: (i, 0)), pl.BlockSpec((1, k), lambda i: (0, 0)), const(w)]
    in_specs += [const(wt) for wt, _ in wts]
    out_specs = [pl.BlockSpec((TM_PROJ, n), lambda i: (i, 0))]
    out_shape = [jax.ShapeDtypeStruct((m, n), BF16)]
    for wt, width in wts:
        per_step = TM_PROJ // width
        out_specs.append(pl.BlockSpec((per_step, wt.shape[0], width), lambda i: (i, 0, 0)))
        out_shape.append(jax.ShapeDtypeStruct((m // width, wt.shape[0], width), BF16))
    return pl.pallas_call(
        functools.partial(_norm_matmul_kernel, slabs=tuple(width for _, width in wts)),
        grid=(m // TM_PROJ,),
        in_specs=in_specs, out_specs=out_specs, out_shape=out_shape,
        compiler_params=_cparams(("parallel",)),
        name="norm_in_proj",
    )(h2d, g.reshape(1, k), w, *[wt for wt, _ in wts])


def _t5_bucket_np(dist):
    n = np.maximum(dist, 0)
    max_exact = NUM_BUCKETS // 2
    nf = np.maximum(n, 1).astype(np.float32)
    large = max_exact + (np.log(nf / np.float32(max_exact)) / np.float32(math.log(MAX_DISTANCE / max_exact))
                         * np.float32(NUM_BUCKETS - max_exact)).astype(np.int32)
    large = np.minimum(large, NUM_BUCKETS - 1)
    return np.where(n < max_exact, n, large).astype(np.int32)


def _bias_tile(table_cols, dist, valid):
    bucket = _t5_bucket_np(dist)
    out = jnp.zeros((table_cols.shape[1],) + dist.shape, F32)
    for bkt in np.unique(bucket[valid]):
        out = jnp.where(jnp.asarray(bucket == bkt)[None], table_cols[bkt][:, None, None], out)
    return jnp.where(jnp.asarray(valid)[None], out, NEG)


def _sb_kernel(qt_ref, k_ref, vt_ref, o_ref, acc_ref, c_ref):
    i = pl.program_id(2)
    row = lax.broadcasted_iota(jnp.int32, (LANES, 1), 0)
    head_rows = (row < HEAD_DIM, row >= HEAD_DIM)
    qts = [tuple(jnp.where(hr, qt_ref[0, a], jnp.zeros((LANES, TS), BF16)) for hr in head_rows)
           for a in range(SB_QS)]
    kk = lax.broadcasted_iota(jnp.int32, (TS, TS), 0)
    qq = lax.broadcasted_iota(jnp.int32, (TS, TS), 1)
    tri = (kk <= qq).astype(BF16)
    tri2 = jnp.concatenate([tri, tri], axis=1)
    strict = kk < qq

    def tiles(work, c_in):
        items = [(n, hh) for n in range(len(work)) for hh in range(2)]
        ks = [k_ref[0, pl.ds(pl.multiple_of(j * TS, TS), TS), :] for _, j, _ in work]
        z = {(n, hh): _dot(ks[n], qts[work[n][0]][hh]) for n, hh in items}
        parts = {}
        for it in items:
            keep = work[it[0]][2]
            if keep is not None:
                z[it] = jnp.where(keep, z[it], NEG)
            lk = -(jnp.maximum(z[it], 0.0) + jnp.log2(1.0 + jnp.exp2(-jnp.abs(z[it]))))
            hi = lk.astype(BF16)
            parts[it] = jnp.concatenate([hi, (lk - hi.astype(F32)).astype(BF16)], axis=0)
        incl = {it: _dot(tri2, parts[it]) for it in items}
        c = dict(c_in)
        w = {}
        for n, hh in items:
            a = work[n][0]
            w[(n, hh)] = jnp.exp2(z[(n, hh)] + incl[(n, hh)] + c[(a, hh)]).astype(BF16)
            c[(a, hh)] = c[(a, hh)] + incl[(n, hh)][0:1, :]
        upd = {}
        for n, hh in items:
            a, j, _ = work[n]
            vts = vt_ref[0, j]
            vth = jnp.where(head_rows[hh], vts, jnp.zeros_like(vts))
            part = _dot(vth, w[(n, hh)])
            upd[a] = part if a not in upd else upd[a] + part
        return upd, c

    zero_c = jnp.zeros((1, TS), F32)
    work = []
    for a in range(SB_QS):
        g = i * SB_QS + a
        work.append((a, g, strict))
        work.append((a, g - 1, None) if a > 0 else (a, jnp.maximum(g - 1, 0), kk < jnp.where(g >= 1, TS, 0)))
    upd, c1 = tiles(work, {(a, hh): zero_c for a in range(SB_QS) for hh in range(2)})
    for a in range(SB_QS):
        acc_ref[a] = upd[a]
        for hh in range(2):
            c_ref[a, hh] = c1[(a, hh)]

    for a in range(SB_QS):
        def cond(state):
            j, alive = state
            return jnp.logical_and(j >= 0, alive > -SB_DEAD_LOG * LOG2E)

        def body(state, a=a):
            j, _ = state
            upd, c_new = tiles([(a, j, None)], {(a, hh): c_ref[a, hh] for hh in range(2)})
            acc_ref[a] += upd[a]
            for hh in range(2):
                c_ref[a, hh] = c_new[(a, hh)]
            return j - 1, jnp.maximum(jnp.max(c_new[(a, 0)]), jnp.max(c_new[(a, 1)]))

        alive1 = jnp.maximum(jnp.max(c1[(a, 0)]), jnp.max(c1[(a, 1)]))
        lax.while_loop(cond, body, (i * SB_QS + a - 2, alive1))
        o_ref[0, a * TS:(a + 1) * TS, :] = jnp.transpose(acc_ref[a]).astype(o_ref.dtype)


def _sb_attention(proj3, proj_t):
    b, s, _ = proj3.shape
    pairs = SB_WIDTH // LANES
    nt = s // TS
    return pl.pallas_call(
        _sb_kernel,
        grid=(b, pairs, nt // SB_QS),
        in_specs=[pl.BlockSpec((1, SB_QS, LANES, TS), lambda bb, hp, i: (bb, i, hp, 0)),
                  pl.BlockSpec((1, s, LANES), lambda bb, hp, i: (bb, 0, hp)),
                  pl.BlockSpec((1, nt, LANES, TS), lambda bb, hp, i: (bb, 0, pairs + hp, 0))],
        out_specs=pl.BlockSpec((1, SB_QS * TS, LANES), lambda bb, hp, i: (bb, i, hp)),
        out_shape=jax.ShapeDtypeStruct((b, s, SB_WIDTH), BF16),
        scratch_shapes=[pltpu.VMEM((SB_QS, LANES, TS), F32), pltpu.VMEM((SB_QS, 2, 1, TS), F32)],
        compiler_params=_cparams(("parallel", "parallel", "arbitrary")),
        name="sb_attention",
    )(proj_t, proj3, proj_t)


def _diff_kernel(qt_ref, k_ref, vt_ref, bias_ref, lam_ref, g_ref, o_ref, m_ref, acc_ref, sa_ref, sb_ref,
                 *, lambda_init):
    i = pl.program_id(2)
    qt = qt_ref[0, 0]
    row = lax.broadcasted_iota(jnp.int32, (LANES, 1), 0)
    map0 = row < HEAD_DIM
    zero = jnp.zeros_like(qt)
    qts = (jnp.where(map0, qt, zero), jnp.where(map0, zero, qt))

    m_ref[...] = jnp.full_like(m_ref, NEG)
    acc_ref[...] = jnp.zeros_like(acc_ref)

    def absorb(j, s_ref, bias_idx=None, offset=None, nxt=None):
        vts = jnp.concatenate([vt_ref[0, j], jnp.ones((DIFF_SUM_ROWS, TD), BF16)], axis=0)
        s, alpha, pv = [], [], [None, None]
        for mm in range(2):
            sm = s_ref[mm]
            if bias_idx is not None:
                sm = sm + bias_ref[0, mm, bias_idx]
            if offset is not None:
                sm = sm + offset
            m_old = m_ref[mm]
            m_new = jnp.maximum(m_old, jnp.max(sm, axis=0, keepdims=True))
            m_ref[mm] = m_new
            alpha.append(jnp.exp2(m_old - m_new))
            s.append(sm - m_new)
        for c in range(TD // DIFF_CHUNK):
            rows = slice(c * DIFF_CHUNK, (c + 1) * DIFF_CHUNK)
            if nxt is not None:
                nxt_ref, jn = nxt
                ks = k_ref[0, pl.ds(pl.multiple_of(jn * TD + c * DIFF_CHUNK, DIFF_CHUNK), DIFF_CHUNK), :]
                for mm in range(2):
                    nxt_ref[mm, rows, :] = _dot(ks, qts[mm])
            for mm in range(2):
                part = _dot(vts[:, rows], jnp.exp2(s[mm][rows]).astype(BF16))
                pv[mm] = part if pv[mm] is None else pv[mm] + part
        for mm in range(2):
            acc_ref[mm] = alpha[mm] * acc_ref[mm] + pv[mm]

    ks = k_ref[0, pl.ds(pl.multiple_of(i * TD, TD), TD), :]
    for mm in range(2):
        sa_ref[mm] = _dot(ks, qts[mm])
    absorb(i, sa_ref, bias_idx=0, nxt=(sb_ref, jnp.maximum(i - 1, 0)))
    absorb(jnp.maximum(i - 1, 0), sb_ref, bias_idx=1, offset=jnp.where(i >= 1, 0.0, NEG),
           nxt=(sa_ref, jnp.maximum(i - 2, 0)))

    @pl.when(i >= 2)
    def _():
        def body(u, carry):
            ja = i - 2 - 2 * u
            absorb(ja, sa_ref, nxt=(sb_ref, ja - 1))
            absorb(ja - 1, sb_ref, nxt=(sa_ref, jnp.maximum(ja - 2, 0)))
            return carry

        lax.fori_loop(0, (i - 1) // 2, body, 0)

        @pl.when((i - 1) % 2 == 1)
        def _():
            absorb(0, sa_ref)

    lam = (jnp.exp(jnp.sum(lam_ref[0:1, :] * lam_ref[1:2, :], axis=1, keepdims=True))
           - jnp.exp(jnp.sum(lam_ref[2:3, :] * lam_ref[3:4, :], axis=1, keepdims=True)) + lambda_init)
    l0, l1 = acc_ref[0, LANES:LANES + 1, :], acc_ref[1, LANES:LANES + 1, :]
    ot = acc_ref[0, 0:LANES, :] / l0 - lam * (acc_ref[1, 0:LANES, :] / l1)
    o = jnp.transpose(ot)
    o_ref[0] = (_rms(o, g_ref[...], SUBLN_EPS) * (1.0 - lambda_init)).astype(o_ref.dtype)


def _diff_attention(proj3, proj_t, bias, lam_params, subln_g, lambda_init):
    b, s, _ = proj3.shape
    kbase = SB_WIDTH // LANES
    nt = s // TD
    return pl.pallas_call(
        functools.partial(_diff_kernel, lambda_init=lambda_init),
        grid=(b, DIFF_HEADS, nt),
        in_specs=[pl.BlockSpec((1, 1, LANES, TD), lambda bb, h, i: (bb, i, h, 0)),
                  pl.BlockSpec((1, s, LANES), lambda bb, h, i: (bb, 0, kbase + h)),
                  pl.BlockSpec((1, nt, LANES, TD), lambda bb, h, i: (bb, 0, DIFF_HEADS + h, 0)),
                  pl.BlockSpec((1, 2, 2, TD, TD), lambda bb, h, i: (h, 0, 0, 0, 0)),
                  pl.BlockSpec((4, HEAD_DIM), lambda bb, h, i: (0, 0)),
                  pl.BlockSpec((1, LANES), lambda bb, h, i: (0, 0))],
        out_specs=pl.BlockSpec((1, TD, LANES), lambda bb, h, i: (bb, i, h)),
        out_shape=jax.ShapeDtypeStruct((b, s, DIFF_WIDTH), BF16),
        scratch_shapes=[pltpu.VMEM((2, 1, TD), F32), pltpu.VMEM((2, LANES + DIFF_SUM_ROWS, TD), F32),
                        pltpu.VMEM((2, TD, TD), F32), pltpu.VMEM((2, TD, TD), F32)],
        compiler_params=_cparams(("parallel", "parallel", "arbitrary")),
        name="diff_attention",
    )(proj_t, proj3, proj_t, bias, lam_params, subln_g.reshape(1, LANES))


def _diff_bias(t5_table):
    cols = t5_table.astype(F32)[:, SB_HEADS:]
    cols = (cols - cols[NUM_BUCKETS - 1:NUM_BUCKETS, :]) * LOG2E
    nb = TD // MAX_DISTANCE
    ki = np.arange(MAX_DISTANCE)[:, None]
    qi = np.arange(MAX_DISTANCE)[None, :]
    d0 = qi - ki
    diag = _bias_tile(cols, d0, d0 >= 0)
    prev = _bias_tile(cols, d0 + MAX_DISTANCE, np.ones_like(d0, bool))
    zero = jnp.zeros_like(diag)
    masked = jnp.full_like(diag, NEG)

    def block0(kb, qb):
        return masked if kb > qb else diag if kb == qb else prev if kb == qb - 1 else zero

    t0 = jnp.concatenate([jnp.concatenate([block0(kb, qb) for qb in range(nb)], axis=2) for kb in range(nb)], axis=1)
    t1 = jnp.concatenate([jnp.concatenate([prev if (kb == nb - 1 and qb == 0) else zero for qb in range(nb)], axis=2)
                          for kb in range(nb)], axis=1)
    both = jnp.stack([t0, t1], axis=1)
    return both.reshape(DIFF_HEADS, 2, 2, TD, TD)


def _dil_kernel(q_ref, kp_ref, kc_ref, vp_ref, vc_ref, bias_ref, o_ref, qf, kf, vf, acc_s, m_s, l_s):
    t_idx = pl.program_id(2)
    qf[...] = q_ref[0].astype(F32)
    kf[0:DT, :] = kp_ref[0].astype(F32)
    kf[DT:2 * DT, :] = kc_ref[0].astype(F32)
    vf[0:DT, :] = vp_ref[0].astype(F32)
    vf[DT:2 * DT, :] = vc_ref[0].astype(F32)
    lane = lax.broadcasted_iota(jnp.int32, (1, LANES), 1)
    head0 = lane < HEAD_DIM
    heads = (head0, jnp.logical_not(head0))

    for g, (_, r) in enumerate(DIL_BRANCHES):
        nblk = DT // (r * DQ)
        shift = nblk.bit_length() - 1

        def group(t0, carry, g=g, r=r, nblk=nblk, shift=shift):
            rows, k0s, pre, s, p, mx = {}, {}, {}, {}, {}, {}

            def scores(u):
                t = t0 * DIL_UNROLL + u
                c = lax.shift_right_logical(t, shift)
                n = jnp.bitwise_and(t, nblk - 1)
                q0 = c + n * (DQ * r)
                k0s[u] = DT + q0 - DQ * r
                rows[u] = pl.ds(q0, DQ, stride=r)
                qg = qf[rows[u], :].astype(BF16)
                kg = kf[pl.ds(k0s[u], 2 * DQ, stride=r), :].astype(BF16)
                for hh in range(2):
                    s[(u, hh)] = _dot_nt(jnp.where(heads[hh], qg, jnp.zeros_like(qg)), kg)
                pre[u] = jnp.where(jnp.logical_and(t_idx == 0, n == 0), 1, 0)

            def softmax(u):
                for hh in range(2):
                    sb = s.pop((u, hh)) + bias_ref[0, hh, g, pre[u]]
                    m = jnp.max(sb, axis=1, keepdims=True)
                    p[(u, hh)] = jnp.exp2(sb - m).astype(BF16)
                    mx[(u, hh)] = jnp.broadcast_to(m, (DQ, LANES))

            def values(u):
                vg = vf[pl.ds(k0s[u], 2 * DQ, stride=r), :].astype(BF16)
                res = [_dot(p.pop((u, hh)), jnp.where(heads[hh], vg, jnp.ones_like(vg))) for hh in range(2)]
                acc_s[g, rows[u], :] = jnp.where(head0, res[0], res[1])
                m_s[g, rows[u], :] = jnp.where(head0, mx.pop((u, 0)), mx.pop((u, 1)))
                l_s[g, rows[u], :] = pltpu.roll(jnp.where(head0, res[1], res[0]), HEAD_DIM, 1)

            for step in range(DIL_UNROLL + 2 * DIL_SKEW):
                if step < DIL_UNROLL:
                    scores(step)
                if 0 <= step - DIL_SKEW < DIL_UNROLL:
                    softmax(step - DIL_SKEW)
                if step >= 2 * DIL_SKEW:
                    values(step - 2 * DIL_SKEW)
            return carry

        lax.fori_loop(0, r * nblk // DIL_UNROLL, group, 0)

    def merge(ci, carry):
        rows = pl.ds(pl.multiple_of(ci * TQ, TQ), TQ)
        ms = [m_s[g, rows, :] for g in range(len(DIL_BRANCHES))]
        m = functools.reduce(jnp.maximum, ms)
        num = jnp.zeros((TQ, LANES), F32)
        den = jnp.zeros((TQ, LANES), F32)
        for g in range(len(DIL_BRANCHES)):
            wgt = jnp.exp2(ms[g] - m)
            num = num + wgt * acc_s[g, rows, :]
            den = den + wgt * l_s[g, rows, :]
        o_ref[0, rows, :] = (num / den).astype(o_ref.dtype)
        return carry

    lax.fori_loop(0, DT // TQ, merge, 0)


def _dil_bias(t5_table):
    qi = np.arange(DQ)[:, None]
    ki = np.arange(2 * DQ)[None, :] - DQ
    steps = qi - ki
    valid = (steps >= 0) & (steps <= DIL_WINDOW)
    table = t5_table.astype(F32) * LOG2E
    tiles = [jnp.stack([_bias_tile(table, steps * r, v) for v in (valid, valid & (ki >= 0))], axis=1)
             for _, r in DIL_BRANCHES]
    return jnp.stack(tiles, axis=1).reshape(DIL_HEADS // 2, 2, len(DIL_BRANCHES), 2, DQ, 2 * DQ)


def _dilated_attention(proj3, t5_table):
    b, s, _ = proj3.shape
    pairs = DIL_HEADS // 2
    bias = _dil_bias(t5_table)
    cur = lambda off: (lambda bb, hp, t: (bb, t, off * pairs + hp))
    prev = lambda off: (lambda bb, hp, t: (bb, jnp.maximum(t - 1, 0), off * pairs + hp))
    tile = lambda imap: pl.BlockSpec((1, DT, LANES), imap)
    slab = lambda rows: pltpu.VMEM((rows, LANES), F32)
    per_branch = pltpu.VMEM((len(DIL_BRANCHES), DT, LANES), F32)
    return pl.pallas_call(
        _dil_kernel,
        grid=(b, pairs, s // DT),
        in_specs=[tile(cur(0)), tile(prev(1)), tile(cur(1)), tile(prev(2)), tile(cur(2)),
                  pl.BlockSpec((1, 2, len(DIL_BRANCHES), 2, DQ, 2 * DQ), lambda bb, hp, t: (hp, 0, 0, 0, 0, 0))],
        out_specs=tile(cur(0)),
        out_shape=jax.ShapeDtypeStruct((b, s, D_MODEL), BF16),
        scratch_shapes=[slab(DT), slab(2 * DT), slab(2 * DT), per_branch, per_branch, per_branch],
        compiler_params=_cparams(("parallel", "parallel", "arbitrary")),
        name="dilated_attention",
    )(proj3, proj3, proj3, proj3, proj3, bias)


def _post_kernel(*refs, n_o, final):
    h_ref = refs[0]
    o_refs = refs[1:1 + n_o]
    (wo_ref, p_ref, g_mlp_ref, w_up_ref, w_down_ref, g_ple_ref, w_gate_ref, w_proj_ref, g_fin_ref,
     out_ref, acc_ref) = refs[1 + n_o:]
    o = o_refs[0][...] if n_o == 1 else jnp.concatenate([r[...] for r in o_refs], axis=1)
    h = h_ref[...] + _dot(o, wo_ref[...])
    hn = _rms(h, g_mlp_ref[...], NORM_EPS).astype(BF16)
    acc_ref[...] = h
    for c in range(D_FF // FF_CHUNK):
        u = jnp.maximum(_dot(hn, w_up_ref[:, c * FF_CHUNK:(c + 1) * FF_CHUNK]), 0.0)
        acc_ref[...] += _dot((u * u).astype(BF16), w_down_ref[c * FF_CHUNK:(c + 1) * FF_CHUNK, :])
    h = acc_ref[...]
    gate = jax.nn.sigmoid(_dot(_rms(h, g_ple_ref[...], NORM_EPS).astype(BF16), w_gate_ref[...]))
    h = h + _dot(p_ref[...].astype(BF16), w_proj_ref[...]) * gate
    if final:
        h = _rms(h, g_fin_ref[...], NORM_EPS)
    out_ref[...] = h


def _post(h2d, o_list, w_out, p2d, g_mlp, w_up, w_down, g_ple, w_gate, w_proj, g_fin, final):
    m, d = h2d.shape
    n_o = len(o_list)
    row = lambda width: pl.BlockSpec((TM_POST, width), lambda i: (i, 0))
    const = lambda a: pl.BlockSpec(a.shape, lambda i: (0,) * a.ndim, pipeline_mode=pl.Buffered(1))
    vec = lambda g: g.reshape(1, d).astype(F32)
    weights = [w_out, vec(g_mlp), w_up, w_down, vec(g_ple), w_gate, w_proj, vec(g_fin)]
    return pl.pallas_call(
        functools.partial(_post_kernel, n_o=n_o, final=final),
        grid=(m // TM_POST,),
        in_specs=([row(d)] + [row(o.shape[1]) for o in o_list] + [const(weights[0])]
                  + [row(p2d.shape[1])] + [const(w) for w in weights[1:]]),
        out_specs=row(d),
        out_shape=jax.ShapeDtypeStruct((m, d), F32),
        scratch_shapes=[pltpu.VMEM((TM_POST, d), F32)],
        compiler_params=_cparams(("parallel",)),
        name="out_mlp_ple",
    )(h2d, *o_list, weights[0], p2d, *weights[1:])


def _scale_cols(w_in, col_scale):
    return (w_in * jnp.asarray(col_scale, F32)[None, :]).astype(BF16)


def kernel(x, p, t5_table, w_in_even, w_out_even, lambda_q1, lambda_k1, lambda_q2, lambda_k2, subln_g,
           w_in_odd, w_out_odd, norm_mix_g, norm_mlp_g, w_mlp_up, w_mlp_down, norm_ple_g, w_ple_gate,
           w_ple_proj, final_norm_g):
    b, s, d = x.shape
    depth = p.shape[0]
    assert d == D_MODEL and s % DT == 0 and (b * s) % TM_POST == 0
    assert all(w == DIL_WINDOW * r and w <= DT for w, r in DIL_BRANCHES)
    col = np.arange(3 * D_MODEL)
    qk_scale = HEAD_DIM ** -0.5
    even_q = (col < SB_WIDTH) | ((col >= 3 * SB_WIDTH) & (col < 3 * SB_WIDTH + DIFF_WIDTH))
    even_scale = np.where(even_q, qk_scale * LOG2E, 1.0)
    odd_scale = np.where(col < DIL_HEADS * HEAD_DIM, qk_scale * LOG2E, 1.0)
    h = x.reshape(b * s, d)
    for i in range(depth):
        if i % 2 == 0:
            e = i // 2
            w_in = _scale_cols(w_in_even[e], even_scale)
            qa, ka, va, qb, kb, vb = (w_in[:, lo:lo + SB_WIDTH] for lo in range(0, 6 * SB_WIDTH, SB_WIDTH))
            proj, t_sb, t_d = _norm_matmul(
                h, norm_mix_g[i], jnp.concatenate([ka, kb], axis=1),
                ((jnp.concatenate([qa, va], axis=1).T, TS), (jnp.concatenate([qb, vb], axis=1).T, TD)))
            proj = proj.reshape(b, s, -1)
            t_sb = t_sb.reshape(b, s // TS, 2 * SB_WIDTH, TS)
            t_d = t_d.reshape(b, s // TD, 2 * DIFF_WIDTH, TD)
            lambda_init = 0.8 - 0.6 * math.exp(-0.3 * i)
            lam_params = jnp.stack([lambda_q1[e], lambda_k1[e], lambda_q2[e], lambda_k2[e]]).astype(F32)
            o_sb = _sb_attention(proj, t_sb)
            o_d = _diff_attention(proj, t_d, _diff_bias(t5_table), lam_params, subln_g[e].astype(F32),
                                  lambda_init)
            w_out = w_out_even[e].astype(BF16)
            o_list = [o_sb.reshape(b * s, SB_WIDTH), o_d.reshape(b * s, DIFF_WIDTH)]
        else:
            o = i // 2
            (proj,) = _norm_matmul(h, norm_mix_g[i], _scale_cols(w_in_odd[o], odd_scale))
            proj = proj.reshape(b, s, -1)
            o_list = [_dilated_attention(proj, t5_table).reshape(b * s, d)]
            w_out = w_out_odd[o].astype(BF16)
        h = _post(h, o_list, w_out, p[i].reshape(b * s, PLE_DIM), norm_mlp_g[i],
                  w_mlp_up[i].astype(BF16), w_mlp_down[i].astype(BF16), norm_ple_g[i],
                  w_ple_gate[i].astype(BF16), w_ple_proj[i].astype(BF16), final_norm_g,
                  final=(i == depth - 1))
    return h.reshape(b, s, d)
```

```python
import functools
import math

import numpy as np
import jax
import jax.numpy as jnp
from jax import lax
from jax.experimental import pallas as pl
from jax.experimental.pallas import tpu as pltpu

F32 = jnp.float32
BF16 = jnp.bfloat16

D_MODEL = 1024
HEAD_DIM = 64
LANES = 128
SB_HEADS = 8
DIFF_HEADS = 4
DIL_HEADS = 16
SB_WIDTH = SB_HEADS * HEAD_DIM
DIFF_WIDTH = DIFF_HEADS * 2 * HEAD_DIM
DIL_BRANCHES = ((128, 1), (512, 4), (2048, 16))
DIL_WINDOW = 128
NUM_BUCKETS = 32
MAX_DISTANCE = 128
D_FF = 4 * D_MODEL
PLE_DIM = 256
NORM_EPS = 1e-6
SUBLN_EPS = 1e-5
NEG = -1e30
LOG2E = math.log2(math.e)
SB_DEAD_LOG = 88.0

VMEM_LIMIT_BYTES = 56 * 1024 * 1024

TM_PROJ = 512
TM_POST = 512
FF_CHUNK = 1024
TQ = 256
TS = 256
SB_QS = 4
SB_HALF = 128
TD = TM_PROJ
DIFF_SUM_ROWS = 16
DIFF_CHUNK = TD
DT = 2048
DQ = DIL_WINDOW
DIL_UNROLL = 16
DIL_SKEW = 3


def _cparams(sem):
    return pltpu.CompilerParams(dimension_semantics=sem, vmem_limit_bytes=VMEM_LIMIT_BYTES)


def _rms(x, g, eps):
    return x * lax.rsqrt(jnp.mean(x * x, axis=-1, keepdims=True) + eps) * g


def _dot(a, b):
    return jnp.dot(a, b, preferred_element_type=F32)


def _dot_nt(a, b):
    return lax.dot_general(a, b, (((1,), (1,)), ((), ())), preferred_element_type=F32)


def _norm_matmul_kernel(h_ref, g_ref, w_ref, *rest, slabs):
    hn = _rms(h_ref[...], g_ref[...], NORM_EPS).astype(BF16)
    n_t = len(slabs)
    wt_refs, o_ref, ot_refs = rest[:n_t], rest[n_t], rest[n_t + 1:]
    for wt_ref, ot_ref, width in zip(wt_refs, ot_refs, slabs):
        res = _dot_nt(wt_ref[...], hn).astype(ot_ref.dtype)
        for c in range(TM_PROJ // width):
            ot_ref[c] = res[:, c * width:(c + 1) * width]
    o_ref[...] = _dot(hn, w_ref[...]).astype(o_ref.dtype)


def _norm_matmul(h2d, g, w, wts=()):
    m, k = h2d.shape
    n = w.shape[1]
    const = lambda a: pl.BlockSpec(a.shape, lambda i: (0,) * a.ndim, pipeline_mode=pl.Buffered(1))
    in_specs = [pl.BlockSpec((TM_PROJ, k), lambda i: (i, 0)), pl.BlockSpec((1, k), lambda i: (0, 0)), const(w)]
    in_specs += [const(wt) for wt, _ in wts]
    out_specs = [pl.BlockSpec((TM_PROJ, n), lambda i: (i, 0))]
    out_shape = [jax.ShapeDtypeStruct((m, n), BF16)]
    for wt, width in wts:
        per_step = TM_PROJ // width
        out_specs.append(pl.BlockSpec((per_step, wt.shape[0], width), lambda i: (i, 0, 0)))
        out_shape.append(jax.ShapeDtypeStruct((m // width, wt.shape[0], width), BF16))
    return pl.pallas_call(
        functools.partial(_norm_matmul_kernel, slabs=tuple(width for _, width in wts)),
        grid=(m // TM_PROJ,),
        in_specs=in_specs, out_specs=out_specs, out_shape=out_shape,
        compiler_params=_cparams(("parallel",)),
        name="norm_in_proj",
    )(h2d, g.reshape(1, k), w, *[wt for wt, _ in wts])


def _t5_bucket_np(dist):
    n = np.maximum(dist, 0)
    max_exact = NUM_BUCKETS // 2
    nf = np.maximum(n, 1).astype(np.float32)
    large = max_exact + (np.log(nf / np.float32(max_exact)) / np.float32(math.log(MAX_DISTANCE / max_exact))
                         * np.float32(NUM_BUCKETS - max_exact)).astype(np.int32)
    large = np.minimum(large, NUM_BUCKETS - 1)
    return np.where(n < max_exact, n, large).astype(np.int32)


def _bias_tile(table_cols, dist, valid):
    bucket = _t5_bucket_np(dist)
    out = jnp.zeros((table_cols.shape[1],) + dist.shape, F32)
    for bkt in np.unique(bucket[valid]):
        out = jnp.where(jnp.asarray(bucket == bkt)[None], table_cols[bkt][:, None, None], out)
    return jnp.where(jnp.asarray(valid)[None], out, NEG)


def _sb_kernel(qt_ref, k_ref, vt_ref, o_ref, acc_ref, c_ref):
    i = pl.program_id(2)
    row = lax.broadcasted_iota(jnp.int32, (LANES, 1), 0)
    head_rows = (row < HEAD_DIM, row >= HEAD_DIM)
    qts = [tuple(jnp.where(hr, qt_ref[0, a], jnp.zeros((LANES, TS), BF16)) for hr in head_rows)
           for a in range(SB_QS)]
    kk = lax.broadcasted_iota(jnp.int32, (TS, TS), 0)
    qq = lax.broadcasted_iota(jnp.int32, (TS, TS), 1)
    strict = kk < qq
    per_half = TS // SB_HALF

    def tri2(n):
        tri = (lax.broadcasted_iota(jnp.int32, (n, n), 0) <= lax.broadcasted_iota(jnp.int32, (n, n), 1)).astype(BF16)
        return jnp.concatenate([tri, tri], axis=1)

    tri2s = {n: tri2(n) for n in (TS, SB_HALF)}

    def tiles(work, c_in):
        items = [(n, hh) for n in range(len(work)) for hh in range(2)]
        ks = [k_ref[0, pl.ds(pl.multiple_of(k0, SB_HALF), nk), :] for _, k0, nk, _, _ in work]
        z = {(n, hh): _dot(ks[n], qts[work[n][0]][hh]) for n, hh in items}
        parts = {}
        for it in items:
            keep = work[it[0]][4]
            if keep is not None:
                z[it] = jnp.where(keep, z[it], NEG)
            lk = -(jnp.maximum(z[it], 0.0) + jnp.log2(1.0 + jnp.exp2(-jnp.abs(z[it]))))
            hi = lk.astype(BF16)
            parts[it] = jnp.concatenate([hi, (lk - hi.astype(F32)).astype(BF16)], axis=0)
        incl = {it: _dot(tri2s[work[it[0]][2]], parts[it]) for it in items}
        c = dict(c_in)
        w = {}
        for n, hh in items:
            a = work[n][0]
            w[(n, hh)] = jnp.exp2(z[(n, hh)] + incl[(n, hh)] + c[(a, hh)]).astype(BF16)
            c[(a, hh)] = c[(a, hh)] + incl[(n, hh)][0:1, :]
        per_head = {}
        for n, hh in items:
            a = work[n][0]
            part = _dot(work[n][3](), w[(n, hh)])
            per_head[(a, hh)] = part if (a, hh) not in per_head else per_head[(a, hh)] + part
        upd = {a: jnp.where(head_rows[0], per_head[(a, 0)], per_head[(a, 1)]) for a, _ in per_head}
        return upd, c

    def half_tile(a, e, keep=None):
        return (a, e * SB_HALF, SB_HALF, lambda: vt_ref[0, e], keep)

    def diag_vt(g):
        return jnp.concatenate([vt_ref[0, g * per_half + c] for c in range(per_half)], axis=1)

    zero_c = jnp.zeros((1, TS), F32)
    work = []
    for a in range(SB_QS):
        g = i * SB_QS + a
        work.append((a, g * TS, TS, functools.partial(diag_vt, g), strict))
        if a > 0:
            work.append(half_tile(a, g * per_half - 1))
        else:
            exists = lax.broadcasted_iota(jnp.int32, (SB_HALF, TS), 0) < jnp.where(g >= 1, SB_HALF, 0)
            work.append(half_tile(a, jnp.maximum(g * per_half - 1, 0), exists))
    upd, c1 = tiles(work, {(a, hh): zero_c for a in range(SB_QS) for hh in range(2)})
    for a in range(SB_QS):
        acc_ref[a] = upd[a]
        for hh in range(2):
            c_ref[a, hh] = c1[(a, hh)]

    for a in range(SB_QS):
        def cond(state):
            e, alive = state
            return jnp.logical_and(e >= 0, alive > -SB_DEAD_LOG * LOG2E)

        def body(state, a=a):
            e, _ = state
            upd, c_new = tiles([half_tile(a, e)], {(a, hh): c_ref[a, hh] for hh in range(2)})
            acc_ref[a] += upd[a]
            for hh in range(2):
                c_ref[a, hh] = c_new[(a, hh)]
            return e - 1, jnp.maximum(jnp.max(c_new[(a, 0)]), jnp.max(c_new[(a, 1)]))

        alive1 = jnp.maximum(jnp.max(c1[(a, 0)]), jnp.max(c1[(a, 1)]))
        lax.while_loop(cond, body, ((i * SB_QS + a) * per_half - 2, alive1))
        o_ref[0, a * TS:(a + 1) * TS, :] = jnp.transpose(acc_ref[a]).astype(o_ref.dtype)


def _sb_attention(proj3, q_t, v_t):
    b, s, _ = proj3.shape
    pairs = SB_WIDTH // LANES
    return pl.pallas_call(
        _sb_kernel,
        grid=(b, pairs, s // (SB_QS * TS)),
        in_specs=[pl.BlockSpec((1, SB_QS, LANES, TS), lambda bb, hp, i: (bb, i, hp, 0)),
                  pl.BlockSpec((1, s, LANES), lambda bb, hp, i: (bb, 0, hp)),
                  pl.BlockSpec((1, s // SB_HALF, LANES, SB_HALF), lambda bb, hp, i: (bb, 0, hp, 0))],
        out_specs=pl.BlockSpec((1, SB_QS * TS, LANES), lambda bb, hp, i: (bb, i, hp)),
        out_shape=jax.ShapeDtypeStruct((b, s, SB_WIDTH), BF16),
        scratch_shapes=[pltpu.VMEM((SB_QS, LANES, TS), F32), pltpu.VMEM((SB_QS, 2, 1, TS), F32)],
        compiler_params=_cparams(("parallel", "parallel", "arbitrary")),
        name="sb_attention",
    )(q_t, proj3, v_t)


def _diff_kernel(qt_ref, k_ref, vt_ref, bias_ref, lam_ref, g_ref, o_ref, m_ref, acc_ref, sa_ref, sb_ref,
                 *, lambda_init):
    i = pl.program_id(2)
    qt = qt_ref[0, 0]
    row = lax.broadcasted_iota(jnp.int32, (LANES, 1), 0)
    map0 = row < HEAD_DIM
    zero = jnp.zeros_like(qt)
    qts = (jnp.where(map0, qt, zero), jnp.where(map0, zero, qt))

    m_ref[...] = jnp.full_like(m_ref, NEG)
    acc_ref[...] = jnp.zeros_like(acc_ref)

    def absorb(j, s_ref, bias_idx=None, offset=None, nxt=None):
        vts = jnp.concatenate([vt_ref[0, j], jnp.ones((DIFF_SUM_ROWS, TD), BF16)], axis=0)
        s, alpha, pv = [], [], [None, None]
        for mm in range(2):
            sm = s_ref[mm]
            if bias_idx is not None:
                sm = sm + bias_ref[0, mm, bias_idx]
            if offset is not None:
                sm = sm + offset
            m_old = m_ref[mm]
            m_new = jnp.maximum(m_old, jnp.max(sm, axis=0, keepdims=True))
            m_ref[mm] = m_new
            alpha.append(jnp.exp2(m_old - m_new))
            s.append(sm - m_new)
        for c in range(TD // DIFF_CHUNK):
            rows = slice(c * DIFF_CHUNK, (c + 1) * DIFF_CHUNK)
            if nxt is not None:
                nxt_ref, jn = nxt
                ks = k_ref[0, pl.ds(pl.multiple_of(jn * TD + c * DIFF_CHUNK, DIFF_CHUNK), DIFF_CHUNK), :]
                for mm in range(2):
                    nxt_ref[mm, rows, :] = _dot(ks, qts[mm])
            for mm in range(2):
                part = _dot(vts[:, rows], jnp.exp2(s[mm][rows]).astype(BF16))
                pv[mm] = part if pv[mm] is None else pv[mm] + part
        for mm in range(2):
            acc_ref[mm] = alpha[mm] * acc_ref[mm] + pv[mm]

    ks = k_ref[0, pl.ds(pl.multiple_of(i * TD, TD), TD), :]
    for mm in range(2):
        sa_ref[mm] = _dot(ks, qts[mm])
    absorb(i, sa_ref, bias_idx=0, nxt=(sb_ref, jnp.maximum(i - 1, 0)))
    absorb(jnp.maximum(i - 1, 0), sb_ref, bias_idx=1, offset=jnp.where(i >= 1, 0.0, NEG),
           nxt=(sa_ref, jnp.maximum(i - 2, 0)))

    @pl.when(i >= 2)
    def _():
        def body(u, carry):
            ja = i - 2 - 2 * u
            absorb(ja, sa_ref, nxt=(sb_ref, ja - 1))
            absorb(ja - 1, sb_ref, nxt=(sa_ref, jnp.maximum(ja - 2, 0)))
            return carry

        lax.fori_loop(0, (i - 1) // 2, body, 0)

        @pl.when((i - 1) % 2 == 1)
        def _():
            absorb(0, sa_ref)

    lam = (jnp.exp(jnp.sum(lam_ref[0:1, :] * lam_ref[1:2, :], axis=1, keepdims=True))
           - jnp.exp(jnp.sum(lam_ref[2:3, :] * lam_ref[3:4, :], axis=1, keepdims=True)) + lambda_init)
    l0, l1 = acc_ref[0, LANES:LANES + 1, :], acc_ref[1, LANES:LANES + 1, :]
    ot = acc_ref[0, 0:LANES, :] / l0 - lam * (acc_ref[1, 0:LANES, :] / l1)
    o = jnp.transpose(ot)
    o_ref[0] = (_rms(o, g_ref[...], SUBLN_EPS) * (1.0 - lambda_init)).astype(o_ref.dtype)


def _diff_attention(proj3, proj_t, bias, lam_params, subln_g, lambda_init):
    b, s, _ = proj3.shape
    kbase = SB_WIDTH // LANES
    nt = s // TD
    return pl.pallas_call(
        functools.partial(_diff_kernel, lambda_init=lambda_init),
        grid=(b, DIFF_HEADS, nt),
        in_specs=[pl.BlockSpec((1, 1, LANES, TD), lambda bb, h, i: (bb, i, h, 0)),
                  pl.BlockSpec((1, s, LANES), lambda bb, h, i: (bb, 0, kbase + h)),
                  pl.BlockSpec((1, nt, LANES, TD), lambda bb, h, i: (bb, 0, DIFF_HEADS + h, 0)),
                  pl.BlockSpec((1, 2, 2, TD, TD), lambda bb, h, i: (h, 0, 0, 0, 0)),
                  pl.BlockSpec((4, HEAD_DIM), lambda bb, h, i: (0, 0)),
                  pl.BlockSpec((1, LANES), lambda bb, h, i: (0, 0))],
        out_specs=pl.BlockSpec((1, TD, LANES), lambda bb, h, i: (bb, i, h)),
        out_shape=jax.ShapeDtypeStruct((b, s, DIFF_WIDTH), BF16),
        scratch_shapes=[pltpu.VMEM((2, 1, TD), F32), pltpu.VMEM((2, LANES + DIFF_SUM_ROWS, TD), F32),
                        pltpu.VMEM((2, TD, TD), F32), pltpu.VMEM((2, TD, TD), F32)],
        compiler_params=_cparams(("parallel", "parallel", "arbitrary")),
        name="diff_attention",
    )(proj_t, proj3, proj_t, bias, lam_params, subln_g.reshape(1, LANES))


def _diff_bias(t5_table):
    cols = t5_table.astype(F32)[:, SB_HEADS:]
    cols = (cols - cols[NUM_BUCKETS - 1:NUM_BUCKETS, :]) * LOG2E
    nb = TD // MAX_DISTANCE
    ki = np.arange(MAX_DISTANCE)[:, None]
    qi = np.arange(MAX_DISTANCE)[None, :]
    d0 = qi - ki
    diag = _bias_tile(cols, d0, d0 >= 0)
    prev = _bias_tile(cols, d0 + MAX_DISTANCE, np.ones_like(d0, bool))
    zero = jnp.zeros_like(diag)
    masked = jnp.full_like(diag, NEG)

    def block0(kb, qb):
        return masked if kb > qb else diag if kb == qb else prev if kb == qb - 1 else zero

    t0 = jnp.concatenate([jnp.concatenate([block0(kb, qb) for qb in range(nb)], axis=2) for kb in range(nb)], axis=1)
    t1 = jnp.concatenate([jnp.concatenate([prev if (kb == nb - 1 and qb == 0) else zero for qb in range(nb)], axis=2)
                          for kb in range(nb)], axis=1)
    both = jnp.stack([t0, t1], axis=1)
    return both.reshape(DIFF_HEADS, 2, 2, TD, TD)


def _dil_kernel(q_ref, kp_ref, kc_ref, vp_ref, vc_ref, bias_ref, o_ref, qf, kf, vf, acc_s, m_s, l_s):
    t_idx = pl.program_id(2)
    qf[...] = q_ref[0].astype(F32)
    kf[0:DT, :] = kp_ref[0].astype(F32)
    kf[DT:2 * DT, :] = kc_ref[0].astype(F32)
    vf[0:DT, :] = vp_ref[0].astype(F32)
    vf[DT:2 * DT, :] = vc_ref[0].astype(F32)
    lane = lax.broadcasted_iota(jnp.int32, (1, LANES), 1)
    head0 = lane < HEAD_DIM
    heads = (head0, jnp.logical_not(head0))

    for g, (_, r) in enumerate(DIL_BRANCHES):
        nblk = DT // (r * DQ)
        shift = nblk.bit_length() - 1

        def group(t0, carry, g=g, r=r, nblk=nblk, shift=shift):
            rows, k0s, pre, s, p, mx = {}, {}, {}, {}, {}, {}

            def scores(u):
                t = t0 * DIL_UNROLL + u
                c = lax.shift_right_logical(t, shift)
                n = jnp.bitwise_and(t, nblk - 1)
                q0 = c + n * (DQ * r)
                k0s[u] = DT + q0 - DQ * r
                rows[u] = pl.ds(q0, DQ, stride=r)
                qg = qf[rows[u], :].astype(BF16)
                kg = kf[pl.ds(k0s[u], 2 * DQ, stride=r), :].astype(BF16)
                for hh in range(2):
                    s[(u, hh)] = _dot_nt(jnp.where(heads[hh], qg, jnp.zeros_like(qg)), kg)
                pre[u] = jnp.where(jnp.logical_and(t_idx == 0, n == 0), 1, 0)

            def softmax(u):
                for hh in range(2):
                    sb = s.pop((u, hh)) + bias_ref[0, hh, g, pre[u]]
                    m = jnp.max(sb, axis=1, keepdims=True)
                    p[(u, hh)] = jnp.exp2(sb - m).astype(BF16)
                    mx[(u, hh)] = jnp.broadcast_to(m, (DQ, LANES))

            def values(u):
                vg = vf[pl.ds(k0s[u], 2 * DQ, stride=r), :].astype(BF16)
                res = [_dot(p.pop((u, hh)), jnp.where(heads[hh], vg, jnp.ones_like(vg))) for hh in range(2)]
                acc_s[g, rows[u], :] = jnp.where(head0, res[0], res[1])
                m_s[g, rows[u], :] = jnp.where(head0, mx.pop((u, 0)), mx.pop((u, 1)))
                l_s[g, rows[u], :] = pltpu.roll(jnp.where(head0, res[1], res[0]), HEAD_DIM, 1)

            for step in range(DIL_UNROLL + 2 * DIL_SKEW):
                if step < DIL_UNROLL:
                    scores(step)
                if 0 <= step - DIL_SKEW < DIL_UNROLL:
                    softmax(step - DIL_SKEW)
                if step >= 2 * DIL_SKEW:
                    values(step - 2 * DIL_SKEW)
            return carry

        lax.fori_loop(0, r * nblk // DIL_UNROLL, group, 0)

    def merge(ci, carry):
        rows = pl.ds(pl.multiple_of(ci * TQ, TQ), TQ)
        ms = [m_s[g, rows, :] for g in range(len(DIL_BRANCHES))]
        m = functools.reduce(jnp.maximum, ms)
        num = jnp.zeros((TQ, LANES), F32)
        den = jnp.zeros((TQ, LANES), F32)
        for g in range(len(DIL_BRANCHES)):
            wgt = jnp.exp2(ms[g] - m)
            num = num + wgt * acc_s[g, rows, :]
            den = den + wgt * l_s[g, rows, :]
        o_ref[0, rows, :] = (num / den).astype(o_ref.dtype)
        return carry

    lax.fori_loop(0, DT // TQ, merge, 0)


def _dil_bias(t5_table):
    qi = np.arange(DQ)[:, None]
    ki = np.arange(2 * DQ)[None, :] - DQ
    steps = qi - ki
    valid = (steps >= 0) & (steps <= DIL_WINDOW)
    table = t5_table.astype(F32) * LOG2E
    tiles = [jnp.stack([_bias_tile(table, steps * r, v) for v in (valid, valid & (ki >= 0))], axis=1)
             for _, r in DIL_BRANCHES]
    return jnp.stack(tiles, axis=1).reshape(DIL_HEADS // 2, 2, len(DIL_BRANCHES), 2, DQ, 2 * DQ)


def _dilated_attention(proj3, t5_table):
    b, s, _ = proj3.shape
    pairs = DIL_HEADS // 2
    bias = _dil_bias(t5_table)
    cur = lambda off: (lambda bb, hp, t: (bb, t, off * pairs + hp))
    prev = lambda off: (lambda bb, hp, t: (bb, jnp.maximum(t - 1, 0), off * pairs + hp))
    tile = lambda imap: pl.BlockSpec((1, DT, LANES), imap)
    slab = lambda rows: pltpu.VMEM((rows, LANES), F32)
    per_branch = pltpu.VMEM((len(DIL_BRANCHES), DT, LANES), F32)
    return pl.pallas_call(
        _dil_kernel,
        grid=(b, pairs, s // DT),
        in_specs=[tile(cur(0)), tile(prev(1)), tile(cur(1)), tile(prev(2)), tile(cur(2)),
                  pl.BlockSpec((1, 2, len(DIL_BRANCHES), 2, DQ, 2 * DQ), lambda bb, hp, t: (hp, 0, 0, 0, 0, 0))],
        out_specs=tile(cur(0)),
        out_shape=jax.ShapeDtypeStruct((b, s, D_MODEL), BF16),
        scratch_shapes=[slab(DT), slab(2 * DT), slab(2 * DT), per_branch, per_branch, per_branch],
        compiler_params=_cparams(("parallel", "parallel", "arbitrary")),
        name="dilated_attention",
    )(proj3, proj3, proj3, proj3, proj3, bias)


def _post_kernel(*refs, n_o, final):
    h_ref = refs[0]
    o_refs = refs[1:1 + n_o]
    (wo_ref, p_ref, g_mlp_ref, w_up_ref, w_down_ref, g_ple_ref, w_gate_ref, w_proj_ref, g_fin_ref,
     out_ref, acc_ref) = refs[1 + n_o:]
    o = o_refs[0][...] if n_o == 1 else jnp.concatenate([r[...] for r in o_refs], axis=1)
    h = h_ref[...] + _dot(o, wo_ref[...])
    hn = _rms(h, g_mlp_ref[...], NORM_EPS).astype(BF16)
    acc_ref[...] = h
    for c in range(D_FF // FF_CHUNK):
        u = jnp.maximum(_dot(hn, w_up_ref[:, c * FF_CHUNK:(c + 1) * FF_CHUNK]), 0.0)
        acc_ref[...] += _dot((u * u).astype(BF16), w_down_ref[c * FF_CHUNK:(c + 1) * FF_CHUNK, :])
    h = acc_ref[...]
    gate = jax.nn.sigmoid(_dot(_rms(h, g_ple_ref[...], NORM_EPS).astype(BF16), w_gate_ref[...]))
    h = h + _dot(p_ref[...].astype(BF16), w_proj_ref[...]) * gate
    if final:
        h = _rms(h, g_fin_ref[...], NORM_EPS)
    out_ref[...] = h


def _post(h2d, o_list, w_out, p2d, g_mlp, w_up, w_down, g_ple, w_gate, w_proj, g_fin, final):
    m, d = h2d.shape
    n_o = len(o_list)
    row = lambda width: pl.BlockSpec((TM_POST, width), lambda i: (i, 0))
    const = lambda a: pl.BlockSpec(a.shape, lambda i: (0,) * a.ndim, pipeline_mode=pl.Buffered(1))
    vec = lambda g: g.reshape(1, d).astype(F32)
    weights = [w_out, vec(g_mlp), w_up, w_down, vec(g_ple), w_gate, w_proj, vec(g_fin)]
    return pl.pallas_call(
        functools.partial(_post_kernel, n_o=n_o, final=final),
        grid=(m // TM_POST,),
        in_specs=([row(d)] + [row(o.shape[1]) for o in o_list] + [const(weights[0])]
                  + [row(p2d.shape[1])] + [const(w) for w in weights[1:]]),
        out_specs=row(d),
        out_shape=jax.ShapeDtypeStruct((m, d), F32),
        scratch_shapes=[pltpu.VMEM((TM_POST, d), F32)],
        compiler_params=_cparams(("parallel",)),
        name="out_mlp_ple",
    )(h2d, *o_list, weights[0], p2d, *weights[1:])


def _scale_cols(w_in, col_scale):
    return (w_in * jnp.asarray(col_scale, F32)[None, :]).astype(BF16)


def kernel(x, p, t5_table, w_in_even, w_out_even, lambda_q1, lambda_k1, lambda_q2, lambda_k2, subln_g,
           w_in_odd, w_out_odd, norm_mix_g, norm_mlp_g, w_mlp_up, w_mlp_down, norm_ple_g, w_ple_gate,
           w_ple_proj, final_norm_g):
    b, s, d = x.shape
    depth = p.shape[0]
    assert d == D_MODEL and s % DT == 0 and (b * s) % TM_POST == 0
    assert all(w == DIL_WINDOW * r and w <= DT for w, r in DIL_BRANCHES)
    col = np.arange(3 * D_MODEL)
    qk_scale = HEAD_DIM ** -0.5
    even_q = (col < SB_WIDTH) | ((col >= 3 * SB_WIDTH) & (col < 3 * SB_WIDTH + DIFF_WIDTH))
    even_scale = np.where(even_q, qk_scale * LOG2E, 1.0)
    odd_scale = np.where(col < DIL_HEADS * HEAD_DIM, qk_scale * LOG2E, 1.0)
    h = x.reshape(b * s, d)
    for i in range(depth):
        if i % 2 == 0:
            e = i // 2
            w_in = _scale_cols(w_in_even[e], even_scale)
            qa, ka, va, qb, kb, vb = (w_in[:, lo:lo + SB_WIDTH] for lo in range(0, 6 * SB_WIDTH, SB_WIDTH))
            proj, qa_t, va_t, t_d = _norm_matmul(
                h, norm_mix_g[i], jnp.concatenate([ka, kb], axis=1),
                ((qa.T, TS), (va.T, SB_HALF), (jnp.concatenate([qb, vb], axis=1).T, TD)))
            proj = proj.reshape(b, s, -1)
            qa_t = qa_t.reshape(b, s // TS, SB_WIDTH, TS)
            va_t = va_t.reshape(b, s // SB_HALF, SB_WIDTH, SB_HALF)
            t_d = t_d.reshape(b, s // TD, 2 * DIFF_WIDTH, TD)
            lambda_init = 0.8 - 0.6 * math.exp(-0.3 * i)
            lam_params = jnp.stack([lambda_q1[e], lambda_k1[e], lambda_q2[e], lambda_k2[e]]).astype(F32)
            o_sb = _sb_attention(proj, qa_t, va_t)
            o_d = _diff_attention(proj, t_d, _diff_bias(t5_table), lam_params, subln_g[e].astype(F32),
                                  lambda_init)
            w_out = w_out_even[e].astype(BF16)
            o_list = [o_sb.reshape(b * s, SB_WIDTH), o_d.reshape(b * s, DIFF_WIDTH)]
        else:
            o = i // 2
            (proj,) = _norm_matmul(h, norm_mix_g[i], _scale_cols(w_in_odd[o], odd_scale))
            proj = proj.reshape(b, s, -1)
            o_list = [_dilated_attention(proj, t5_table).reshape(b * s, d)]
            w_out = w_out_odd[o].astype(BF16)
        h = _post(h, o_list, w_out, p[i].reshape(b * s, PLE_DIM), norm_mlp_g[i],
                  w_mlp_up[i].astype(BF16), w_mlp_down[i].astype(BF16), norm_ple_g[i],
                  w_ple_gate[i].astype(BF16), w_ple_proj[i].astype(BF16), final_norm_g,
                  final=(i == depth - 1))
    return h.reshape(b, s, d)
```

```python
import functools
import math

import numpy as np
import jax
import jax.numpy as jnp
from jax import lax
from jax.experimental import pallas as pl
from jax.experimental.pallas import tpu as pltpu

F32 = jnp.float32
BF16 = jnp.bfloat16

D_MODEL = 1024
HEAD_DIM = 64
LANES = 128
SB_HEADS = 8
DIFF_HEADS = 4
DIL_HEADS = 16
SB_WIDTH = SB_HEADS * HEAD_DIM
DIFF_WIDTH = DIFF_HEADS * 2 * HEAD_DIM
DIL_BRANCHES = ((128, 1), (512, 4), (2048, 16))
DIL_WINDOW = 128
NUM_BUCKETS = 32
MAX_DISTANCE = 128
D_FF = 4 * D_MODEL
PLE_DIM = 256
NORM_EPS = 1e-6
SUBLN_EPS = 1e-5
NEG = -1e30
LOG2E = math.log2(math.e)
SB_DEAD_LOG = 88.0

VMEM_LIMIT_BYTES = 56 * 1024 * 1024

TM_PROJ = 512
TM_POST = 512
FF_CHUNK = 1024
TQ = 256
TS = 256
SB_QS = 4
SB_HALF = 128
TD = TM_PROJ
DIFF_SUM_ROWS = 16
DIFF_HPS = 2
DT = 2048
DQ = DIL_WINDOW
DIL_UNROLL = 16
DIL_SKEW = 3


def _cparams(sem):
    return pltpu.CompilerParams(dimension_semantics=sem, vmem_limit_bytes=VMEM_LIMIT_BYTES)


def _rms(x, g, eps):
    return x * lax.rsqrt(jnp.mean(x * x, axis=-1, keepdims=True) + eps) * g


def _dot(a, b):
    return jnp.dot(a, b, preferred_element_type=F32)


def _dot_nt(a, b):
    return lax.dot_general(a, b, (((1,), (1,)), ((), ())), preferred_element_type=F32)


def _norm_matmul_kernel(h_ref, g_ref, w_ref, *rest, slabs):
    hn = _rms(h_ref[...], g_ref[...], NORM_EPS).astype(BF16)
    n_t = len(slabs)
    wt_refs, o_ref, ot_refs = rest[:n_t], rest[n_t], rest[n_t + 1:]
    for wt_ref, ot_ref, width in zip(wt_refs, ot_refs, slabs):
        res = _dot_nt(wt_ref[...], hn).astype(ot_ref.dtype)
        for c in range(TM_PROJ // width):
            ot_ref[c] = res[:, c * width:(c + 1) * width]
    o_ref[...] = _dot(hn, w_ref[...]).astype(o_ref.dtype)


def _norm_matmul(h2d, g, w, wts=()):
    m, k = h2d.shape
    n = w.shape[1]
    const = lambda a: pl.BlockSpec(a.shape, lambda i: (0,) * a.ndim, pipeline_mode=pl.Buffered(1))
    in_specs = [pl.BlockSpec((TM_PROJ, k), lambda i: (i, 0)), pl.BlockSpec((1, k), lambda i: (0, 0)), const(w)]
    in_specs += [const(wt) for wt, _ in wts]
    out_specs = [pl.BlockSpec((TM_PROJ, n), lambda i: (i, 0))]
    out_shape = [jax.ShapeDtypeStruct((m, n), BF16)]
    for wt, width in wts:
        per_step = TM_PROJ // width
        out_specs.append(pl.BlockSpec((per_step, wt.shape[0], width), lambda i: (i, 0, 0)))
        out_shape.append(jax.ShapeDtypeStruct((m // width, wt.shape[0], width), BF16))
    return pl.pallas_call(
        functools.partial(_norm_matmul_kernel, slabs=tuple(width for _, width in wts)),
        grid=(m // TM_PROJ,),
        in_specs=in_specs, out_specs=out_specs, out_shape=out_shape,
        compiler_params=_cparams(("parallel",)),
        name="norm_in_proj",
    )(h2d, g.reshape(1, k), w, *[wt for wt, _ in wts])


def _t5_bucket_np(dist):
    n = np.maximum(dist, 0)
    max_exact = NUM_BUCKETS // 2
    nf = np.maximum(n, 1).astype(np.float32)
    large = max_exact + (np.log(nf / np.float32(max_exact)) / np.float32(math.log(MAX_DISTANCE / max_exact))
                         * np.float32(NUM_BUCKETS - max_exact)).astype(np.int32)
    large = np.minimum(large, NUM_BUCKETS - 1)
    return np.where(n < max_exact, n, large).astype(np.int32)


def _bias_tile(table_cols, dist, valid):
    bucket = _t5_bucket_np(dist)
    out = jnp.zeros((table_cols.shape[1],) + dist.shape, F32)
    for bkt in np.unique(bucket[valid]):
        out = jnp.where(jnp.asarray(bucket == bkt)[None], table_cols[bkt][:, None, None], out)
    return jnp.where(jnp.asarray(valid)[None], out, NEG)


def _sb_kernel(qt_ref, k_ref, vt_ref, o_ref, acc_ref, c_ref):
    i = pl.program_id(2)
    row = lax.broadcasted_iota(jnp.int32, (LANES, 1), 0)
    head_rows = (row < HEAD_DIM, row >= HEAD_DIM)
    qts = [tuple(jnp.where(hr, qt_ref[0, a], jnp.zeros((LANES, TS), BF16)) for hr in head_rows)
           for a in range(SB_QS)]
    kk = lax.broadcasted_iota(jnp.int32, (TS, TS), 0)
    qq = lax.broadcasted_iota(jnp.int32, (TS, TS), 1)
    strict = kk < qq
    per_half = TS // SB_HALF

    def tri2(n):
        tri = (lax.broadcasted_iota(jnp.int32, (n, n), 0) <= lax.broadcasted_iota(jnp.int32, (n, n), 1)).astype(BF16)
        return jnp.concatenate([tri, tri], axis=1)

    tri2s = {n: tri2(n) for n in (TS, SB_HALF)}

    def tiles(work, c_in):
        items = [(n, hh) for n in range(len(work)) for hh in range(2)]
        ks = [k_ref[0, pl.ds(pl.multiple_of(k0, SB_HALF), nk), :] for _, k0, nk, _, _ in work]
        z = {(n, hh): _dot(ks[n], qts[work[n][0]][hh]) for n, hh in items}
        parts = {}
        for it in items:
            keep = work[it[0]][4]
            if keep is not None:
                z[it] = jnp.where(keep, z[it], NEG)
            lk = -(jnp.maximum(z[it], 0.0) + jnp.log2(1.0 + jnp.exp2(-jnp.abs(z[it]))))
            hi = lk.astype(BF16)
            parts[it] = jnp.concatenate([hi, (lk - hi.astype(F32)).astype(BF16)], axis=0)
        incl = {it: _dot(tri2s[work[it[0]][2]], parts[it]) for it in items}
        c = dict(c_in)
        w = {}
        for n, hh in items:
            a = work[n][0]
            w[(n, hh)] = jnp.exp2(z[(n, hh)] + incl[(n, hh)] + c[(a, hh)]).astype(BF16)
            c[(a, hh)] = c[(a, hh)] + incl[(n, hh)][0:1, :]
        per_head = {}
        for n, hh in items:
            a = work[n][0]
            part = _dot(work[n][3](), w[(n, hh)])
            per_head[(a, hh)] = part if (a, hh) not in per_head else per_head[(a, hh)] + part
        upd = {a: jnp.where(head_rows[0], per_head[(a, 0)], per_head[(a, 1)]) for a, _ in per_head}
        return upd, c

    def half_tile(a, e, keep=None):
        return (a, e * SB_HALF, SB_HALF, lambda: vt_ref[0, e], keep)

    def diag_vt(g):
        return jnp.concatenate([vt_ref[0, g * per_half + c] for c in range(per_half)], axis=1)

    zero_c = jnp.zeros((1, TS), F32)
    work = []
    for a in range(SB_QS):
        g = i * SB_QS + a
        work.append((a, g * TS, TS, functools.partial(diag_vt, g), strict))
        if a > 0:
            work.append(half_tile(a, g * per_half - 1))
        else:
            exists = lax.broadcasted_iota(jnp.int32, (SB_HALF, TS), 0) < jnp.where(g >= 1, SB_HALF, 0)
            work.append(half_tile(a, jnp.maximum(g * per_half - 1, 0), exists))
    upd, c1 = tiles(work, {(a, hh): zero_c for a in range(SB_QS) for hh in range(2)})
    for a in range(SB_QS):
        acc_ref[a] = upd[a]
        for hh in range(2):
            c_ref[a, hh] = c1[(a, hh)]

    for a in range(SB_QS):
        def cond(state):
            e, alive = state
            return jnp.logical_and(e >= 0, alive > -SB_DEAD_LOG * LOG2E)

        def body(state, a=a):
            e, _ = state
            upd, c_new = tiles([half_tile(a, e)], {(a, hh): c_ref[a, hh] for hh in range(2)})
            acc_ref[a] += upd[a]
            for hh in range(2):
                c_ref[a, hh] = c_new[(a, hh)]
            return e - 1, jnp.maximum(jnp.max(c_new[(a, 0)]), jnp.max(c_new[(a, 1)]))

        alive1 = jnp.maximum(jnp.max(c1[(a, 0)]), jnp.max(c1[(a, 1)]))
        lax.while_loop(cond, body, ((i * SB_QS + a) * per_half - 2, alive1))
        o_ref[0, a * TS:(a + 1) * TS, :] = jnp.transpose(acc_ref[a]).astype(o_ref.dtype)


def _sb_attention(proj3, q_t, v_t):
    b, s, _ = proj3.shape
    pairs = SB_WIDTH // LANES
    return pl.pallas_call(
        _sb_kernel,
        grid=(b, pairs, s // (SB_QS * TS)),
        in_specs=[pl.BlockSpec((1, SB_QS, LANES, TS), lambda bb, hp, i: (bb, i, hp, 0)),
                  pl.BlockSpec((1, s, LANES), lambda bb, hp, i: (bb, 0, hp)),
                  pl.BlockSpec((1, s // SB_HALF, LANES, SB_HALF), lambda bb, hp, i: (bb, 0, hp, 0))],
        out_specs=pl.BlockSpec((1, SB_QS * TS, LANES), lambda bb, hp, i: (bb, i, hp)),
        out_shape=jax.ShapeDtypeStruct((b, s, SB_WIDTH), BF16),
        scratch_shapes=[pltpu.VMEM((SB_QS, LANES, TS), F32), pltpu.VMEM((SB_QS, 2, 1, TS), F32)],
        compiler_params=_cparams(("parallel", "parallel", "arbitrary")),
        name="sb_attention",
    )(q_t, proj3, v_t)


def _diff_kernel(qt_ref, k_ref, vt_ref, bias_ref, lam_ref, g_ref, o_ref, m_ref, acc_ref, sa_ref, sb_ref,
                 *, lambda_init):
    i = pl.program_id(2)
    chains = [(hd, mm) for hd in range(DIFF_HPS) for mm in range(2)]
    row = lax.broadcasted_iota(jnp.int32, (LANES, 1), 0)
    map_rows = (row < HEAD_DIM, row >= HEAD_DIM)
    qts = {}
    for hd in range(DIFF_HPS):
        qt = qt_ref[0, 0, hd * LANES:(hd + 1) * LANES, :]
        for mm in range(2):
            qts[(hd, mm)] = jnp.where(map_rows[mm], qt, jnp.zeros_like(qt))

    m_ref[...] = jnp.full_like(m_ref, NEG)
    acc_ref[...] = jnp.zeros_like(acc_ref)

    def stage(s_ref, j):
        ks = k_ref[0, pl.ds(pl.multiple_of(j * TD, TD), TD), :]
        for hd, mm in chains:
            s_ref[hd, mm] = _dot(ks[:, hd * LANES:(hd + 1) * LANES], qts[(hd, mm)])

    def absorb(j, s_ref, bias_idx=None, offset=None, nxt=None):
        s, alpha = {}, {}
        for hd, mm in chains:
            sm = s_ref[hd, mm]
            if bias_idx is not None:
                sm = sm + bias_ref[hd, mm, bias_idx]
            if offset is not None:
                sm = sm + offset
            m_old = m_ref[hd, mm]
            m_new = jnp.maximum(m_old, jnp.max(sm, axis=0, keepdims=True))
            m_ref[hd, mm] = m_new
            alpha[(hd, mm)] = jnp.exp2(m_old - m_new)
            s[(hd, mm)] = sm - m_new
        if nxt is not None:
            stage(*nxt)
        p = {ch: jnp.exp2(s[ch]).astype(BF16) for ch in chains}
        vt = vt_ref[0, j]
        ones = jnp.ones((DIFF_SUM_ROWS, TD), BF16)
        for hd in range(DIFF_HPS):
            vts = jnp.concatenate([vt[hd * LANES:(hd + 1) * LANES], ones], axis=0)
            for mm in range(2):
                acc_ref[hd, mm] = alpha[(hd, mm)] * acc_ref[hd, mm] + _dot(vts, p[(hd, mm)])

    stage(sa_ref, i)
    absorb(i, sa_ref, bias_idx=0, nxt=(sb_ref, jnp.maximum(i - 1, 0)))
    absorb(jnp.maximum(i - 1, 0), sb_ref, bias_idx=1, offset=jnp.where(i >= 1, 0.0, NEG),
           nxt=(sa_ref, jnp.maximum(i - 2, 0)))

    @pl.when(i >= 2)
    def _():
        def body(u, carry):
            ja = i - 2 - 2 * u
            absorb(ja, sa_ref, nxt=(sb_ref, ja - 1))
            absorb(ja - 1, sb_ref, nxt=(sa_ref, jnp.maximum(ja - 2, 0)))
            return carry

        lax.fori_loop(0, (i - 1) // 2, body, 0)

        @pl.when((i - 1) % 2 == 1)
        def _():
            absorb(0, sa_ref)

    lam = (jnp.exp(jnp.sum(lam_ref[0:1, :] * lam_ref[1:2, :], axis=1, keepdims=True))
           - jnp.exp(jnp.sum(lam_ref[2:3, :] * lam_ref[3:4, :], axis=1, keepdims=True)) + lambda_init)
    for hd in range(DIFF_HPS):
        l0, l1 = acc_ref[hd, 0, LANES:LANES + 1, :], acc_ref[hd, 1, LANES:LANES + 1, :]
        ot = acc_ref[hd, 0, 0:LANES, :] / l0 - lam * (acc_ref[hd, 1, 0:LANES, :] / l1)
        o = jnp.transpose(ot)
        o_ref[0, :, hd * LANES:(hd + 1) * LANES] = (
            _rms(o, g_ref[...], SUBLN_EPS) * (1.0 - lambda_init)).astype(o_ref.dtype)


def _diff_attention(proj3, proj_t, bias, lam_params, subln_g, lambda_init):
    b, s, _ = proj3.shape
    width = DIFF_HPS * LANES
    groups = DIFF_WIDTH // width
    nt = s // TD
    return pl.pallas_call(
        functools.partial(_diff_kernel, lambda_init=lambda_init),
        grid=(b, groups, nt),
        in_specs=[pl.BlockSpec((1, 1, width, TD), lambda bb, h, i: (bb, i, h, 0)),
                  pl.BlockSpec((1, s, width), lambda bb, h, i: (bb, 0, groups + h)),
                  pl.BlockSpec((1, nt, width, TD), lambda bb, h, i: (bb, 0, groups + h, 0)),
                  pl.BlockSpec((DIFF_HPS, 2, 2, TD, TD), lambda bb, h, i: (h, 0, 0, 0, 0),
                               pipeline_mode=pl.Buffered(1)),
                  pl.BlockSpec((4, HEAD_DIM), lambda bb, h, i: (0, 0)),
                  pl.BlockSpec((1, LANES), lambda bb, h, i: (0, 0))],
        out_specs=pl.BlockSpec((1, TD, width), lambda bb, h, i: (bb, i, h)),
        out_shape=jax.ShapeDtypeStruct((b, s, DIFF_WIDTH), BF16),
        scratch_shapes=[pltpu.VMEM((DIFF_HPS, 2, 1, TD), F32),
                        pltpu.VMEM((DIFF_HPS, 2, LANES + DIFF_SUM_ROWS, TD), F32),
                        pltpu.VMEM((DIFF_HPS, 2, TD, TD), F32), pltpu.VMEM((DIFF_HPS, 2, TD, TD), F32)],
        compiler_params=_cparams(("parallel", "parallel", "arbitrary")),
        name="diff_attention",
    )(proj_t, proj3, proj_t, bias, lam_params, subln_g.reshape(1, LANES))


def _diff_bias(t5_table):
    cols = t5_table.astype(F32)[:, SB_HEADS:]
    cols = (cols - cols[NUM_BUCKETS - 1:NUM_BUCKETS, :]) * LOG2E
    nb = TD // MAX_DISTANCE
    ki = np.arange(MAX_DISTANCE)[:, None]
    qi = np.arange(MAX_DISTANCE)[None, :]
    d0 = qi - ki
    diag = _bias_tile(cols, d0, d0 >= 0)
    prev = _bias_tile(cols, d0 + MAX_DISTANCE, np.ones_like(d0, bool))
    zero = jnp.zeros_like(diag)
    masked = jnp.full_like(diag, NEG)

    def block0(kb, qb):
        return masked if kb > qb else diag if kb == qb else prev if kb == qb - 1 else zero

    t0 = jnp.concatenate([jnp.concatenate([block0(kb, qb) for qb in range(nb)], axis=2) for kb in range(nb)], axis=1)
    t1 = jnp.concatenate([jnp.concatenate([prev if (kb == nb - 1 and qb == 0) else zero for qb in range(nb)], axis=2)
                          for kb in range(nb)], axis=1)
    both = jnp.stack([t0, t1], axis=1)
    return both.reshape(DIFF_HEADS, 2, 2, TD, TD)


def _dil_kernel(q_ref, kp_ref, kc_ref, vp_ref, vc_ref, bias_ref, o_ref, qf, kf, vf, acc_s, m_s, l_s):
    t_idx = pl.program_id(2)
    qf[...] = q_ref[0].astype(F32)
    kf[0:DT, :] = kp_ref[0].astype(F32)
    kf[DT:2 * DT, :] = kc_ref[0].astype(F32)
    vf[0:DT, :] = vp_ref[0].astype(F32)
    vf[DT:2 * DT, :] = vc_ref[0].astype(F32)
    lane = lax.broadcasted_iota(jnp.int32, (1, LANES), 1)
    head0 = lane < HEAD_DIM
    heads = (head0, jnp.logical_not(head0))

    for g, (_, r) in enumerate(DIL_BRANCHES):
        nblk = DT // (r * DQ)
        shift = nblk.bit_length() - 1

        def group(t0, carry, g=g, r=r, nblk=nblk, shift=shift):
            rows, k0s, pre, s, p, mx = {}, {}, {}, {}, {}, {}

            def scores(u):
                t = t0 * DIL_UNROLL + u
                c = lax.shift_right_logical(t, shift)
                n = jnp.bitwise_and(t, nblk - 1)
                q0 = c + n * (DQ * r)
                k0s[u] = DT + q0 - DQ * r
                rows[u] = pl.ds(q0, DQ, stride=r)
                qg = qf[rows[u], :].astype(BF16)
                kg = kf[pl.ds(k0s[u], 2 * DQ, stride=r), :].astype(BF16)
                for hh in range(2):
                    s[(u, hh)] = _dot_nt(jnp.where(heads[hh], qg, jnp.zeros_like(qg)), kg)
                pre[u] = jnp.where(jnp.logical_and(t_idx == 0, n == 0), 1, 0)

            def softmax(u):
                for hh in range(2):
                    sb = s.pop((u, hh)) + bias_ref[0, hh, g, pre[u]]
                    m = jnp.max(sb, axis=1, keepdims=True)
                    p[(u, hh)] = jnp.exp2(sb - m).astype(BF16)
                    mx[(u, hh)] = jnp.broadcast_to(m, (DQ, LANES))

            def values(u):
                vg = vf[pl.ds(k0s[u], 2 * DQ, stride=r), :].astype(BF16)
                res = [_dot(p.pop((u, hh)), jnp.where(heads[hh], vg, jnp.ones_like(vg))) for hh in range(2)]
                acc_s[g, rows[u], :] = jnp.where(head0, res[0], res[1])
                m_s[g, rows[u], :] = jnp.where(head0, mx.pop((u, 0)), mx.pop((u, 1)))
                l_s[g, rows[u], :] = pltpu.roll(jnp.where(head0, res[1], res[0]), HEAD_DIM, 1)

            for step in range(DIL_UNROLL + 2 * DIL_SKEW):
                if step < DIL_UNROLL:
                    scores(step)
                if 0 <= step - DIL_SKEW < DIL_UNROLL:
                    softmax(step - DIL_SKEW)
                if step >= 2 * DIL_SKEW:
                    values(step - 2 * DIL_SKEW)
            return carry

        lax.fori_loop(0, r * nblk // DIL_UNROLL, group, 0)

    def merge(ci, carry):
        rows = pl.ds(pl.multiple_of(ci * TQ, TQ), TQ)
        ms = [m_s[g, rows, :] for g in range(len(DIL_BRANCHES))]
        m = functools.reduce(jnp.maximum, ms)
        num = jnp.zeros((TQ, LANES), F32)
        den = jnp.zeros((TQ, LANES), F32)
        for g in range(len(DIL_BRANCHES)):
            wgt = jnp.exp2(ms[g] - m)
            num = num + wgt * acc_s[g, rows, :]
            den = den + wgt * l_s[g, rows, :]
        o_ref[0, rows, :] = (num / den).astype(o_ref.dtype)
        return carry

    lax.fori_loop(0, DT // TQ, merge, 0)


def _dil_bias(t5_table):
    qi = np.arange(DQ)[:, None]
    ki = np.arange(2 * DQ)[None, :] - DQ
    steps = qi - ki
    valid = (steps >= 0) & (steps <= DIL_WINDOW)
    table = t5_table.astype(F32) * LOG2E
    tiles = [jnp.stack([_bias_tile(table, steps * r, v) for v in (valid, valid & (ki >= 0))], axis=1)
             for _, r in DIL_BRANCHES]
    return jnp.stack(tiles, axis=1).reshape(DIL_HEADS // 2, 2, len(DIL_BRANCHES), 2, DQ, 2 * DQ)


def _dilated_attention(proj3, t5_table):
    b, s, _ = proj3.shape
    pairs = DIL_HEADS // 2
    bias = _dil_bias(t5_table)
    cur = lambda off: (lambda bb, hp, t: (bb, t, off * pairs + hp))
    prev = lambda off: (lambda bb, hp, t: (bb, jnp.maximum(t - 1, 0), off * pairs + hp))
    tile = lambda imap: pl.BlockSpec((1, DT, LANES), imap)
    slab = lambda rows: pltpu.VMEM((rows, LANES), F32)
    per_branch = pltpu.VMEM((len(DIL_BRANCHES), DT, LANES), F32)
    return pl.pallas_call(
        _dil_kernel,
        grid=(b, pairs, s // DT),
        in_specs=[tile(cur(0)), tile(prev(1)), tile(cur(1)), tile(prev(2)), tile(cur(2)),
                  pl.BlockSpec((1, 2, len(DIL_BRANCHES), 2, DQ, 2 * DQ), lambda bb, hp, t: (hp, 0, 0, 0, 0, 0))],
        out_specs=tile(cur(0)),
        out_shape=jax.ShapeDtypeStruct((b, s, D_MODEL), BF16),
        scratch_shapes=[slab(DT), slab(2 * DT), slab(2 * DT), per_branch, per_branch, per_branch],
        compiler_params=_cparams(("parallel", "parallel", "arbitrary")),
        name="dilated_attention",
    )(proj3, proj3, proj3, proj3, proj3, bias)


def _post_kernel(*refs, n_o, final):
    h_ref = refs[0]
    o_refs = refs[1:1 + n_o]
    (wo_ref, p_ref, g_mlp_ref, w_up_ref, w_down_ref, g_ple_ref, w_gate_ref, w_proj_ref, g_fin_ref,
     out_ref, acc_ref) = refs[1 + n_o:]
    o = o_refs[0][...] if n_o == 1 else jnp.concatenate([r[...] for r in o_refs], axis=1)
    h = h_ref[...] + _dot(o, wo_ref[...])
    hn = _rms(h, g_mlp_ref[...], NORM_EPS).astype(BF16)
    acc_ref[...] = h
    for c in range(D_FF // FF_CHUNK):
        u = jnp.maximum(_dot(hn, w_up_ref[:, c * FF_CHUNK:(c + 1) * FF_CHUNK]), 0.0)
        acc_ref[...] += _dot((u * u).astype(BF16), w_down_ref[c * FF_CHUNK:(c + 1) * FF_CHUNK, :])
    h = acc_ref[...]
    gate = jax.nn.sigmoid(_dot(_rms(h, g_ple_ref[...], NORM_EPS).astype(BF16), w_gate_ref[...]))
    h = h + _dot(p_ref[...].astype(BF16), w_proj_ref[...]) * gate
    if final:
        h = _rms(h, g_fin_ref[...], NORM_EPS)
    out_ref[...] = h


def _post(h2d, o_list, w_out, p2d, g_mlp, w_up, w_down, g_ple, w_gate, w_proj, g_fin, final):
    m, d = h2d.shape
    n_o = len(o_list)
    row = lambda width: pl.BlockSpec((TM_POST, width), lambda i: (i, 0))
    const = lambda a: pl.BlockSpec(a.shape, lambda i: (0,) * a.ndim, pipeline_mode=pl.Buffered(1))
    vec = lambda g: g.reshape(1, d).astype(F32)
    weights = [w_out, vec(g_mlp), w_up, w_down, vec(g_ple), w_gate, w_proj, vec(g_fin)]
    return pl.pallas_call(
        functools.partial(_post_kernel, n_o=n_o, final=final),
        grid=(m // TM_POST,),
        in_specs=([row(d)] + [row(o.shape[1]) for o in o_list] + [const(weights[0])]
                  + [row(p2d.shape[1])] + [const(w) for w in weights[1:]]),
        out_specs=row(d),
        out_shape=jax.ShapeDtypeStruct((m, d), F32),
        scratch_shapes=[pltpu.VMEM((TM_POST, d), F32)],
        compiler_params=_cparams(("parallel",)),
        name="out_mlp_ple",
    )(h2d, *o_list, weights[0], p2d, *weights[1:])


def _scale_cols(w_in, col_scale):
    return (w_in * jnp.asarray(col_scale, F32)[None, :]).astype(BF16)


def kernel(x, p, t5_table, w_in_even, w_out_even, lambda_q1, lambda_k1, lambda_q2, lambda_k2, subln_g,
           w_in_odd, w_out_odd, norm_mix_g, norm_mlp_g, w_mlp_up, w_mlp_down, norm_ple_g, w_ple_gate,
           w_ple_proj, final_norm_g):
    b, s, d = x.shape
    depth = p.shape[0]
    assert d == D_MODEL and s % DT == 0 and (b * s) % TM_POST == 0
    assert all(w == DIL_WINDOW * r and w <= DT for w, r in DIL_BRANCHES)
    col = np.arange(3 * D_MODEL)
    qk_scale = HEAD_DIM ** -0.5
    even_q = (col < SB_WIDTH) | ((col >= 3 * SB_WIDTH) & (col < 3 * SB_WIDTH + DIFF_WIDTH))
    even_scale = np.where(even_q, qk_scale * LOG2E, 1.0)
    odd_scale = np.where(col < DIL_HEADS * HEAD_DIM, qk_scale * LOG2E, 1.0)
    h = x.reshape(b * s, d)
    for i in range(depth):
        if i % 2 == 0:
            e = i // 2
            w_in = _scale_cols(w_in_even[e], even_scale)
            qa, ka, va, qb, kb, vb = (w_in[:, lo:lo + SB_WIDTH] for lo in range(0, 6 * SB_WIDTH, SB_WIDTH))
            proj, qa_t, va_t, t_d = _norm_matmul(
                h, norm_mix_g[i], jnp.concatenate([ka, kb], axis=1),
                ((qa.T, TS), (va.T, SB_HALF), (jnp.concatenate([qb, vb], axis=1).T, TD)))
            proj = proj.reshape(b, s, -1)
            qa_t = qa_t.reshape(b, s // TS, SB_WIDTH, TS)
            va_t = va_t.reshape(b, s // SB_HALF, SB_WIDTH, SB_HALF)
            t_d = t_d.reshape(b, s // TD, 2 * DIFF_WIDTH, TD)
            lambda_init = 0.8 - 0.6 * math.exp(-0.3 * i)
            lam_params = jnp.stack([lambda_q1[e], lambda_k1[e], lambda_q2[e], lambda_k2[e]]).astype(F32)
            o_sb = _sb_attention(proj, qa_t, va_t)
            o_d = _diff_attention(proj, t_d, _diff_bias(t5_table), lam_params, subln_g[e].astype(F32),
                                  lambda_init)
            w_out = w_out_even[e].astype(BF16)
            o_list = [o_sb.reshape(b * s, SB_WIDTH), o_d.reshape(b * s, DIFF_WIDTH)]
        else:
            o = i // 2
            (proj,) = _norm_matmul(h, norm_mix_g[i], _scale_cols(w_in_odd[o], odd_scale))
            proj = proj.reshape(b, s, -1)
            o_list = [_dilated_attention(proj, t5_table).reshape(b * s, d)]
            w_out = w_out_odd[o].astype(BF16)
        h = _post(h, o_list, w_out, p[i].reshape(b * s, PLE_DIM), norm_mlp_g[i],
                  w_mlp_up[i].astype(BF16), w_mlp_down[i].astype(BF16), norm_ple_g[i],
                  w_ple_gate[i].astype(BF16), w_ple_proj[i].astype(BF16), final_norm_g,
                  final=(i == depth - 1))
    return h.reshape(b, s, d)
```

```python
import functools
import math

import numpy as np
import jax
import jax.numpy as jnp
from jax import lax
from jax.experimental import pallas as pl
from jax.experimental.pallas import tpu as pltpu

F32 = jnp.float32
BF16 = jnp.bfloat16

D_MODEL = 1024
HEAD_DIM = 64
LANES = 128
SB_HEADS = 8
DIFF_HEADS = 4
DIL_HEADS = 16
SB_WIDTH = SB_HEADS * HEAD_DIM
DIFF_WIDTH = DIFF_HEADS * 2 * HEAD_DIM
DIL_BRANCHES = ((128, 1), (512, 4), (2048, 16))
DIL_WINDOW = 128
NUM_BUCKETS = 32
MAX_DISTANCE = 128
D_FF = 4 * D_MODEL
PLE_DIM = 256
NORM_EPS = 1e-6
SUBLN_EPS = 1e-5
NEG = -1e30
LOG2E = math.log2(math.e)
SB_DEAD_LOG = 88.0

VMEM_LIMIT_BYTES = 58 * 1024 * 1024

TM_PROJ = 512
TM_POST = 512
FF_CHUNK = 1024
TQ = 256
TS = 256
SB_QS = 4
SB_HALF = 128
TD = TM_PROJ
DIFF_SUM_ROWS = 16
DIFF_HPS = 4
DT = 2048
DQ = DIL_WINDOW
DIL_UNROLL = 16
DIL_SKEW = 3


def _cparams(sem):
    return pltpu.CompilerParams(dimension_semantics=sem, vmem_limit_bytes=VMEM_LIMIT_BYTES)


def _rms(x, g, eps):
    return x * lax.rsqrt(jnp.mean(x * x, axis=-1, keepdims=True) + eps) * g


def _dot(a, b):
    return jnp.dot(a, b, preferred_element_type=F32)


def _dot_nt(a, b):
    return lax.dot_general(a, b, (((1,), (1,)), ((), ())), preferred_element_type=F32)


def _norm_matmul_kernel(h_ref, g_ref, w_ref, *rest, slabs):
    hn = _rms(h_ref[...], g_ref[...], NORM_EPS).astype(BF16)
    n_t = len(slabs)
    wt_refs, o_ref, ot_refs = rest[:n_t], rest[n_t], rest[n_t + 1:]
    for wt_ref, ot_ref, width in zip(wt_refs, ot_refs, slabs):
        res = _dot_nt(wt_ref[...], hn).astype(ot_ref.dtype)
        for c in range(TM_PROJ // width):
            ot_ref[c] = res[:, c * width:(c + 1) * width]
    o_ref[...] = _dot(hn, w_ref[...]).astype(o_ref.dtype)


def _norm_matmul(h2d, g, w, wts=()):
    m, k = h2d.shape
    n = w.shape[1]
    const = lambda a: pl.BlockSpec(a.shape, lambda i: (0,) * a.ndim, pipeline_mode=pl.Buffered(1))
    in_specs = [pl.BlockSpec((TM_PROJ, k), lambda i: (i, 0)), pl.BlockSpec((1, k), lambda i: (0, 0)), const(w)]
    in_specs += [const(wt) for wt, _ in wts]
    out_specs = [pl.BlockSpec((TM_PROJ, n), lambda i: (i, 0))]
    out_shape = [jax.ShapeDtypeStruct((m, n), BF16)]
    for wt, width in wts:
        per_step = TM_PROJ // width
        out_specs.append(pl.BlockSpec((per_step, wt.shape[0], width), lambda i: (i, 0, 0)))
        out_shape.append(jax.ShapeDtypeStruct((m // width, wt.shape[0], width), BF16))
    return pl.pallas_call(
        functools.partial(_norm_matmul_kernel, slabs=tuple(width for _, width in wts)),
        grid=(m // TM_PROJ,),
        in_specs=in_specs, out_specs=out_specs, out_shape=out_shape,
        compiler_params=_cparams(("parallel",)),
        name="norm_in_proj",
    )(h2d, g.reshape(1, k), w, *[wt for wt, _ in wts])


def _t5_bucket_np(dist):
    n = np.maximum(dist, 0)
    max_exact = NUM_BUCKETS // 2
    nf = np.maximum(n, 1).astype(np.float32)
    large = max_exact + (np.log(nf / np.float32(max_exact)) / np.float32(math.log(MAX_DISTANCE / max_exact))
                         * np.float32(NUM_BUCKETS - max_exact)).astype(np.int32)
    large = np.minimum(large, NUM_BUCKETS - 1)
    return np.where(n < max_exact, n, large).astype(np.int32)


def _bias_tile(table_cols, dist, valid):
    bucket = _t5_bucket_np(dist)
    out = jnp.zeros((table_cols.shape[1],) + dist.shape, F32)
    for bkt in np.unique(bucket[valid]):
        out = jnp.where(jnp.asarray(bucket == bkt)[None], table_cols[bkt][:, None, None], out)
    return jnp.where(jnp.asarray(valid)[None], out, NEG)


def _sb_kernel(qt_ref, k_ref, vt_ref, o_ref, acc_ref, c_ref):
    i = pl.program_id(2)
    row = lax.broadcasted_iota(jnp.int32, (LANES, 1), 0)
    head_rows = (row < HEAD_DIM, row >= HEAD_DIM)
    qts = [tuple(jnp.where(hr, qt_ref[0, a], jnp.zeros((LANES, TS), BF16)) for hr in head_rows)
           for a in range(SB_QS)]
    kk = lax.broadcasted_iota(jnp.int32, (TS, TS), 0)
    qq = lax.broadcasted_iota(jnp.int32, (TS, TS), 1)
    strict = kk < qq
    per_half = TS // SB_HALF

    def tri2(n):
        tri = (lax.broadcasted_iota(jnp.int32, (n, n), 0) <= lax.broadcasted_iota(jnp.int32, (n, n), 1)).astype(BF16)
        return jnp.concatenate([tri, tri], axis=1)

    tri2s = {n: tri2(n) for n in (TS, SB_HALF)}

    def tiles(work, c_in):
        items = [(n, hh) for n in range(len(work)) for hh in range(2)]
        ks = [k_ref[0, pl.ds(pl.multiple_of(k0, SB_HALF), nk), :] for _, k0, nk, _, _ in work]
        z = {(n, hh): _dot(ks[n], qts[work[n][0]][hh]) for n, hh in items}
        parts = {}
        for it in items:
            keep = work[it[0]][4]
            if keep is not None:
                z[it] = jnp.where(keep, z[it], NEG)
            lk = -(jnp.maximum(z[it], 0.0) + jnp.log2(1.0 + jnp.exp2(-jnp.abs(z[it]))))
            hi = lk.astype(BF16)
            parts[it] = jnp.concatenate([hi, (lk - hi.astype(F32)).astype(BF16)], axis=0)
        incl = {it: _dot(tri2s[work[it[0]][2]], parts[it]) for it in items}
        c = dict(c_in)
        w = {}
        for n, hh in items:
            a = work[n][0]
            w[(n, hh)] = jnp.exp2(z[(n, hh)] + incl[(n, hh)] + c[(a, hh)]).astype(BF16)
            c[(a, hh)] = c[(a, hh)] + incl[(n, hh)][0:1, :]
        per_head = {}
        for n, hh in items:
            a = work[n][0]
            part = _dot(work[n][3](), w[(n, hh)])
            per_head[(a, hh)] = part if (a, hh) not in per_head else per_head[(a, hh)] + part
        upd = {a: jnp.where(head_rows[0], per_head[(a, 0)], per_head[(a, 1)]) for a, _ in per_head}
        return upd, c

    def half_tile(a, e, keep=None):
        return (a, e * SB_HALF, SB_HALF, lambda: vt_ref[0, e], keep)

    def diag_vt(g):
        return jnp.concatenate([vt_ref[0, g * per_half + c] for c in range(per_half)], axis=1)

    zero_c = jnp.zeros((1, TS), F32)
    work = []
    for a in range(SB_QS):
        g = i * SB_QS + a
        work.append((a, g * TS, TS, functools.partial(diag_vt, g), strict))
        if a > 0:
            work.append(half_tile(a, g * per_half - 1))
        else:
            exists = lax.broadcasted_iota(jnp.int32, (SB_HALF, TS), 0) < jnp.where(g >= 1, SB_HALF, 0)
            work.append(half_tile(a, jnp.maximum(g * per_half - 1, 0), exists))
    upd, c1 = tiles(work, {(a, hh): zero_c for a in range(SB_QS) for hh in range(2)})
    for a in range(SB_QS):
        acc_ref[a] = upd[a]
        for hh in range(2):
            c_ref[a, hh] = c1[(a, hh)]

    for a in range(SB_QS):
        def cond(state):
            e, alive = state
            return jnp.logical_and(e >= 0, alive > -SB_DEAD_LOG * LOG2E)

        def body(state, a=a):
            e, _ = state
            upd, c_new = tiles([half_tile(a, e)], {(a, hh): c_ref[a, hh] for hh in range(2)})
            acc_ref[a] += upd[a]
            for hh in range(2):
                c_ref[a, hh] = c_new[(a, hh)]
            return e - 1, jnp.maximum(jnp.max(c_new[(a, 0)]), jnp.max(c_new[(a, 1)]))

        alive1 = jnp.maximum(jnp.max(c1[(a, 0)]), jnp.max(c1[(a, 1)]))
        lax.while_loop(cond, body, ((i * SB_QS + a) * per_half - 2, alive1))
        o_ref[0, a * TS:(a + 1) * TS, :] = jnp.transpose(acc_ref[a]).astype(o_ref.dtype)


def _sb_attention(proj3, q_t, v_t):
    b, s, _ = proj3.shape
    pairs = SB_WIDTH // LANES
    return pl.pallas_call(
        _sb_kernel,
        grid=(b, pairs, s // (SB_QS * TS)),
        in_specs=[pl.BlockSpec((1, SB_QS, LANES, TS), lambda bb, hp, i: (bb, i, hp, 0)),
                  pl.BlockSpec((1, s, LANES), lambda bb, hp, i: (bb, 0, hp)),
                  pl.BlockSpec((1, s // SB_HALF, LANES, SB_HALF), lambda bb, hp, i: (bb, 0, hp, 0))],
        out_specs=pl.BlockSpec((1, SB_QS * TS, LANES), lambda bb, hp, i: (bb, i, hp)),
        out_shape=jax.ShapeDtypeStruct((b, s, SB_WIDTH), BF16),
        scratch_shapes=[pltpu.VMEM((SB_QS, LANES, TS), F32), pltpu.VMEM((SB_QS, 2, 1, TS), F32)],
        compiler_params=_cparams(("parallel", "parallel", "arbitrary")),
        name="sb_attention",
    )(q_t, proj3, v_t)


def _diff_kernel(qt_ref, k_ref, vt_ref, bias_ref, lam_ref, g_ref, o_ref, m_ref, acc_ref, sa_ref, sb_ref,
                 *, lambda_init):
    i = pl.program_id(2)
    chains = [(hd, mm) for hd in range(DIFF_HPS) for mm in range(2)]
    row = lax.broadcasted_iota(jnp.int32, (LANES, 1), 0)
    map_rows = (row < HEAD_DIM, row >= HEAD_DIM)
    qts = {}
    for hd in range(DIFF_HPS):
        qt = qt_ref[0, 0, hd * LANES:(hd + 1) * LANES, :]
        for mm in range(2):
            qts[(hd, mm)] = jnp.where(map_rows[mm], qt, jnp.zeros_like(qt))

    m_ref[...] = jnp.full_like(m_ref, NEG)
    acc_ref[...] = jnp.zeros_like(acc_ref)

    def stage(s_ref, j):
        ks = k_ref[0, pl.ds(pl.multiple_of(j * TD, TD), TD), :]
        for hd, mm in chains:
            s_ref[hd, mm] = _dot(ks[:, hd * LANES:(hd + 1) * LANES], qts[(hd, mm)])

    def add_bias(s_ref, hd, mm, near):
        nb = TD // MAX_DISTANCE
        blk = lambda kb, qb: (hd, mm, slice(kb * MAX_DISTANCE, (kb + 1) * MAX_DISTANCE),
                              slice(qb * MAX_DISTANCE, (qb + 1) * MAX_DISTANCE))
        if near == 1:
            s_ref[blk(nb - 1, 0)] += bias_ref[hd, mm, 1]
            return
        for kb in range(nb):
            for qb in range(nb):
                if kb > qb:
                    s_ref[blk(kb, qb)] = jnp.full((MAX_DISTANCE, MAX_DISTANCE), NEG, F32)
                elif kb == qb:
                    s_ref[blk(kb, qb)] += bias_ref[hd, mm, 0]
                elif kb == qb - 1:
                    s_ref[blk(kb, qb)] += bias_ref[hd, mm, 1]

    def absorb(j, s_ref, near=None, offset=None, nxt=None):
        s, alpha = {}, {}
        for hd, mm in chains:
            if near is not None:
                add_bias(s_ref, hd, mm, near)
            sm = s_ref[hd, mm]
            if offset is not None:
                sm = sm + offset
            m_old = m_ref[hd, mm]
            m_new = jnp.maximum(m_old, jnp.max(sm, axis=0, keepdims=True))
            m_ref[hd, mm] = m_new
            alpha[(hd, mm)] = jnp.exp2(m_old - m_new)
            s[(hd, mm)] = sm - m_new
        if nxt is not None:
            stage(*nxt)
        p = {ch: jnp.exp2(s[ch]).astype(BF16) for ch in chains}
        vt = vt_ref[0, j]
        ones = jnp.ones((DIFF_SUM_ROWS, TD), BF16)
        for hd in range(DIFF_HPS):
            vts = jnp.concatenate([vt[hd * LANES:(hd + 1) * LANES], ones], axis=0)
            for mm in range(2):
                acc_ref[hd, mm] = alpha[(hd, mm)] * acc_ref[hd, mm] + _dot(vts, p[(hd, mm)])

    stage(sa_ref, i)
    absorb(i, sa_ref, near=0, nxt=(sb_ref, jnp.maximum(i - 1, 0)))
    absorb(jnp.maximum(i - 1, 0), sb_ref, near=1, offset=jnp.where(i >= 1, 0.0, NEG),
           nxt=(sa_ref, jnp.maximum(i - 2, 0)))

    @pl.when(i >= 2)
    def _():
        def body(u, carry):
            ja = i - 2 - 2 * u
            absorb(ja, sa_ref, nxt=(sb_ref, ja - 1))
            absorb(ja - 1, sb_ref, nxt=(sa_ref, jnp.maximum(ja - 2, 0)))
            return carry

        lax.fori_loop(0, (i - 1) // 2, body, 0)

        @pl.when((i - 1) % 2 == 1)
        def _():
            absorb(0, sa_ref)

    lam = (jnp.exp(jnp.sum(lam_ref[0:1, :] * lam_ref[1:2, :], axis=1, keepdims=True))
           - jnp.exp(jnp.sum(lam_ref[2:3, :] * lam_ref[3:4, :], axis=1, keepdims=True)) + lambda_init)
    for hd in range(DIFF_HPS):
        l0, l1 = acc_ref[hd, 0, LANES:LANES + 1, :], acc_ref[hd, 1, LANES:LANES + 1, :]
        ot = acc_ref[hd, 0, 0:LANES, :] / l0 - lam * (acc_ref[hd, 1, 0:LANES, :] / l1)
        o = jnp.transpose(ot)
        o_ref[0, :, hd * LANES:(hd + 1) * LANES] = (
            _rms(o, g_ref[...], SUBLN_EPS) * (1.0 - lambda_init)).astype(o_ref.dtype)


def _diff_attention(proj3, proj_t, bias, lam_params, subln_g, lambda_init):
    b, s, _ = proj3.shape
    width = DIFF_HPS * LANES
    groups = DIFF_WIDTH // width
    nt = s // TD
    return pl.pallas_call(
        functools.partial(_diff_kernel, lambda_init=lambda_init),
        grid=(b, groups, nt),
        in_specs=[pl.BlockSpec((1, 1, width, TD), lambda bb, h, i: (bb, i, h, 0)),
                  pl.BlockSpec((1, s, width), lambda bb, h, i: (bb, 0, groups + h), pipeline_mode=pl.Buffered(1)),
                  pl.BlockSpec((1, nt, width, TD), lambda bb, h, i: (bb, 0, groups + h, 0),
                               pipeline_mode=pl.Buffered(1)),
                  pl.BlockSpec((DIFF_HPS, 2, 2, MAX_DISTANCE, MAX_DISTANCE), lambda bb, h, i: (h, 0, 0, 0, 0)),
                  pl.BlockSpec((4, HEAD_DIM), lambda bb, h, i: (0, 0)),
                  pl.BlockSpec((1, LANES), lambda bb, h, i: (0, 0))],
        out_specs=pl.BlockSpec((1, TD, width), lambda bb, h, i: (bb, i, h)),
        out_shape=jax.ShapeDtypeStruct((b, s, DIFF_WIDTH), BF16),
        scratch_shapes=[pltpu.VMEM((DIFF_HPS, 2, 1, TD), F32),
                        pltpu.VMEM((DIFF_HPS, 2, LANES + DIFF_SUM_ROWS, TD), F32),
                        pltpu.VMEM((DIFF_HPS, 2, TD, TD), F32), pltpu.VMEM((DIFF_HPS, 2, TD, TD), F32)],
        compiler_params=_cparams(("parallel", "parallel", "arbitrary")),
        name="diff_attention",
    )(proj_t, proj3, proj_t, bias, lam_params, subln_g.reshape(1, LANES))


def _diff_bias(t5_table):
    cols = t5_table.astype(F32)[:, SB_HEADS:]
    cols = (cols - cols[NUM_BUCKETS - 1:NUM_BUCKETS, :]) * LOG2E
    ki = np.arange(MAX_DISTANCE)[:, None]
    qi = np.arange(MAX_DISTANCE)[None, :]
    d0 = qi - ki
    diag = _bias_tile(cols, d0, d0 >= 0)
    prev = _bias_tile(cols, d0 + MAX_DISTANCE, np.ones_like(d0, bool))
    return jnp.stack([diag, prev], axis=1).reshape(DIFF_HEADS, 2, 2, MAX_DISTANCE, MAX_DISTANCE)


def _dil_kernel(q_ref, kp_ref, kc_ref, vp_ref, vc_ref, bias_ref, o_ref, qf, kf, vf, acc_s, m_s, l_s):
    t_idx = pl.program_id(2)
    qf[...] = q_ref[0].astype(F32)
    kf[0:DT, :] = kp_ref[0].astype(F32)
    kf[DT:2 * DT, :] = kc_ref[0].astype(F32)
    vf[0:DT, :] = vp_ref[0].astype(F32)
    vf[DT:2 * DT, :] = vc_ref[0].astype(F32)
    lane = lax.broadcasted_iota(jnp.int32, (1, LANES), 1)
    head0 = lane < HEAD_DIM
    heads = (head0, jnp.logical_not(head0))

    for g, (_, r) in enumerate(DIL_BRANCHES):
        nblk = DT // (r * DQ)
        shift = nblk.bit_length() - 1

        def group(t0, carry, g=g, r=r, nblk=nblk, shift=shift):
            rows, k0s, pre, s, p, mx = {}, {}, {}, {}, {}, {}

            def scores(u):
                t = t0 * DIL_UNROLL + u
                c = lax.shift_right_logical(t, shift)
                n = jnp.bitwise_and(t, nblk - 1)
                q0 = c + n * (DQ * r)
                k0s[u] = DT + q0 - DQ * r
                rows[u] = pl.ds(q0, DQ, stride=r)
                qg = qf[rows[u], :].astype(BF16)
                kg = kf[pl.ds(k0s[u], 2 * DQ, stride=r), :].astype(BF16)
                for hh in range(2):
                    s[(u, hh)] = _dot_nt(jnp.where(heads[hh], qg, jnp.zeros_like(qg)), kg)
                pre[u] = jnp.where(jnp.logical_and(t_idx == 0, n == 0), 1, 0)

            def softmax(u):
                for hh in range(2):
                    sb = s.pop((u, hh)) + bias_ref[0, hh, g, pre[u]]
                    m = jnp.max(sb, axis=1, keepdims=True)
                    p[(u, hh)] = jnp.exp2(sb - m).astype(BF16)
                    mx[(u, hh)] = jnp.broadcast_to(m, (DQ, LANES))

            def values(u):
                vg = vf[pl.ds(k0s[u], 2 * DQ, stride=r), :].astype(BF16)
                res = [_dot(p.pop((u, hh)), jnp.where(heads[hh], vg, jnp.ones_like(vg))) for hh in range(2)]
                acc_s[g, rows[u], :] = jnp.where(head0, res[0], res[1])
                m_s[g, rows[u], :] = jnp.where(head0, mx.pop((u, 0)), mx.pop((u, 1)))
                l_s[g, rows[u], :] = pltpu.roll(jnp.where(head0, res[1], res[0]), HEAD_DIM, 1)

            for step in range(DIL_UNROLL + 2 * DIL_SKEW):
                if step < DIL_UNROLL:
                    scores(step)
                if 0 <= step - DIL_SKEW < DIL_UNROLL:
                    softmax(step - DIL_SKEW)
                if step >= 2 * DIL_SKEW:
                    values(step - 2 * DIL_SKEW)
            return carry

        lax.fori_loop(0, r * nblk // DIL_UNROLL, group, 0)

    def merge(ci, carry):
        rows = pl.ds(pl.multiple_of(ci * TQ, TQ), TQ)
        ms = [m_s[g, rows, :] for g in range(len(DIL_BRANCHES))]
        m = functools.reduce(jnp.maximum, ms)
        num = jnp.zeros((TQ, LANES), F32)
        den = jnp.zeros((TQ, LANES), F32)
        for g in range(len(DIL_BRANCHES)):
            wgt = jnp.exp2(ms[g] - m)
            num = num + wgt * acc_s[g, rows, :]
            den = den + wgt * l_s[g, rows, :]
        o_ref[0, rows, :] = (num / den).astype(o_ref.dtype)
        return carry

    lax.fori_loop(0, DT // TQ, merge, 0)


def _dil_bias(t5_table):
    qi = np.arange(DQ)[:, None]
    ki = np.arange(2 * DQ)[None, :] - DQ
    steps = qi - ki
    valid = (steps >= 0) & (steps <= DIL_WINDOW)
    table = t5_table.astype(F32) * LOG2E
    tiles = [jnp.stack([_bias_tile(table, steps * r, v) for v in (valid, valid & (ki >= 0))], axis=1)
             for _, r in DIL_BRANCHES]
    return jnp.stack(tiles, axis=1).reshape(DIL_HEADS // 2, 2, len(DIL_BRANCHES), 2, DQ, 2 * DQ)


def _dilated_attention(proj3, t5_table):
    b, s, _ = proj3.shape
    pairs = DIL_HEADS // 2
    bias = _dil_bias(t5_table)
    cur = lambda off: (lambda bb, hp, t: (bb, t, off * pairs + hp))
    prev = lambda off: (lambda bb, hp, t: (bb, jnp.maximum(t - 1, 0), off * pairs + hp))
    tile = lambda imap: pl.BlockSpec((1, DT, LANES), imap)
    slab = lambda rows: pltpu.VMEM((rows, LANES), F32)
    per_branch = pltpu.VMEM((len(DIL_BRANCHES), DT, LANES), F32)
    return pl.pallas_call(
        _dil_kernel,
        grid=(b, pairs, s // DT),
        in_specs=[tile(cur(0)), tile(prev(1)), tile(cur(1)), tile(prev(2)), tile(cur(2)),
                  pl.BlockSpec((1, 2, len(DIL_BRANCHES), 2, DQ, 2 * DQ), lambda bb, hp, t: (hp, 0, 0, 0, 0, 0))],
        out_specs=tile(cur(0)),
        out_shape=jax.ShapeDtypeStruct((b, s, D_MODEL), BF16),
        scratch_shapes=[slab(DT), slab(2 * DT), slab(2 * DT), per_branch, per_branch, per_branch],
        compiler_params=_cparams(("parallel", "parallel", "arbitrary")),
        name="dilated_attention",
    )(proj3, proj3, proj3, proj3, proj3, bias)


def _post_kernel(*refs, n_o, final):
    h_ref = refs[0]
    o_refs = refs[1:1 + n_o]
    (wo_ref, p_ref, g_mlp_ref, w_up_ref, w_down_ref, g_ple_ref, w_gate_ref, w_proj_ref, g_fin_ref,
     out_ref, acc_ref) = refs[1 + n_o:]
    o = o_refs[0][...] if n_o == 1 else jnp.concatenate([r[...] for r in o_refs], axis=1)
    h = h_ref[...] + _dot(o, wo_ref[...])
    hn = _rms(h, g_mlp_ref[...], NORM_EPS).astype(BF16)
    acc_ref[...] = h
    for c in range(D_FF // FF_CHUNK):
        u = jnp.maximum(_dot(hn, w_up_ref[:, c * FF_CHUNK:(c + 1) * FF_CHUNK]), 0.0)
        acc_ref[...] += _dot((u * u).astype(BF16), w_down_ref[c * FF_CHUNK:(c + 1) * FF_CHUNK, :])
    h = acc_ref[...]
    gate = jax.nn.sigmoid(_dot(_rms(h, g_ple_ref[...], NORM_EPS).astype(BF16), w_gate_ref[...]))
    h = h + _dot(p_ref[...].astype(BF16), w_proj_ref[...]) * gate
    if final:
        h = _rms(h, g_fin_ref[...], NORM_EPS)
    out_ref[...] = h


def _post(h2d, o_list, w_out, p2d, g_mlp, w_up, w_down, g_ple, w_gate, w_proj, g_fin, final):
    m, d = h2d.shape
    n_o = len(o_list)
    row = lambda width: pl.BlockSpec((TM_POST, width), lambda i: (i, 0))
    const = lambda a: pl.BlockSpec(a.shape, lambda i: (0,) * a.ndim, pipeline_mode=pl.Buffered(1))
    vec = lambda g: g.reshape(1, d).astype(F32)
    weights = [w_out, vec(g_mlp), w_up, w_down, vec(g_ple), w_gate, w_proj, vec(g_fin)]
    return pl.pallas_call(
        functools.partial(_post_kernel, n_o=n_o, final=final),
        grid=(m // TM_POST,),
        in_specs=([row(d)] + [row(o.shape[1]) for o in o_list] + [const(weights[0])]
                  + [row(p2d.shape[1])] + [const(w) for w in weights[1:]]),
        out_specs=row(d),
        out_shape=jax.ShapeDtypeStruct((m, d), F32),
        scratch_shapes=[pltpu.VMEM((TM_POST, d), F32)],
        compiler_params=_cparams(("parallel",)),
        name="out_mlp_ple",
    )(h2d, *o_list, weights[0], p2d, *weights[1:])


def _scale_cols(w_in, col_scale):
    return (w_in * jnp.asarray(col_scale, F32)[None, :]).astype(BF16)


def kernel(x, p, t5_table, w_in_even, w_out_even, lambda_q1, lambda_k1, lambda_q2, lambda_k2, subln_g,
           w_in_odd, w_out_odd, norm_mix_g, norm_mlp_g, w_mlp_up, w_mlp_down, norm_ple_g, w_ple_gate,
           w_ple_proj, final_norm_g):
    b, s, d = x.shape
    depth = p.shape[0]
    assert d == D_MODEL and s % DT == 0 and (b * s) % TM_POST == 0
    assert all(w == DIL_WINDOW * r and w <= DT for w, r in DIL_BRANCHES)
    col = np.arange(3 * D_MODEL)
    qk_scale = HEAD_DIM ** -0.5
    even_q = (col < SB_WIDTH) | ((col >= 3 * SB_WIDTH) & (col < 3 * SB_WIDTH + DIFF_WIDTH))
    even_scale = np.where(even_q, qk_scale * LOG2E, 1.0)
    odd_scale = np.where(col < DIL_HEADS * HEAD_DIM, qk_scale * LOG2E, 1.0)
    h = x.reshape(b * s, d)
    for i in range(depth):
        if i % 2 == 0:
            e = i // 2
            w_in = _scale_cols(w_in_even[e], even_scale)
            qa, ka, va, qb, kb, vb = (w_in[:, lo:lo + SB_WIDTH] for lo in range(0, 6 * SB_WIDTH, SB_WIDTH))
            proj, qa_t, va_t, t_d = _norm_matmul(
                h, norm_mix_g[i], jnp.concatenate([ka, kb], axis=1),
                ((qa.T, TS), (va.T, SB_HALF), (jnp.concatenate([qb, vb], axis=1).T, TD)))
            proj = proj.reshape(b, s, -1)
            qa_t = qa_t.reshape(b, s // TS, SB_WIDTH, TS)
            va_t = va_t.reshape(b, s // SB_HALF, SB_WIDTH, SB_HALF)
            t_d = t_d.reshape(b, s // TD, 2 * DIFF_WIDTH, TD)
            lambda_init = 0.8 - 0.6 * math.exp(-0.3 * i)
            lam_params = jnp.stack([lambda_q1[e], lambda_k1[e], lambda_q2[e], lambda_k2[e]]).astype(F32)
            o_sb = _sb_attention(proj, qa_t, va_t)
            o_d = _diff_attention(proj, t_d, _diff_bias(t5_table), lam_params, subln_g[e].astype(F32),
                                  lambda_init)
            w_out = w_out_even[e].astype(BF16)
            o_list = [o_sb.reshape(b * s, SB_WIDTH), o_d.reshape(b * s, DIFF_WIDTH)]
        else:
            o = i // 2
            (proj,) = _norm_matmul(h, norm_mix_g[i], _scale_cols(w_in_odd[o], odd_scale))
            proj = proj.reshape(b, s, -1)
            o_list = [_dilated_attention(proj, t5_table).reshape(b * s, d)]
            w_out = w_out_odd[o].astype(BF16)
        h = _post(h, o_list, w_out, p[i].reshape(b * s, PLE_DIM), norm_mlp_g[i],
                  w_mlp_up[i].astype(BF16), w_mlp_down[i].astype(BF16), norm_ple_g[i],
                  w_ple_gate[i].astype(BF16), w_ple_proj[i].astype(BF16), final_norm_g,
                  final=(i == depth - 1))
    return h.reshape(b, s, d)
```

```python
import functools
import math

import numpy as np
import jax
import jax.numpy as jnp
from jax import lax
from jax.experimental import pallas as pl
from jax.experimental.pallas import tpu as pltpu

F32 = jnp.float32
BF16 = jnp.bfloat16

D_MODEL = 1024
HEAD_DIM = 64
LANES = 128
SB_HEADS = 8
DIFF_HEADS = 4
DIL_HEADS = 16
SB_WIDTH = SB_HEADS * HEAD_DIM
DIFF_WIDTH = DIFF_HEADS * 2 * HEAD_DIM
DIL_BRANCHES = ((128, 1), (512, 4), (2048, 16))
DIL_WINDOW = 128
NUM_BUCKETS = 32
MAX_DISTANCE = 128
D_FF = 4 * D_MODEL
PLE_DIM = 256
NORM_EPS = 1e-6
SUBLN_EPS = 1e-5
NEG = -1e30
LOG2E = math.log2(math.e)
SB_DEAD_LOG = 88.0

VMEM_LIMIT_BYTES = 60 * 1024 * 1024

TM_PROJ = 512
TM_POST = 512
FF_CHUNK = 1024
TQ = 256
TS = 256
SB_QS = 4
SB_HALF = 128
TD = TM_PROJ
DIFF_SUM_ROWS = 16
DIFF_HPS = 4
DT = 2048
DQ = DIL_WINDOW
DIL_UNROLL = 16
DIL_SKEW = 3


def _cparams(sem):
    return pltpu.CompilerParams(dimension_semantics=sem, vmem_limit_bytes=VMEM_LIMIT_BYTES)


def _rms(x, g, eps):
    return x * lax.rsqrt(jnp.mean(x * x, axis=-1, keepdims=True) + eps) * g


def _dot(a, b):
    return jnp.dot(a, b, preferred_element_type=F32)


def _dot_nt(a, b):
    return lax.dot_general(a, b, (((1,), (1,)), ((), ())), preferred_element_type=F32)


def _norm_matmul_kernel(h_ref, g_ref, w_ref, *rest, slabs):
    hn = _rms(h_ref[...], g_ref[...], NORM_EPS).astype(BF16)
    n_t = len(slabs)
    wt_refs, o_ref, ot_refs = rest[:n_t], rest[n_t], rest[n_t + 1:]
    for wt_ref, ot_ref, width in zip(wt_refs, ot_refs, slabs):
        res = _dot_nt(wt_ref[...], hn).astype(ot_ref.dtype)
        for c in range(TM_PROJ // width):
            ot_ref[c] = res[:, c * width:(c + 1) * width]
    o_ref[...] = _dot(hn, w_ref[...]).astype(o_ref.dtype)


def _norm_matmul(h2d, g, w, wts=()):
    m, k = h2d.shape
    n = w.shape[1]
    const = lambda a: pl.BlockSpec(a.shape, lambda i: (0,) * a.ndim, pipeline_mode=pl.Buffered(1))
    in_specs = [pl.BlockSpec((TM_PROJ, k), lambda i: (i, 0)), pl.BlockSpec((1, k), lambda i: (0, 0)), const(w)]
    in_specs += [const(wt) for wt, _ in wts]
    out_specs = [pl.BlockSpec((TM_PROJ, n), lambda i: (i, 0))]
    out_shape = [jax.ShapeDtypeStruct((m, n), BF16)]
    for wt, width in wts:
        per_step = TM_PROJ // width
        out_specs.append(pl.BlockSpec((per_step, wt.shape[0], width), lambda i: (i, 0, 0)))
        out_shape.append(jax.ShapeDtypeStruct((m // width, wt.shape[0], width), BF16))
    return pl.pallas_call(
        functools.partial(_norm_matmul_kernel, slabs=tuple(width for _, width in wts)),
        grid=(m // TM_PROJ,),
        in_specs=in_specs, out_specs=out_specs, out_shape=out_shape,
        compiler_params=_cparams(("parallel",)),
        name="norm_in_proj",
    )(h2d, g.reshape(1, k), w, *[wt for wt, _ in wts])


def _t5_bucket_np(dist):
    n = np.maximum(dist, 0)
    max_exact = NUM_BUCKETS // 2
    nf = np.maximum(n, 1).astype(np.float32)
    large = max_exact + (np.log(nf / np.float32(max_exact)) / np.float32(math.log(MAX_DISTANCE / max_exact))
                         * np.float32(NUM_BUCKETS - max_exact)).astype(np.int32)
    large = np.minimum(large, NUM_BUCKETS - 1)
    return np.where(n < max_exact, n, large).astype(np.int32)


def _bias_tile(table_cols, dist, valid):
    bucket = _t5_bucket_np(dist)
    out = jnp.zeros((table_cols.shape[1],) + dist.shape, F32)
    for bkt in np.unique(bucket[valid]):
        out = jnp.where(jnp.asarray(bucket == bkt)[None], table_cols[bkt][:, None, None], out)
    return jnp.where(jnp.asarray(valid)[None], out, NEG)


def _sb_kernel(qt_ref, k_ref, vt_ref, o_ref, acc_ref, c_ref):
    i = pl.program_id(2)
    row = lax.broadcasted_iota(jnp.int32, (LANES, 1), 0)
    head_rows = (row < HEAD_DIM, row >= HEAD_DIM)
    qts = [tuple(jnp.where(hr, qt_ref[0, a], jnp.zeros((LANES, TS), BF16)) for hr in head_rows)
           for a in range(SB_QS)]
    kk = lax.broadcasted_iota(jnp.int32, (TS, TS), 0)
    qq = lax.broadcasted_iota(jnp.int32, (TS, TS), 1)
    strict = kk < qq
    per_half = TS // SB_HALF

    def tri2(n):
        tri = (lax.broadcasted_iota(jnp.int32, (n, n), 0) <= lax.broadcasted_iota(jnp.int32, (n, n), 1)).astype(BF16)
        return jnp.concatenate([tri, tri], axis=1)

    tri2s = {n: tri2(n) for n in (TS, SB_HALF)}

    def tiles(work, c_in):
        items = [(n, hh) for n in range(len(work)) for hh in range(2)]
        ks = [k_ref[0, pl.ds(pl.multiple_of(k0, SB_HALF), nk), :] for _, k0, nk, _, _ in work]
        z = {(n, hh): _dot(ks[n], qts[work[n][0]][hh]) for n, hh in items}
        parts = {}
        for it in items:
            keep = work[it[0]][4]
            if keep is not None:
                z[it] = jnp.where(keep, z[it], NEG)
            lk = -(jnp.maximum(z[it], 0.0) + jnp.log2(1.0 + jnp.exp2(-jnp.abs(z[it]))))
            hi = lk.astype(BF16)
            parts[it] = jnp.concatenate([hi, (lk - hi.astype(F32)).astype(BF16)], axis=0)
        incl = {it: _dot(tri2s[work[it[0]][2]], parts[it]) for it in items}
        c = dict(c_in)
        w = {}
        for n, hh in items:
            a = work[n][0]
            w[(n, hh)] = jnp.exp2(z[(n, hh)] + incl[(n, hh)] + c[(a, hh)]).astype(BF16)
            c[(a, hh)] = c[(a, hh)] + incl[(n, hh)][0:1, :]
        per_head = {}
        for n, hh in items:
            a = work[n][0]
            part = _dot(work[n][3](), w[(n, hh)])
            per_head[(a, hh)] = part if (a, hh) not in per_head else per_head[(a, hh)] + part
        upd = {a: jnp.where(head_rows[0], per_head[(a, 0)], per_head[(a, 1)]) for a, _ in per_head}
        return upd, c

    def half_tile(a, e, keep=None):
        return (a, e * SB_HALF, SB_HALF, lambda: vt_ref[0, e], keep)

    def diag_vt(g):
        return jnp.concatenate([vt_ref[0, g * per_half + c] for c in range(per_half)], axis=1)

    zero_c = jnp.zeros((1, TS), F32)
    work = []
    for a in range(SB_QS):
        g = i * SB_QS + a
        work.append((a, g * TS, TS, functools.partial(diag_vt, g), strict))
        if a > 0:
            work.append(half_tile(a, g * per_half - 1))
        else:
            exists = lax.broadcasted_iota(jnp.int32, (SB_HALF, TS), 0) < jnp.where(g >= 1, SB_HALF, 0)
            work.append(half_tile(a, jnp.maximum(g * per_half - 1, 0), exists))
    upd, c1 = tiles(work, {(a, hh): zero_c for a in range(SB_QS) for hh in range(2)})
    for a in range(SB_QS):
        acc_ref[a] = upd[a]
        for hh in range(2):
            c_ref[a, hh] = c1[(a, hh)]

    for a in range(SB_QS):
        def cond(state):
            e, alive = state
            return jnp.logical_and(e >= 0, alive > -SB_DEAD_LOG * LOG2E)

        def body(state, a=a):
            e, _ = state
            upd, c_new = tiles([half_tile(a, e)], {(a, hh): c_ref[a, hh] for hh in range(2)})
            acc_ref[a] += upd[a]
            for hh in range(2):
                c_ref[a, hh] = c_new[(a, hh)]
            return e - 1, jnp.maximum(jnp.max(c_new[(a, 0)]), jnp.max(c_new[(a, 1)]))

        alive1 = jnp.maximum(jnp.max(c1[(a, 0)]), jnp.max(c1[(a, 1)]))
        lax.while_loop(cond, body, ((i * SB_QS + a) * per_half - 2, alive1))
        o_ref[0, a * TS:(a + 1) * TS, :] = jnp.transpose(acc_ref[a]).astype(o_ref.dtype)


def _sb_attention(proj3, q_t, v_t):
    b, s, _ = proj3.shape
    pairs = SB_WIDTH // LANES
    return pl.pallas_call(
        _sb_kernel,
        grid=(b, pairs, s // (SB_QS * TS)),
        in_specs=[pl.BlockSpec((1, SB_QS, LANES, TS), lambda bb, hp, i: (bb, i, hp, 0)),
                  pl.BlockSpec((1, s, LANES), lambda bb, hp, i: (bb, 0, hp)),
                  pl.BlockSpec((1, s // SB_HALF, LANES, SB_HALF), lambda bb, hp, i: (bb, 0, hp, 0))],
        out_specs=pl.BlockSpec((1, SB_QS * TS, LANES), lambda bb, hp, i: (bb, i, hp)),
        out_shape=jax.ShapeDtypeStruct((b, s, SB_WIDTH), BF16),
        scratch_shapes=[pltpu.VMEM((SB_QS, LANES, TS), F32), pltpu.VMEM((SB_QS, 2, 1, TS), F32)],
        compiler_params=_cparams(("parallel", "parallel", "arbitrary")),
        name="sb_attention",
    )(q_t, proj3, v_t)


def _diff_kernel(qt_ref, k_ref, vt_ref, bias_ref, lam_ref, g_ref, o_ref, m_ref, acc_ref, sa_ref, sb_ref,
                 *, lambda_init):
    i = pl.program_id(2)
    sb_ref = sb_ref.at[:, :, :, LANES:LANES + TD]
    chains = [(hd, mm) for hd in range(DIFF_HPS) for mm in range(2)]
    row = lax.broadcasted_iota(jnp.int32, (LANES, 1), 0)
    map_rows = (row < HEAD_DIM, row >= HEAD_DIM)
    qts = {}
    for hd in range(DIFF_HPS):
        qt = qt_ref[0, 0, hd * LANES:(hd + 1) * LANES, :]
        for mm in range(2):
            qts[(hd, mm)] = jnp.where(map_rows[mm], qt, jnp.zeros_like(qt))

    m_ref[...] = jnp.full_like(m_ref, NEG)
    acc_ref[...] = jnp.zeros_like(acc_ref)

    def stage(s_ref, j):
        ks = k_ref[0, pl.ds(pl.multiple_of(j * TD, TD), TD), :]
        for hd, mm in chains:
            s_ref[hd, mm] = _dot(ks[:, hd * LANES:(hd + 1) * LANES], qts[(hd, mm)])

    def add_bias(s_ref, hd, mm, near):
        nb = TD // MAX_DISTANCE
        blk = lambda kb, qb: (hd, mm, slice(kb * MAX_DISTANCE, (kb + 1) * MAX_DISTANCE),
                              slice(qb * MAX_DISTANCE, (qb + 1) * MAX_DISTANCE))
        if near == 1:
            s_ref[blk(nb - 1, 0)] += bias_ref[hd, mm, 1]
            return
        for kb in range(nb):
            for qb in range(nb):
                if kb > qb:
                    s_ref[blk(kb, qb)] = jnp.full((MAX_DISTANCE, MAX_DISTANCE), NEG, F32)
                elif kb == qb:
                    s_ref[blk(kb, qb)] += bias_ref[hd, mm, 0]
                elif kb == qb - 1:
                    s_ref[blk(kb, qb)] += bias_ref[hd, mm, 1]

    def absorb(j, s_ref, near=None, offset=None, nxt=None):
        s, alpha = {}, {}
        for hd, mm in chains:
            if near is not None:
                add_bias(s_ref, hd, mm, near)
            sm = s_ref[hd, mm]
            if offset is not None:
                sm = sm + offset
            m_old = m_ref[hd, mm]
            m_new = jnp.maximum(m_old, jnp.max(sm, axis=0, keepdims=True))
            m_ref[hd, mm] = m_new
            alpha[(hd, mm)] = jnp.exp2(m_old - m_new)
            s[(hd, mm)] = sm - m_new
        if nxt is not None:
            stage(*nxt)
        p = {ch: jnp.exp2(s[ch]).astype(BF16) for ch in chains}
        vt = vt_ref[0, j]
        ones = jnp.ones((DIFF_SUM_ROWS, TD), BF16)
        for hd in range(DIFF_HPS):
            vts = jnp.concatenate([vt[hd * LANES:(hd + 1) * LANES], ones], axis=0)
            for mm in range(2):
                acc_ref[hd, mm] = alpha[(hd, mm)] * acc_ref[hd, mm] + _dot(vts, p[(hd, mm)])

    stage(sa_ref, i)
    absorb(i, sa_ref, near=0, nxt=(sb_ref, jnp.maximum(i - 1, 0)))
    absorb(jnp.maximum(i - 1, 0), sb_ref, near=1, offset=jnp.where(i >= 1, 0.0, NEG),
           nxt=(sa_ref, jnp.maximum(i - 2, 0)))

    @pl.when(i >= 2)
    def _():
        def body(u, carry):
            ja = i - 2 - 2 * u
            absorb(ja, sa_ref, nxt=(sb_ref, ja - 1))
            absorb(ja - 1, sb_ref, nxt=(sa_ref, jnp.maximum(ja - 2, 0)))
            return carry

        lax.fori_loop(0, (i - 1) // 2, body, 0)

        @pl.when((i - 1) % 2 == 1)
        def _():
            absorb(0, sa_ref)

    lam = (jnp.exp(jnp.sum(lam_ref[0:1, :] * lam_ref[1:2, :], axis=1, keepdims=True))
           - jnp.exp(jnp.sum(lam_ref[2:3, :] * lam_ref[3:4, :], axis=1, keepdims=True)) + lambda_init)
    for hd in range(DIFF_HPS):
        l0, l1 = acc_ref[hd, 0, LANES:LANES + 1, :], acc_ref[hd, 1, LANES:LANES + 1, :]
        ot = acc_ref[hd, 0, 0:LANES, :] / l0 - lam * (acc_ref[hd, 1, 0:LANES, :] / l1)
        o = jnp.transpose(ot)
        o_ref[0, :, hd * LANES:(hd + 1) * LANES] = (
            _rms(o, g_ref[...], SUBLN_EPS) * (1.0 - lambda_init)).astype(o_ref.dtype)


def _diff_attention(proj3, proj_t, bias, lam_params, subln_g, lambda_init):
    b, s, _ = proj3.shape
    width = DIFF_HPS * LANES
    groups = DIFF_WIDTH // width
    nt = s // TD
    return pl.pallas_call(
        functools.partial(_diff_kernel, lambda_init=lambda_init),
        grid=(b, groups, nt),
        in_specs=[pl.BlockSpec((1, 1, width, TD), lambda bb, h, i: (bb, i, h, 0)),
                  pl.BlockSpec((1, s, width), lambda bb, h, i: (bb, 0, groups + h), pipeline_mode=pl.Buffered(1)),
                  pl.BlockSpec((1, nt, width, TD), lambda bb, h, i: (bb, 0, groups + h, 0),
                               pipeline_mode=pl.Buffered(1)),
                  pl.BlockSpec((DIFF_HPS, 2, 2, MAX_DISTANCE, MAX_DISTANCE), lambda bb, h, i: (h, 0, 0, 0, 0)),
                  pl.BlockSpec((4, HEAD_DIM), lambda bb, h, i: (0, 0)),
                  pl.BlockSpec((1, LANES), lambda bb, h, i: (0, 0))],
        out_specs=pl.BlockSpec((1, TD, width), lambda bb, h, i: (bb, i, h)),
        out_shape=jax.ShapeDtypeStruct((b, s, DIFF_WIDTH), BF16),
        scratch_shapes=[pltpu.VMEM((DIFF_HPS, 2, 1, TD), F32),
                        pltpu.VMEM((DIFF_HPS, 2, LANES + DIFF_SUM_ROWS, TD), F32),
                        pltpu.VMEM((DIFF_HPS, 2, TD, TD), F32), pltpu.VMEM((DIFF_HPS, 2, TD, TD + LANES), F32)],
        compiler_params=_cparams(("parallel", "parallel", "arbitrary")),
        name="diff_attention",
    )(proj_t, proj3, proj_t, bias, lam_params, subln_g.reshape(1, LANES))


def _diff_bias(t5_table):
    cols = t5_table.astype(F32)[:, SB_HEADS:]
    cols = (cols - cols[NUM_BUCKETS - 1:NUM_BUCKETS, :]) * LOG2E
    ki = np.arange(MAX_DISTANCE)[:, None]
    qi = np.arange(MAX_DISTANCE)[None, :]
    d0 = qi - ki
    diag = _bias_tile(cols, d0, d0 >= 0)
    prev = _bias_tile(cols, d0 + MAX_DISTANCE, np.ones_like(d0, bool))
    return jnp.stack([diag, prev], axis=1).reshape(DIFF_HEADS, 2, 2, MAX_DISTANCE, MAX_DISTANCE)


def _dil_kernel(q_ref, kp_ref, kc_ref, vp_ref, vc_ref, bias_ref, o_ref, qf, kf, vf, acc_s, m_s, l_s):
    t_idx = pl.program_id(2)
    qf[...] = q_ref[0].astype(F32)
    kf[0:DT, :] = kp_ref[0].astype(F32)
    kf[DT:2 * DT, :] = kc_ref[0].astype(F32)
    vf[0:DT, :] = vp_ref[0].astype(F32)
    vf[DT:2 * DT, :] = vc_ref[0].astype(F32)
    lane = lax.broadcasted_iota(jnp.int32, (1, LANES), 1)
    head0 = lane < HEAD_DIM
    heads = (head0, jnp.logical_not(head0))

    for g, (_, r) in enumerate(DIL_BRANCHES):
        nblk = DT // (r * DQ)
        shift = nblk.bit_length() - 1

        def group(t0, carry, g=g, r=r, nblk=nblk, shift=shift):
            rows, k0s, pre, s, p, mx = {}, {}, {}, {}, {}, {}

            def scores(u):
                t = t0 * DIL_UNROLL + u
                c = lax.shift_right_logical(t, shift)
                n = jnp.bitwise_and(t, nblk - 1)
                q0 = c + n * (DQ * r)
                k0s[u] = DT + q0 - DQ * r
                rows[u] = pl.ds(q0, DQ, stride=r)
                qg = qf[rows[u], :].astype(BF16)
                kg = kf[pl.ds(k0s[u], 2 * DQ, stride=r), :].astype(BF16)
                for hh in range(2):
                    s[(u, hh)] = _dot_nt(jnp.where(heads[hh], qg, jnp.zeros_like(qg)), kg)
                pre[u] = jnp.where(jnp.logical_and(t_idx == 0, n == 0), 1, 0)

            def softmax(u):
                for hh in range(2):
                    sb = s.pop((u, hh)) + bias_ref[0, hh, g, pre[u]]
                    m = jnp.max(sb, axis=1, keepdims=True)
                    p[(u, hh)] = jnp.exp2(sb - m).astype(BF16)
                    mx[(u, hh)] = jnp.broadcast_to(m, (DQ, LANES))

            def values(u):
                vg = vf[pl.ds(k0s[u], 2 * DQ, stride=r), :].astype(BF16)
                res = [_dot(p.pop((u, hh)), jnp.where(heads[hh], vg, jnp.ones_like(vg))) for hh in range(2)]
                acc_s[g, rows[u], :] = jnp.where(head0, res[0], res[1])
                m_s[g, rows[u], :] = jnp.where(head0, mx.pop((u, 0)), mx.pop((u, 1)))
                l_s[g, rows[u], :] = pltpu.roll(jnp.where(head0, res[1], res[0]), HEAD_DIM, 1)

            for step in range(DIL_UNROLL + 2 * DIL_SKEW):
                if step < DIL_UNROLL:
                    scores(step)
                if 0 <= step - DIL_SKEW < DIL_UNROLL:
                    softmax(step - DIL_SKEW)
                if step >= 2 * DIL_SKEW:
                    values(step - 2 * DIL_SKEW)
            return carry

        lax.fori_loop(0, r * nblk // DIL_UNROLL, group, 0)

    def merge(ci, carry):
        rows = pl.ds(pl.multiple_of(ci * TQ, TQ), TQ)
        ms = [m_s[g, rows, :] for g in range(len(DIL_BRANCHES))]
        m = functools.reduce(jnp.maximum, ms)
        num = jnp.zeros((TQ, LANES), F32)
        den = jnp.zeros((TQ, LANES), F32)
        for g in range(len(DIL_BRANCHES)):
            wgt = jnp.exp2(ms[g] - m)
            num = num + wgt * acc_s[g, rows, :]
            den = den + wgt * l_s[g, rows, :]
        o_ref[0, rows, :] = (num / den).astype(o_ref.dtype)
        return carry

    lax.fori_loop(0, DT // TQ, merge, 0)


def _dil_bias(t5_table):
    qi = np.arange(DQ)[:, None]
    ki = np.arange(2 * DQ)[None, :] - DQ
    steps = qi - ki
    valid = (steps >= 0) & (steps <= DIL_WINDOW)
    table = t5_table.astype(F32) * LOG2E
    tiles = [jnp.stack([_bias_tile(table, steps * r, v) for v in (valid, valid & (ki >= 0))], axis=1)
             for _, r in DIL_BRANCHES]
    return jnp.stack(tiles, axis=1).reshape(DIL_HEADS // 2, 2, len(DIL_BRANCHES), 2, DQ, 2 * DQ)


def _dilated_attention(proj3, t5_table):
    b, s, _ = proj3.shape
    pairs = DIL_HEADS // 2
    bias = _dil_bias(t5_table)
    cur = lambda off: (lambda bb, hp, t: (bb, t, off * pairs + hp))
    prev = lambda off: (lambda bb, hp, t: (bb, jnp.maximum(t - 1, 0), off * pairs + hp))
    tile = lambda imap: pl.BlockSpec((1, DT, LANES), imap)
    slab = lambda rows: pltpu.VMEM((rows, LANES), F32)
    per_branch = pltpu.VMEM((len(DIL_BRANCHES), DT, LANES), F32)
    return pl.pallas_call(
        _dil_kernel,
        grid=(b, pairs, s // DT),
        in_specs=[tile(cur(0)), tile(prev(1)), tile(cur(1)), tile(prev(2)), tile(cur(2)),
                  pl.BlockSpec((1, 2, len(DIL_BRANCHES), 2, DQ, 2 * DQ), lambda bb, hp, t: (hp, 0, 0, 0, 0, 0))],
        out_specs=tile(cur(0)),
        out_shape=jax.ShapeDtypeStruct((b, s, D_MODEL), BF16),
        scratch_shapes=[slab(DT), slab(2 * DT), slab(2 * DT), per_branch, per_branch, per_branch],
        compiler_params=_cparams(("parallel", "parallel", "arbitrary")),
        name="dilated_attention",
    )(proj3, proj3, proj3, proj3, proj3, bias)


def _post_kernel(*refs, n_o, final):
    h_ref = refs[0]
    o_refs = refs[1:1 + n_o]
    (wo_ref, p_ref, g_mlp_ref, w_up_ref, w_down_ref, g_ple_ref, w_gate_ref, w_proj_ref, g_fin_ref,
     out_ref, acc_ref) = refs[1 + n_o:]
    o = o_refs[0][...] if n_o == 1 else jnp.concatenate([r[...] for r in o_refs], axis=1)
    h = h_ref[...] + _dot(o, wo_ref[...])
    hn = _rms(h, g_mlp_ref[...], NORM_EPS).astype(BF16)
    acc_ref[...] = h
    for c in range(D_FF // FF_CHUNK):
        u = jnp.maximum(_dot(hn, w_up_ref[:, c * FF_CHUNK:(c + 1) * FF_CHUNK]), 0.0)
        acc_ref[...] += _dot((u * u).astype(BF16), w_down_ref[c * FF_CHUNK:(c + 1) * FF_CHUNK, :])
    h = acc_ref[...]
    gate = jax.nn.sigmoid(_dot(_rms(h, g_ple_ref[...], NORM_EPS).astype(BF16), w_gate_ref[...]))
    h = h + _dot(p_ref[...].astype(BF16), w_proj_ref[...]) * gate
    if final:
        h = _rms(h, g_fin_ref[...], NORM_EPS)
    out_ref[...] = h


def _post(h2d, o_list, w_out, p2d, g_mlp, w_up, w_down, g_ple, w_gate, w_proj, g_fin, final):
    m, d = h2d.shape
    n_o = len(o_list)
    row = lambda width: pl.BlockSpec((TM_POST, width), lambda i: (i, 0))
    const = lambda a: pl.BlockSpec(a.shape, lambda i: (0,) * a.ndim, pipeline_mode=pl.Buffered(1))
    vec = lambda g: g.reshape(1, d).astype(F32)
    weights = [w_out, vec(g_mlp), w_up, w_down, vec(g_ple), w_gate, w_proj, vec(g_fin)]
    return pl.pallas_call(
        functools.partial(_post_kernel, n_o=n_o, final=final),
        grid=(m // TM_POST,),
        in_specs=([row(d)] + [row(o.shape[1]) for o in o_list] + [const(weights[0])]
                  + [row(p2d.shape[1])] + [const(w) for w in weights[1:]]),
        out_specs=row(d),
        out_shape=jax.ShapeDtypeStruct((m, d), F32),
        scratch_shapes=[pltpu.VMEM((TM_POST, d), F32)],
        compiler_params=_cparams(("parallel",)),
        name="out_mlp_ple",
    )(h2d, *o_list, weights[0], p2d, *weights[1:])


def _scale_cols(w_in, col_scale):
    return (w_in * jnp.asarray(col_scale, F32)[None, :]).astype(BF16)


def kernel(x, p, t5_table, w_in_even, w_out_even, lambda_q1, lambda_k1, lambda_q2, lambda_k2, subln_g,
           w_in_odd, w_out_odd, norm_mix_g, norm_mlp_g, w_mlp_up, w_mlp_down, norm_ple_g, w_ple_gate,
           w_ple_proj, final_norm_g):
    b, s, d = x.shape
    depth = p.shape[0]
    assert d == D_MODEL and s % DT == 0 and (b * s) % TM_POST == 0
    assert all(w == DIL_WINDOW * r and w <= DT for w, r in DIL_BRANCHES)
    col = np.arange(3 * D_MODEL)
    qk_scale = HEAD_DIM ** -0.5
    even_q = (col < SB_WIDTH) | ((col >= 3 * SB_WIDTH) & (col < 3 * SB_WIDTH + DIFF_WIDTH))
    even_scale = np.where(even_q, qk_scale * LOG2E, 1.0)
    odd_scale = np.where(col < DIL_HEADS * HEAD_DIM, qk_scale * LOG2E, 1.0)
    h = x.reshape(b * s, d)
    for i in range(depth):
        if i % 2 == 0:
            e = i // 2
            w_in = _scale_cols(w_in_even[e], even_scale)
            qa, ka, va, qb, kb, vb = (w_in[:, lo:lo + SB_WIDTH] for lo in range(0, 6 * SB_WIDTH, SB_WIDTH))
            proj, qa_t, va_t, t_d = _norm_matmul(
                h, norm_mix_g[i], jnp.concatenate([ka, kb], axis=1),
                ((qa.T, TS), (va.T, SB_HALF), (jnp.concatenate([qb, vb], axis=1).T, TD)))
            proj = proj.reshape(b, s, -1)
            qa_t = qa_t.reshape(b, s // TS, SB_WIDTH, TS)
            va_t = va_t.reshape(b, s // SB_HALF, SB_WIDTH, SB_HALF)
            t_d = t_d.reshape(b, s // TD, 2 * DIFF_WIDTH, TD)
            lambda_init = 0.8 - 0.6 * math.exp(-0.3 * i)
            lam_params = jnp.stack([lambda_q1[e], lambda_k1[e], lambda_q2[e], lambda_k2[e]]).astype(F32)
            o_sb = _sb_attention(proj, qa_t, va_t)
            o_d = _diff_attention(proj, t_d, _diff_bias(t5_table), lam_params, subln_g[e].astype(F32),
                                  lambda_init)
            w_out = w_out_even[e].astype(BF16)
            o_list = [o_sb.reshape(b * s, SB_WIDTH), o_d.reshape(b * s, DIFF_WIDTH)]
        else:
            o = i // 2
            (proj,) = _norm_matmul(h, norm_mix_g[i], _scale_cols(w_in_odd[o], odd_scale))
            proj = proj.reshape(b, s, -1)
            o_list = [_dilated_attention(proj, t5_table).reshape(b * s, d)]
            w_out = w_out_odd[o].astype(BF16)
        h = _post(h, o_list, w_out, p[i].reshape(b * s, PLE_DIM), norm_mlp_g[i],
                  w_mlp_up[i].astype(BF16), w_mlp_down[i].astype(BF16), norm_ple_g[i],
                  w_ple_gate[i].astype(BF16), w_ple_proj[i].astype(BF16), final_norm_g,
                  final=(i == depth - 1))
    return h.reshape(b, s, d)
```

```python
import functools
import math

import numpy as np
import jax
import jax.numpy as jnp
from jax import lax
from jax.experimental import pallas as pl
from jax.experimental.pallas import tpu as pltpu

F32 = jnp.float32
BF16 = jnp.bfloat16

D_MODEL = 1024
HEAD_DIM = 64
LANES = 128
SB_HEADS = 8
DIFF_HEADS = 4
DIL_HEADS = 16
SB_WIDTH = SB_HEADS * HEAD_DIM
DIFF_WIDTH = DIFF_HEADS * 2 * HEAD_DIM
DIL_BRANCHES = ((128, 1), (512, 4), (2048, 16))
DIL_WINDOW = 128
NUM_BUCKETS = 32
MAX_DISTANCE = 128
D_FF = 4 * D_MODEL
PLE_DIM = 256
NORM_EPS = 1e-6
SUBLN_EPS = 1e-5
NEG = -1e30
LOG2E = math.log2(math.e)
SB_DEAD_LOG = 88.0

VMEM_LIMIT_BYTES = 58 * 1024 * 1024

TM_PROJ = 512
TM_POST = 1024
FF_CHUNK = 512
TQ = 256
TS = 256
SB_QS = 4
SB_HALF = 128
TD = TM_PROJ
DIFF_SUM_ROWS = 16
DIFF_HPS = 4
DT = 2048
DQ = DIL_WINDOW
DIL_UNROLL = 16
DIL_SKEW = 3


def _cparams(sem):
    return pltpu.CompilerParams(dimension_semantics=sem, vmem_limit_bytes=VMEM_LIMIT_BYTES)


def _rms(x, g, eps):
    return x * lax.rsqrt(jnp.mean(x * x, axis=-1, keepdims=True) + eps) * g


def _dot(a, b):
    return jnp.dot(a, b, preferred_element_type=F32)


def _dot_nt(a, b):
    return lax.dot_general(a, b, (((1,), (1,)), ((), ())), preferred_element_type=F32)


def _norm_matmul_kernel(h_ref, g_ref, w_ref, *rest, slabs):
    hn = _rms(h_ref[...], g_ref[...], NORM_EPS).astype(BF16)
    n_t = len(slabs)
    wt_refs, o_ref, ot_refs = rest[:n_t], rest[n_t], rest[n_t + 1:]
    for wt_ref, ot_ref, width in zip(wt_refs, ot_refs, slabs):
        res = _dot_nt(wt_ref[...], hn).astype(ot_ref.dtype)
        for c in range(TM_PROJ // width):
            ot_ref[c] = res[:, c * width:(c + 1) * width]
    o_ref[...] = _dot(hn, w_ref[...]).astype(o_ref.dtype)


def _norm_matmul(h2d, g, w, wts=()):
    m, k = h2d.shape
    n = w.shape[1]
    const = lambda a: pl.BlockSpec(a.shape, lambda i: (0,) * a.ndim, pipeline_mode=pl.Buffered(1))
    in_specs = [pl.BlockSpec((TM_PROJ, k), lambda i: (i, 0)), pl.BlockSpec((1, k), lambda i: (0, 0)), const(w)]
    in_specs += [const(wt) for wt, _ in wts]
    out_specs = [pl.BlockSpec((TM_PROJ, n), lambda i: (i, 0))]
    out_shape = [jax.ShapeDtypeStruct((m, n), BF16)]
    for wt, width in wts:
        per_step = TM_PROJ // width
        out_specs.append(pl.BlockSpec((per_step, wt.shape[0], width), lambda i: (i, 0, 0)))
        out_shape.append(jax.ShapeDtypeStruct((m // width, wt.shape[0], width), BF16))
    return pl.pallas_call(
        functools.partial(_norm_matmul_kernel, slabs=tuple(width for _, width in wts)),
        grid=(m // TM_PROJ,),
        in_specs=in_specs, out_specs=out_specs, out_shape=out_shape,
        compiler_params=_cparams(("parallel",)),
        name="norm_in_proj",
    )(h2d, g.reshape(1, k), w, *[wt for wt, _ in wts])


def _t5_bucket_np(dist):
    n = np.maximum(dist, 0)
    max_exact = NUM_BUCKETS // 2
    nf = np.maximum(n, 1).astype(np.float32)
    large = max_exact + (np.log(nf / np.float32(max_exact)) / np.float32(math.log(MAX_DISTANCE / max_exact))
                         * np.float32(NUM_BUCKETS - max_exact)).astype(np.int32)
    large = np.minimum(large, NUM_BUCKETS - 1)
    return np.where(n < max_exact, n, large).astype(np.int32)


def _bias_tile(table_cols, dist, valid):
    bucket = _t5_bucket_np(dist)
    out = jnp.zeros((table_cols.shape[1],) + dist.shape, F32)
    for bkt in np.unique(bucket[valid]):
        out = jnp.where(jnp.asarray(bucket == bkt)[None], table_cols[bkt][:, None, None], out)
    return jnp.where(jnp.asarray(valid)[None], out, NEG)


def _sb_kernel(qt_ref, k_ref, vt_ref, o_ref, acc_ref, c_ref):
    i = pl.program_id(2)
    row = lax.broadcasted_iota(jnp.int32, (LANES, 1), 0)
    head_rows = (row < HEAD_DIM, row >= HEAD_DIM)
    qts = [tuple(jnp.where(hr, qt_ref[0, a], jnp.zeros((LANES, TS), BF16)) for hr in head_rows)
           for a in range(SB_QS)]
    kk = lax.broadcasted_iota(jnp.int32, (TS, TS), 0)
    qq = lax.broadcasted_iota(jnp.int32, (TS, TS), 1)
    strict = kk < qq
    per_half = TS // SB_HALF

    def tri2(n):
        tri = (lax.broadcasted_iota(jnp.int32, (n, n), 0) <= lax.broadcasted_iota(jnp.int32, (n, n), 1)).astype(BF16)
        return jnp.concatenate([tri, tri], axis=1)

    tri2s = {n: tri2(n) for n in (TS, SB_HALF)}

    def tiles(work, c_in):
        items = [(n, hh) for n in range(len(work)) for hh in range(2)]
        ks = [k_ref[0, pl.ds(pl.multiple_of(k0, SB_HALF), nk), :] for _, k0, nk, _, _ in work]
        z = {(n, hh): _dot(ks[n], qts[work[n][0]][hh]) for n, hh in items}
        parts = {}
        for it in items:
            keep = work[it[0]][4]
            if keep is not None:
                z[it] = jnp.where(keep, z[it], NEG)
            lk = -(jnp.maximum(z[it], 0.0) + jnp.log2(1.0 + jnp.exp2(-jnp.abs(z[it]))))
            hi = lk.astype(BF16)
            parts[it] = jnp.concatenate([hi, (lk - hi.astype(F32)).astype(BF16)], axis=0)
        incl = {it: _dot(tri2s[work[it[0]][2]], parts[it]) for it in items}
        c = dict(c_in)
        w = {}
        for n, hh in items:
            a = work[n][0]
            w[(n, hh)] = jnp.exp2(z[(n, hh)] + incl[(n, hh)] + c[(a, hh)]).astype(BF16)
            c[(a, hh)] = c[(a, hh)] + incl[(n, hh)][0:1, :]
        per_head = {}
        for n, hh in items:
            a = work[n][0]
            part = _dot(work[n][3](), w[(n, hh)])
            per_head[(a, hh)] = part if (a, hh) not in per_head else per_head[(a, hh)] + part
        upd = {a: jnp.where(head_rows[0], per_head[(a, 0)], per_head[(a, 1)]) for a, _ in per_head}
        return upd, c

    def half_tile(a, e, keep=None):
        return (a, e * SB_HALF, SB_HALF, lambda: vt_ref[0, e], keep)

    def diag_vt(g):
        return jnp.concatenate([vt_ref[0, g * per_half + c] for c in range(per_half)], axis=1)

    zero_c = jnp.zeros((1, TS), F32)
    work = []
    for a in range(SB_QS):
        g = i * SB_QS + a
        work.append((a, g * TS, TS, functools.partial(diag_vt, g), strict))
        if a > 0:
            work.append(half_tile(a, g * per_half - 1))
        else:
            exists = lax.broadcasted_iota(jnp.int32, (SB_HALF, TS), 0) < jnp.where(g >= 1, SB_HALF, 0)
            work.append(half_tile(a, jnp.maximum(g * per_half - 1, 0), exists))
    upd, c1 = tiles(work, {(a, hh): zero_c for a in range(SB_QS) for hh in range(2)})
    for a in range(SB_QS):
        acc_ref[a] = upd[a]
        for hh in range(2):
            c_ref[a, hh] = c1[(a, hh)]

    for a in range(SB_QS):
        def cond(state):
            e, alive = state
            return jnp.logical_and(e >= 0, alive > -SB_DEAD_LOG * LOG2E)

        def body(state, a=a):
            e, _ = state
            upd, c_new = tiles([half_tile(a, e)], {(a, hh): c_ref[a, hh] for hh in range(2)})
            acc_ref[a] += upd[a]
            for hh in range(2):
                c_ref[a, hh] = c_new[(a, hh)]
            return e - 1, jnp.maximum(jnp.max(c_new[(a, 0)]), jnp.max(c_new[(a, 1)]))

        alive1 = jnp.maximum(jnp.max(c1[(a, 0)]), jnp.max(c1[(a, 1)]))
        lax.while_loop(cond, body, ((i * SB_QS + a) * per_half - 2, alive1))
        o_ref[0, a * TS:(a + 1) * TS, :] = jnp.transpose(acc_ref[a]).astype(o_ref.dtype)


def _sb_attention(proj3, q_t, v_t):
    b, s, _ = proj3.shape
    pairs = SB_WIDTH // LANES
    return pl.pallas_call(
        _sb_kernel,
        grid=(b, pairs, s // (SB_QS * TS)),
        in_specs=[pl.BlockSpec((1, SB_QS, LANES, TS), lambda bb, hp, i: (bb, i, hp, 0)),
                  pl.BlockSpec((1, s, LANES), lambda bb, hp, i: (bb, 0, hp)),
                  pl.BlockSpec((1, s // SB_HALF, LANES, SB_HALF), lambda bb, hp, i: (bb, 0, hp, 0))],
        out_specs=pl.BlockSpec((1, SB_QS * TS, LANES), lambda bb, hp, i: (bb, i, hp)),
        out_shape=jax.ShapeDtypeStruct((b, s, SB_WIDTH), BF16),
        scratch_shapes=[pltpu.VMEM((SB_QS, LANES, TS), F32), pltpu.VMEM((SB_QS, 2, 1, TS), F32)],
        compiler_params=_cparams(("parallel", "parallel", "arbitrary")),
        name="sb_attention",
    )(q_t, proj3, v_t)


def _diff_kernel(qt_ref, k_ref, vt_ref, bias_ref, lam_ref, g_ref, o_ref, m_ref, acc_ref, sa_ref, sb_ref,
                 *, lambda_init):
    i = pl.program_id(2)
    chains = [(hd, mm) for hd in range(DIFF_HPS) for mm in range(2)]
    row = lax.broadcasted_iota(jnp.int32, (LANES, 1), 0)
    map_rows = (row < HEAD_DIM, row >= HEAD_DIM)
    qts = {}
    for hd in range(DIFF_HPS):
        qt = qt_ref[0, 0, hd * LANES:(hd + 1) * LANES, :]
        for mm in range(2):
            qts[(hd, mm)] = jnp.where(map_rows[mm], qt, jnp.zeros_like(qt))

    m_ref[...] = jnp.full_like(m_ref, NEG)
    acc_ref[...] = jnp.zeros_like(acc_ref)

    def stage(s_ref, j):
        ks = k_ref[0, pl.ds(pl.multiple_of(j * TD, TD), TD), :]
        for hd, mm in chains:
            s_ref[hd, mm] = _dot(ks[:, hd * LANES:(hd + 1) * LANES], qts[(hd, mm)])

    def add_bias(s_ref, hd, mm, near):
        nb = TD // MAX_DISTANCE
        blk = lambda kb, qb: (hd, mm, slice(kb * MAX_DISTANCE, (kb + 1) * MAX_DISTANCE),
                              slice(qb * MAX_DISTANCE, (qb + 1) * MAX_DISTANCE))
        if near == 1:
            s_ref[blk(nb - 1, 0)] += bias_ref[hd, mm, 1]
            return
        for kb in range(nb):
            for qb in range(nb):
                if kb > qb:
                    s_ref[blk(kb, qb)] = jnp.full((MAX_DISTANCE, MAX_DISTANCE), NEG, F32)
                elif kb == qb:
                    s_ref[blk(kb, qb)] += bias_ref[hd, mm, 0]
                elif kb == qb - 1:
                    s_ref[blk(kb, qb)] += bias_ref[hd, mm, 1]

    def absorb(j, s_ref, near=None, offset=None, nxt=None):
        s, alpha = {}, {}
        for hd, mm in chains:
            if near is not None:
                add_bias(s_ref, hd, mm, near)
            sm = s_ref[hd, mm]
            if offset is not None:
                sm = sm + offset
            m_old = m_ref[hd, mm]
            m_new = jnp.maximum(m_old, jnp.max(sm, axis=0, keepdims=True))
            m_ref[hd, mm] = m_new
            alpha[(hd, mm)] = jnp.exp2(m_old - m_new)
            s[(hd, mm)] = sm - m_new
        if nxt is not None:
            stage(*nxt)
        p = {ch: jnp.exp2(s[ch]).astype(BF16) for ch in chains}
        vt = vt_ref[0, j]
        ones = jnp.ones((DIFF_SUM_ROWS, TD), BF16)
        for hd in range(DIFF_HPS):
            vts = jnp.concatenate([vt[hd * LANES:(hd + 1) * LANES], ones], axis=0)
            for mm in range(2):
                acc_ref[hd, mm] = alpha[(hd, mm)] * acc_ref[hd, mm] + _dot(vts, p[(hd, mm)])

    stage(sa_ref, i)
    absorb(i, sa_ref, near=0, nxt=(sb_ref, jnp.maximum(i - 1, 0)))
    absorb(jnp.maximum(i - 1, 0), sb_ref, near=1, offset=jnp.where(i >= 1, 0.0, NEG),
           nxt=(sa_ref, jnp.maximum(i - 2, 0)))

    @pl.when(i >= 2)
    def _():
        def body(u, carry):
            ja = i - 2 - 2 * u
            absorb(ja, sa_ref, nxt=(sb_ref, ja - 1))
            absorb(ja - 1, sb_ref, nxt=(sa_ref, jnp.maximum(ja - 2, 0)))
            return carry

        lax.fori_loop(0, (i - 1) // 2, body, 0)

        @pl.when((i - 1) % 2 == 1)
        def _():
            absorb(0, sa_ref)

    lam = (jnp.exp(jnp.sum(lam_ref[0:1, :] * lam_ref[1:2, :], axis=1, keepdims=True))
           - jnp.exp(jnp.sum(lam_ref[2:3, :] * lam_ref[3:4, :], axis=1, keepdims=True)) + lambda_init)
    for hd in range(DIFF_HPS):
        l0, l1 = acc_ref[hd, 0, LANES:LANES + 1, :], acc_ref[hd, 1, LANES:LANES + 1, :]
        ot = acc_ref[hd, 0, 0:LANES, :] / l0 - lam * (acc_ref[hd, 1, 0:LANES, :] / l1)
        o = jnp.transpose(ot)
        o_ref[0, :, hd * LANES:(hd + 1) * LANES] = (
            _rms(o, g_ref[...], SUBLN_EPS) * (1.0 - lambda_init)).astype(o_ref.dtype)


def _diff_attention(proj3, proj_t, bias, lam_params, subln_g, lambda_init):
    b, s, _ = proj3.shape
    width = DIFF_HPS * LANES
    groups = DIFF_WIDTH // width
    nt = s // TD
    return pl.pallas_call(
        functools.partial(_diff_kernel, lambda_init=lambda_init),
        grid=(b, groups, nt),
        in_specs=[pl.BlockSpec((1, 1, width, TD), lambda bb, h, i: (bb, i, h, 0)),
                  pl.BlockSpec((1, s, width), lambda bb, h, i: (bb, 0, groups + h), pipeline_mode=pl.Buffered(1)),
                  pl.BlockSpec((1, nt, width, TD), lambda bb, h, i: (bb, 0, groups + h, 0),
                               pipeline_mode=pl.Buffered(1)),
                  pl.BlockSpec((DIFF_HPS, 2, 2, MAX_DISTANCE, MAX_DISTANCE), lambda bb, h, i: (h, 0, 0, 0, 0)),
                  pl.BlockSpec((4, HEAD_DIM), lambda bb, h, i: (0, 0)),
                  pl.BlockSpec((1, LANES), lambda bb, h, i: (0, 0))],
        out_specs=pl.BlockSpec((1, TD, width), lambda bb, h, i: (bb, i, h)),
        out_shape=jax.ShapeDtypeStruct((b, s, DIFF_WIDTH), BF16),
        scratch_shapes=[pltpu.VMEM((DIFF_HPS, 2, 1, TD), F32),
                        pltpu.VMEM((DIFF_HPS, 2, LANES + DIFF_SUM_ROWS, TD), F32),
                        pltpu.VMEM((DIFF_HPS, 2, TD, TD), F32), pltpu.VMEM((DIFF_HPS, 2, TD, TD), F32)],
        compiler_params=_cparams(("parallel", "parallel", "arbitrary")),
        name="diff_attention",
    )(proj_t, proj3, proj_t, bias, lam_params, subln_g.reshape(1, LANES))


def _diff_bias(t5_table):
    cols = t5_table.astype(F32)[:, SB_HEADS:]
    cols = (cols - cols[NUM_BUCKETS - 1:NUM_BUCKETS, :]) * LOG2E
    ki = np.arange(MAX_DISTANCE)[:, None]
    qi = np.arange(MAX_DISTANCE)[None, :]
    d0 = qi - ki
    diag = _bias_tile(cols, d0, d0 >= 0)
    prev = _bias_tile(cols, d0 + MAX_DISTANCE, np.ones_like(d0, bool))
    return jnp.stack([diag, prev], axis=1).reshape(DIFF_HEADS, 2, 2, MAX_DISTANCE, MAX_DISTANCE)


def _dil_kernel(q_ref, kp_ref, kc_ref, vp_ref, vc_ref, bias_ref, o_ref, qf, kf, vf, acc_s, m_s, l_s):
    t_idx = pl.program_id(2)
    qf[...] = q_ref[0].astype(F32)
    kf[0:DT, :] = kp_ref[0].astype(F32)
    kf[DT:2 * DT, :] = kc_ref[0].astype(F32)
    vf[0:DT, :] = vp_ref[0].astype(F32)
    vf[DT:2 * DT, :] = vc_ref[0].astype(F32)
    lane = lax.broadcasted_iota(jnp.int32, (1, LANES), 1)
    head0 = lane < HEAD_DIM
    heads = (head0, jnp.logical_not(head0))

    for g, (_, r) in enumerate(DIL_BRANCHES):
        nblk = DT // (r * DQ)
        shift = nblk.bit_length() - 1

        def group(t0, carry, g=g, r=r, nblk=nblk, shift=shift):
            rows, k0s, pre, s, p, mx = {}, {}, {}, {}, {}, {}

            def scores(u):
                t = t0 * DIL_UNROLL + u
                c = lax.shift_right_logical(t, shift)
                n = jnp.bitwise_and(t, nblk - 1)
                q0 = c + n * (DQ * r)
                k0s[u] = DT + q0 - DQ * r
                rows[u] = pl.ds(q0, DQ, stride=r)
                qg = qf[rows[u], :].astype(BF16)
                kg = kf[pl.ds(k0s[u], 2 * DQ, stride=r), :].astype(BF16)
                for hh in range(2):
                    s[(u, hh)] = _dot_nt(jnp.where(heads[hh], qg, jnp.zeros_like(qg)), kg)
                pre[u] = jnp.where(jnp.logical_and(t_idx == 0, n == 0), 1, 0)

            def softmax(u):
                for hh in range(2):
                    sb = s.pop((u, hh)) + bias_ref[0, hh, g, pre[u]]
                    m = jnp.max(sb, axis=1, keepdims=True)
                    p[(u, hh)] = jnp.exp2(sb - m).astype(BF16)
                    mx[(u, hh)] = jnp.broadcast_to(m, (DQ, LANES))

            def values(u):
                vg = vf[pl.ds(k0s[u], 2 * DQ, stride=r), :].astype(BF16)
                res = [_dot(p.pop((u, hh)), jnp.where(heads[hh], vg, jnp.ones_like(vg))) for hh in range(2)]
                acc_s[g, rows[u], :] = jnp.where(head0, res[0], res[1])
                m_s[g, rows[u], :] = jnp.where(head0, mx.pop((u, 0)), mx.pop((u, 1)))
                l_s[g, rows[u], :] = pltpu.roll(jnp.where(head0, res[1], res[0]), HEAD_DIM, 1)

            for step in range(DIL_UNROLL + 2 * DIL_SKEW):
                if step < DIL_UNROLL:
                    scores(step)
                if 0 <= step - DIL_SKEW < DIL_UNROLL:
                    softmax(step - DIL_SKEW)
                if step >= 2 * DIL_SKEW:
                    values(step - 2 * DIL_SKEW)
            return carry

        lax.fori_loop(0, r * nblk // DIL_UNROLL, group, 0)

    def merge(ci, carry):
        rows = pl.ds(pl.multiple_of(ci * TQ, TQ), TQ)
        ms = [m_s[g, rows, :] for g in range(len(DIL_BRANCHES))]
        m = functools.reduce(jnp.maximum, ms)
        num = jnp.zeros((TQ, LANES), F32)
        den = jnp.zeros((TQ, LANES), F32)
        for g in range(len(DIL_BRANCHES)):
            wgt = jnp.exp2(ms[g] - m)
            num = num + wgt * acc_s[g, rows, :]
            den = den + wgt * l_s[g, rows, :]
        o_ref[0, rows, :] = (num / den).astype(o_ref.dtype)
        return carry

    lax.fori_loop(0, DT // TQ, merge, 0)


def _dil_bias(t5_table):
    qi = np.arange(DQ)[:, None]
    ki = np.arange(2 * DQ)[None, :] - DQ
    steps = qi - ki
    valid = (steps >= 0) & (steps <= DIL_WINDOW)
    table = t5_table.astype(F32) * LOG2E
    tiles = [jnp.stack([_bias_tile(table, steps * r, v) for v in (valid, valid & (ki >= 0))], axis=1)
             for _, r in DIL_BRANCHES]
    return jnp.stack(tiles, axis=1).reshape(DIL_HEADS // 2, 2, len(DIL_BRANCHES), 2, DQ, 2 * DQ)


def _dilated_attention(proj3, t5_table):
    b, s, _ = proj3.shape
    pairs = DIL_HEADS // 2
    bias = _dil_bias(t5_table)
    cur = lambda off: (lambda bb, hp, t: (bb, t, off * pairs + hp))
    prev = lambda off: (lambda bb, hp, t: (bb, jnp.maximum(t - 1, 0), off * pairs + hp))
    tile = lambda imap: pl.BlockSpec((1, DT, LANES), imap)
    slab = lambda rows: pltpu.VMEM((rows, LANES), F32)
    per_branch = pltpu.VMEM((len(DIL_BRANCHES), DT, LANES), F32)
    return pl.pallas_call(
        _dil_kernel,
        grid=(b, pairs, s // DT),
        in_specs=[tile(cur(0)), tile(prev(1)), tile(cur(1)), tile(prev(2)), tile(cur(2)),
                  pl.BlockSpec((1, 2, len(DIL_BRANCHES), 2, DQ, 2 * DQ), lambda bb, hp, t: (hp, 0, 0, 0, 0, 0))],
        out_specs=tile(cur(0)),
        out_shape=jax.ShapeDtypeStruct((b, s, D_MODEL), BF16),
        scratch_shapes=[slab(DT), slab(2 * DT), slab(2 * DT), per_branch, per_branch, per_branch],
        compiler_params=_cparams(("parallel", "parallel", "arbitrary")),
        name="dilated_attention",
    )(proj3, proj3, proj3, proj3, proj3, bias)


def _post_kernel(*refs, n_o, final):
    h_ref = refs[0]
    o_refs = refs[1:1 + n_o]
    (wo_ref, p_ref, g_mlp_ref, w_up_ref, w_down_ref, g_ple_ref, w_gate_ref, w_proj_ref, g_fin_ref,
     out_ref, acc_ref) = refs[1 + n_o:]
    o = o_refs[0][...] if n_o == 1 else jnp.concatenate([r[...] for r in o_refs], axis=1)
    h = h_ref[...] + _dot(o, wo_ref[...])
    hn = _rms(h, g_mlp_ref[...], NORM_EPS).astype(BF16)
    acc_ref[...] = h
    for c in range(D_FF // FF_CHUNK):
        u = jnp.maximum(_dot(hn, w_up_ref[:, c * FF_CHUNK:(c + 1) * FF_CHUNK]), 0.0)
        acc_ref[...] += _dot((u * u).astype(BF16), w_down_ref[c * FF_CHUNK:(c + 1) * FF_CHUNK, :])
    h = acc_ref[...]
    gate = jax.nn.sigmoid(_dot(_rms(h, g_ple_ref[...], NORM_EPS).astype(BF16), w_gate_ref[...]))
    h = h + _dot(p_ref[...].astype(BF16), w_proj_ref[...]) * gate
    if final:
        h = _rms(h, g_fin_ref[...], NORM_EPS)
    out_ref[...] = h


def _post(h2d, o_list, w_out, p2d, g_mlp, w_up, w_down, g_ple, w_gate, w_proj, g_fin, final):
    m, d = h2d.shape
    n_o = len(o_list)
    row = lambda width: pl.BlockSpec((TM_POST, width), lambda i: (i, 0))
    const = lambda a: pl.BlockSpec(a.shape, lambda i: (0,) * a.ndim, pipeline_mode=pl.Buffered(1))
    vec = lambda g: g.reshape(1, d).astype(F32)
    weights = [w_out, vec(g_mlp), w_up, w_down, vec(g_ple), w_gate, w_proj, vec(g_fin)]
    return pl.pallas_call(
        functools.partial(_post_kernel, n_o=n_o, final=final),
        grid=(m // TM_POST,),
        in_specs=([row(d)] + [row(o.shape[1]) for o in o_list] + [const(weights[0])]
                  + [row(p2d.shape[1])] + [const(w) for w in weights[1:]]),
        out_specs=row(d),
        out_shape=jax.ShapeDtypeStruct((m, d), F32),
        scratch_shapes=[pltpu.VMEM((TM_POST, d), F32)],
        compiler_params=_cparams(("parallel",)),
        name="out_mlp_ple",
    )(h2d, *o_list, weights[0], p2d, *weights[1:])


def _scale_cols(w_in, col_scale):
    return (w_in * jnp.asarray(col_scale, F32)[None, :]).astype(BF16)


def kernel(x, p, t5_table, w_in_even, w_out_even, lambda_q1, lambda_k1, lambda_q2, lambda_k2, subln_g,
           w_in_odd, w_out_odd, norm_mix_g, norm_mlp_g, w_mlp_up, w_mlp_down, norm_ple_g, w_ple_gate,
           w_ple_proj, final_norm_g):
    b, s, d = x.shape
    depth = p.shape[0]
    assert d == D_MODEL and s % DT == 0 and (b * s) % TM_POST == 0
    assert all(w == DIL_WINDOW * r and w <= DT for w, r in DIL_BRANCHES)
    col = np.arange(3 * D_MODEL)
    qk_scale = HEAD_DIM ** -0.5
    even_q = (col < SB_WIDTH) | ((col >= 3 * SB_WIDTH) & (col < 3 * SB_WIDTH + DIFF_WIDTH))
    even_scale = np.where(even_q, qk_scale * LOG2E, 1.0)
    odd_scale = np.where(col < DIL_HEADS * HEAD_DIM, qk_scale * LOG2E, 1.0)
    h = x.reshape(b * s, d)
    for i in range(depth):
        if i % 2 == 0:
            e = i // 2
            w_in = _scale_cols(w_in_even[e], even_scale)
            qa, ka, va, qb, kb, vb = (w_in[:, lo:lo + SB_WIDTH] for lo in range(0, 6 * SB_WIDTH, SB_WIDTH))
            proj, qa_t, va_t, t_d = _norm_matmul(
                h, norm_mix_g[i], jnp.concatenate([ka, kb], axis=1),
                ((qa.T, TS), (va.T, SB_HALF), (jnp.concatenate([qb, vb], axis=1).T, TD)))
            proj = proj.reshape(b, s, -1)
            qa_t = qa_t.reshape(b, s // TS, SB_WIDTH, TS)
            va_t = va_t.reshape(b, s // SB_HALF, SB_WIDTH, SB_HALF)
            t_d = t_d.reshape(b, s // TD, 2 * DIFF_WIDTH, TD)
            lambda_init = 0.8 - 0.6 * math.exp(-0.3 * i)
            lam_params = jnp.stack([lambda_q1[e], lambda_k1[e], lambda_q2[e], lambda_k2[e]]).astype(F32)
            o_sb = _sb_attention(proj, qa_t, va_t)
            o_d = _diff_attention(proj, t_d, _diff_bias(t5_table), lam_params, subln_g[e].astype(F32),
                                  lambda_init)
            w_out = w_out_even[e].astype(BF16)
            o_list = [o_sb.reshape(b * s, SB_WIDTH), o_d.reshape(b * s, DIFF_WIDTH)]
        else:
            o = i // 2
            (proj,) = _norm_matmul(h, norm_mix_g[i], _scale_cols(w_in_odd[o], odd_scale))
            proj = proj.reshape(b, s, -1)
            o_list = [_dilated_attention(proj, t5_table).reshape(b * s, d)]
            w_out = w_out_odd[o].astype(BF16)
        h = _post(h, o_list, w_out, p[i].reshape(b * s, PLE_DIM), norm_mlp_g[i],
                  w_mlp_up[i].astype(BF16), w_mlp_down[i].astype(BF16), norm_ple_g[i],
                  w_ple_gate[i].astype(BF16), w_ple_proj[i].astype(BF16), final_norm_g,
                  final=(i == depth - 1))
    return h.reshape(b, s, d)
```

```python
import functools
import math

import numpy as np
import jax
import jax.numpy as jnp
from jax import lax
from jax.experimental import pallas as pl
from jax.experimental.pallas import tpu as pltpu

F32 = jnp.float32
BF16 = jnp.bfloat16

D_MODEL = 1024
HEAD_DIM = 64
LANES = 128
SB_HEADS = 8
DIFF_HEADS = 4
DIL_HEADS = 16
SB_WIDTH = SB_HEADS * HEAD_DIM
DIFF_WIDTH = DIFF_HEADS * 2 * HEAD_DIM
DIL_BRANCHES = ((128, 1), (512, 4), (2048, 16))
DIL_WINDOW = 128
NUM_BUCKETS = 32
MAX_DISTANCE = 128
D_FF = 4 * D_MODEL
PLE_DIM = 256
NORM_EPS = 1e-6
SUBLN_EPS = 1e-5
NEG = -1e30
LOG2E = math.log2(math.e)
SB_DEAD_LOG = 88.0

VMEM_LIMIT_BYTES = 58 * 1024 * 1024

TM_PROJ = 1024
TM_POST = 1024
FF_CHUNK = 512
TQ = 256
TS = 256
SB_QS = 4
SB_HALF = 128
TD = 512
DIFF_SUM_ROWS = 16
DIFF_HPS = 4
DT = 2048
DQ = DIL_WINDOW
DIL_UNROLL = 16
DIL_SKEW = 3


def _cparams(sem):
    return pltpu.CompilerParams(dimension_semantics=sem, vmem_limit_bytes=VMEM_LIMIT_BYTES)


def _rms(x, g, eps):
    return x * lax.rsqrt(jnp.mean(x * x, axis=-1, keepdims=True) + eps) * g


def _dot(a, b):
    return jnp.dot(a, b, preferred_element_type=F32)


def _dot_nt(a, b):
    return lax.dot_general(a, b, (((1,), (1,)), ((), ())), preferred_element_type=F32)


def _norm_matmul_kernel(h_ref, g_ref, w_ref, *rest, slabs):
    hn = _rms(h_ref[...], g_ref[...], NORM_EPS).astype(BF16)
    n_t = len(slabs)
    wt_refs, o_ref, ot_refs = rest[:n_t], rest[n_t], rest[n_t + 1:]
    for wt_ref, ot_ref, width in zip(wt_refs, ot_refs, slabs):
        res = _dot_nt(wt_ref[...], hn).astype(ot_ref.dtype)
        for c in range(TM_PROJ // width):
            ot_ref[c] = res[:, c * width:(c + 1) * width]
    o_ref[...] = _dot(hn, w_ref[...]).astype(o_ref.dtype)


def _norm_matmul(h2d, g, w, wts=()):
    m, k = h2d.shape
    n = w.shape[1]
    const = lambda a: pl.BlockSpec(a.shape, lambda i: (0,) * a.ndim, pipeline_mode=pl.Buffered(1))
    in_specs = [pl.BlockSpec((TM_PROJ, k), lambda i: (i, 0)), pl.BlockSpec((1, k), lambda i: (0, 0)), const(w)]
    in_specs += [const(wt) for wt, _ in wts]
    out_specs = [pl.BlockSpec((TM_PROJ, n), lambda i: (i, 0))]
    out_shape = [jax.ShapeDtypeStruct((m, n), BF16)]
    for wt, width in wts:
        per_step = TM_PROJ // width
        out_specs.append(pl.BlockSpec((per_step, wt.shape[0], width), lambda i: (i, 0, 0)))
        out_shape.append(jax.ShapeDtypeStruct((m // width, wt.shape[0], width), BF16))
    return pl.pallas_call(
        functools.partial(_norm_matmul_kernel, slabs=tuple(width for _, width in wts)),
        grid=(m // TM_PROJ,),
        in_specs=in_specs, out_specs=out_specs, out_shape=out_shape,
        compiler_params=_cparams(("parallel",)),
        name="norm_in_proj",
    )(h2d, g.reshape(1, k), w, *[wt for wt, _ in wts])


def _t5_bucket_np(dist):
    n = np.maximum(dist, 0)
    max_exact = NUM_BUCKETS // 2
    nf = np.maximum(n, 1).astype(np.float32)
    large = max_exact + (np.log(nf / np.float32(max_exact)) / np.float32(math.log(MAX_DISTANCE / max_exact))
                         * np.float32(NUM_BUCKETS - max_exact)).astype(np.int32)
    large = np.minimum(large, NUM_BUCKETS - 1)
    return np.where(n < max_exact, n, large).astype(np.int32)


def _bias_tile(table_cols, dist, valid):
    bucket = _t5_bucket_np(dist)
    out = jnp.zeros((table_cols.shape[1],) + dist.shape, F32)
    for bkt in np.unique(bucket[valid]):
        out = jnp.where(jnp.asarray(bucket == bkt)[None], table_cols[bkt][:, None, None], out)
    return jnp.where(jnp.asarray(valid)[None], out, NEG)


def _sb_kernel(qt_ref, k_ref, vt_ref, o_ref, acc_ref, c_ref):
    i = pl.program_id(2)
    row = lax.broadcasted_iota(jnp.int32, (LANES, 1), 0)
    head_rows = (row < HEAD_DIM, row >= HEAD_DIM)
    qts = [tuple(jnp.where(hr, qt_ref[0, a], jnp.zeros((LANES, TS), BF16)) for hr in head_rows)
           for a in range(SB_QS)]
    kk = lax.broadcasted_iota(jnp.int32, (TS, TS), 0)
    qq = lax.broadcasted_iota(jnp.int32, (TS, TS), 1)
    strict = kk < qq
    per_half = TS // SB_HALF

    def tri2(n):
        tri = (lax.broadcasted_iota(jnp.int32, (n, n), 0) <= lax.broadcasted_iota(jnp.int32, (n, n), 1)).astype(BF16)
        return jnp.concatenate([tri, tri], axis=1)

    tri2s = {n: tri2(n) for n in (TS, SB_HALF)}

    def tiles(work, c_in):
        items = [(n, hh) for n in range(len(work)) for hh in range(2)]
        ks = [k_ref[0, pl.ds(pl.multiple_of(k0, SB_HALF), nk), :] for _, k0, nk, _, _ in work]
        z = {(n, hh): _dot(ks[n], qts[work[n][0]][hh]) for n, hh in items}
        parts = {}
        for it in items:
            keep = work[it[0]][4]
            if keep is not None:
                z[it] = jnp.where(keep, z[it], NEG)
            lk = -(jnp.maximum(z[it], 0.0) + jnp.log2(1.0 + jnp.exp2(-jnp.abs(z[it]))))
            hi = lk.astype(BF16)
            parts[it] = jnp.concatenate([hi, (lk - hi.astype(F32)).astype(BF16)], axis=0)
        incl = {it: _dot(tri2s[work[it[0]][2]], parts[it]) for it in items}
        c = dict(c_in)
        w = {}
        for n, hh in items:
            a = work[n][0]
            w[(n, hh)] = jnp.exp2(z[(n, hh)] + incl[(n, hh)] + c[(a, hh)]).astype(BF16)
            c[(a, hh)] = c[(a, hh)] + incl[(n, hh)][0:1, :]
        per_head = {}
        for n, hh in items:
            a = work[n][0]
            part = _dot(work[n][3](), w[(n, hh)])
            per_head[(a, hh)] = part if (a, hh) not in per_head else per_head[(a, hh)] + part
        upd = {a: jnp.where(head_rows[0], per_head[(a, 0)], per_head[(a, 1)]) for a, _ in per_head}
        return upd, c

    def half_tile(a, e, keep=None):
        return (a, e * SB_HALF, SB_HALF, lambda: vt_ref[0, e], keep)

    def diag_vt(g):
        return jnp.concatenate([vt_ref[0, g * per_half + c] for c in range(per_half)], axis=1)

    zero_c = jnp.zeros((1, TS), F32)
    work = []
    for a in range(SB_QS):
        g = i * SB_QS + a
        work.append((a, g * TS, TS, functools.partial(diag_vt, g), strict))
        if a > 0:
            work.append(half_tile(a, g * per_half - 1))
        else:
            exists = lax.broadcasted_iota(jnp.int32, (SB_HALF, TS), 0) < jnp.where(g >= 1, SB_HALF, 0)
            work.append(half_tile(a, jnp.maximum(g * per_half - 1, 0), exists))
    upd, c1 = tiles(work, {(a, hh): zero_c for a in range(SB_QS) for hh in range(2)})
    for a in range(SB_QS):
        acc_ref[a] = upd[a]
        for hh in range(2):
            c_ref[a, hh] = c1[(a, hh)]

    for a in range(SB_QS):
        def cond(state):
            e, alive = state
            return jnp.logical_and(e >= 0, alive > -SB_DEAD_LOG * LOG2E)

        def body(state, a=a):
            e, _ = state
            upd, c_new = tiles([half_tile(a, e)], {(a, hh): c_ref[a, hh] for hh in range(2)})
            acc_ref[a] += upd[a]
            for hh in range(2):
                c_ref[a, hh] = c_new[(a, hh)]
            return e - 1, jnp.maximum(jnp.max(c_new[(a, 0)]), jnp.max(c_new[(a, 1)]))

        alive1 = jnp.maximum(jnp.max(c1[(a, 0)]), jnp.max(c1[(a, 1)]))
        lax.while_loop(cond, body, ((i * SB_QS + a) * per_half - 2, alive1))
        o_ref[0, a * TS:(a + 1) * TS, :] = jnp.transpose(acc_ref[a]).astype(o_ref.dtype)


def _sb_attention(proj3, q_t, v_t):
    b, s, _ = proj3.shape
    pairs = SB_WIDTH // LANES
    return pl.pallas_call(
        _sb_kernel,
        grid=(b, pairs, s // (SB_QS * TS)),
        in_specs=[pl.BlockSpec((1, SB_QS, LANES, TS), lambda bb, hp, i: (bb, i, hp, 0)),
                  pl.BlockSpec((1, s, LANES), lambda bb, hp, i: (bb, 0, hp)),
                  pl.BlockSpec((1, s // SB_HALF, LANES, SB_HALF), lambda bb, hp, i: (bb, 0, hp, 0))],
        out_specs=pl.BlockSpec((1, SB_QS * TS, LANES), lambda bb, hp, i: (bb, i, hp)),
        out_shape=jax.ShapeDtypeStruct((b, s, SB_WIDTH), BF16),
        scratch_shapes=[pltpu.VMEM((SB_QS, LANES, TS), F32), pltpu.VMEM((SB_QS, 2, 1, TS), F32)],
        compiler_params=_cparams(("parallel", "parallel", "arbitrary")),
        name="sb_attention",
    )(q_t, proj3, v_t)


def _diff_kernel(qt_ref, k_ref, vt_ref, bias_ref, lam_ref, g_ref, o_ref, m_ref, acc_ref, sa_ref, sb_ref,
                 *, lambda_init):
    i = pl.program_id(2)
    chains = [(hd, mm) for hd in range(DIFF_HPS) for mm in range(2)]
    row = lax.broadcasted_iota(jnp.int32, (LANES, 1), 0)
    map_rows = (row < HEAD_DIM, row >= HEAD_DIM)
    qts = {}
    for hd in range(DIFF_HPS):
        qt = qt_ref[0, 0, hd * LANES:(hd + 1) * LANES, :]
        for mm in range(2):
            qts[(hd, mm)] = jnp.where(map_rows[mm], qt, jnp.zeros_like(qt))

    m_ref[...] = jnp.full_like(m_ref, NEG)
    acc_ref[...] = jnp.zeros_like(acc_ref)

    def stage(s_ref, j):
        ks = k_ref[0, pl.ds(pl.multiple_of(j * TD, TD), TD), :]
        for hd, mm in chains:
            s_ref[hd, mm] = _dot(ks[:, hd * LANES:(hd + 1) * LANES], qts[(hd, mm)])

    def add_bias(s_ref, hd, mm, near):
        nb = TD // MAX_DISTANCE
        blk = lambda kb, qb: (hd, mm, slice(kb * MAX_DISTANCE, (kb + 1) * MAX_DISTANCE),
                              slice(qb * MAX_DISTANCE, (qb + 1) * MAX_DISTANCE))
        if near == 1:
            s_ref[blk(nb - 1, 0)] += bias_ref[hd, mm, 1]
            return
        for kb in range(nb):
            for qb in range(nb):
                if kb > qb:
                    s_ref[blk(kb, qb)] = jnp.full((MAX_DISTANCE, MAX_DISTANCE), NEG, F32)
                elif kb == qb:
                    s_ref[blk(kb, qb)] += bias_ref[hd, mm, 0]
                elif kb == qb - 1:
                    s_ref[blk(kb, qb)] += bias_ref[hd, mm, 1]

    def absorb(j, s_ref, near=None, offset=None, nxt=None):
        s, alpha = {}, {}
        for hd, mm in chains:
            if near is not None:
                add_bias(s_ref, hd, mm, near)
            sm = s_ref[hd, mm]
            if offset is not None:
                sm = sm + offset
            m_old = m_ref[hd, mm]
            m_new = jnp.maximum(m_old, jnp.max(sm, axis=0, keepdims=True))
            m_ref[hd, mm] = m_new
            alpha[(hd, mm)] = jnp.exp2(m_old - m_new)
            s[(hd, mm)] = sm - m_new
        if nxt is not None:
            stage(*nxt)
        p = {ch: jnp.exp2(s[ch]).astype(BF16) for ch in chains}
        vt = vt_ref[0, j]
        ones = jnp.ones((DIFF_SUM_ROWS, TD), BF16)
        for hd in range(DIFF_HPS):
            vts = jnp.concatenate([vt[hd * LANES:(hd + 1) * LANES], ones], axis=0)
            for mm in range(2):
                acc_ref[hd, mm] = alpha[(hd, mm)] * acc_ref[hd, mm] + _dot(vts, p[(hd, mm)])

    stage(sa_ref, i)
    absorb(i, sa_ref, near=0, nxt=(sb_ref, jnp.maximum(i - 1, 0)))
    absorb(jnp.maximum(i - 1, 0), sb_ref, near=1, offset=jnp.where(i >= 1, 0.0, NEG),
           nxt=(sa_ref, jnp.maximum(i - 2, 0)))

    @pl.when(i >= 2)
    def _():
        def body(u, carry):
            ja = i - 2 - 2 * u
            absorb(ja, sa_ref, nxt=(sb_ref, ja - 1))
            absorb(ja - 1, sb_ref, nxt=(sa_ref, jnp.maximum(ja - 2, 0)))
            return carry

        lax.fori_loop(0, (i - 1) // 2, body, 0)

        @pl.when((i - 1) % 2 == 1)
        def _():
            absorb(0, sa_ref)

    lam = (jnp.exp(jnp.sum(lam_ref[0:1, :] * lam_ref[1:2, :], axis=1, keepdims=True))
           - jnp.exp(jnp.sum(lam_ref[2:3, :] * lam_ref[3:4, :], axis=1, keepdims=True)) + lambda_init)
    for hd in range(DIFF_HPS):
        l0, l1 = acc_ref[hd, 0, LANES:LANES + 1, :], acc_ref[hd, 1, LANES:LANES + 1, :]
        ot = acc_ref[hd, 0, 0:LANES, :] / l0 - lam * (acc_ref[hd, 1, 0:LANES, :] / l1)
        o = jnp.transpose(ot)
        o_ref[0, :, hd * LANES:(hd + 1) * LANES] = (
            _rms(o, g_ref[...], SUBLN_EPS) * (1.0 - lambda_init)).astype(o_ref.dtype)


def _diff_attention(proj3, proj_t, bias, lam_params, subln_g, lambda_init):
    b, s, _ = proj3.shape
    width = DIFF_HPS * LANES
    groups = DIFF_WIDTH // width
    nt = s // TD
    return pl.pallas_call(
        functools.partial(_diff_kernel, lambda_init=lambda_init),
        grid=(b, groups, nt),
        in_specs=[pl.BlockSpec((1, 1, width, TD), lambda bb, h, i: (bb, i, h, 0)),
                  pl.BlockSpec((1, s, width), lambda bb, h, i: (bb, 0, groups + h), pipeline_mode=pl.Buffered(1)),
                  pl.BlockSpec((1, nt, width, TD), lambda bb, h, i: (bb, 0, groups + h, 0),
                               pipeline_mode=pl.Buffered(1)),
                  pl.BlockSpec((DIFF_HPS, 2, 2, MAX_DISTANCE, MAX_DISTANCE), lambda bb, h, i: (h, 0, 0, 0, 0)),
                  pl.BlockSpec((4, HEAD_DIM), lambda bb, h, i: (0, 0)),
                  pl.BlockSpec((1, LANES), lambda bb, h, i: (0, 0))],
        out_specs=pl.BlockSpec((1, TD, width), lambda bb, h, i: (bb, i, h)),
        out_shape=jax.ShapeDtypeStruct((b, s, DIFF_WIDTH), BF16),
        scratch_shapes=[pltpu.VMEM((DIFF_HPS, 2, 1, TD), F32),
                        pltpu.VMEM((DIFF_HPS, 2, LANES + DIFF_SUM_ROWS, TD), F32),
                        pltpu.VMEM((DIFF_HPS, 2, TD, TD), F32), pltpu.VMEM((DIFF_HPS, 2, TD, TD), F32)],
        compiler_params=_cparams(("parallel", "parallel", "arbitrary")),
        name="diff_attention",
    )(proj_t, proj3, proj_t, bias, lam_params, subln_g.reshape(1, LANES))


def _diff_bias(t5_table):
    cols = t5_table.astype(F32)[:, SB_HEADS:]
    cols = (cols - cols[NUM_BUCKETS - 1:NUM_BUCKETS, :]) * LOG2E
    ki = np.arange(MAX_DISTANCE)[:, None]
    qi = np.arange(MAX_DISTANCE)[None, :]
    d0 = qi - ki
    diag = _bias_tile(cols, d0, d0 >= 0)
    prev = _bias_tile(cols, d0 + MAX_DISTANCE, np.ones_like(d0, bool))
    return jnp.stack([diag, prev], axis=1).reshape(DIFF_HEADS, 2, 2, MAX_DISTANCE, MAX_DISTANCE)


def _dil_kernel(q_ref, kp_ref, kc_ref, vp_ref, vc_ref, bias_ref, o_ref, qf, kf, vf, acc_s, m_s, l_s):
    t_idx = pl.program_id(2)
    qf[...] = q_ref[0].astype(F32)
    kf[0:DT, :] = kp_ref[0].astype(F32)
    kf[DT:2 * DT, :] = kc_ref[0].astype(F32)
    vf[0:DT, :] = vp_ref[0].astype(F32)
    vf[DT:2 * DT, :] = vc_ref[0].astype(F32)
    lane = lax.broadcasted_iota(jnp.int32, (1, LANES), 1)
    head0 = lane < HEAD_DIM
    heads = (head0, jnp.logical_not(head0))

    for g, (_, r) in enumerate(DIL_BRANCHES):
        nblk = DT // (r * DQ)
        shift = nblk.bit_length() - 1

        def group(t0, carry, g=g, r=r, nblk=nblk, shift=shift):
            rows, k0s, pre, s, p, mx = {}, {}, {}, {}, {}, {}

            def scores(u):
                t = t0 * DIL_UNROLL + u
                c = lax.shift_right_logical(t, shift)
                n = jnp.bitwise_and(t, nblk - 1)
                q0 = c + n * (DQ * r)
                k0s[u] = DT + q0 - DQ * r
                rows[u] = pl.ds(q0, DQ, stride=r)
                qg = qf[rows[u], :].astype(BF16)
                kg = kf[pl.ds(k0s[u], 2 * DQ, stride=r), :].astype(BF16)
                for hh in range(2):
                    s[(u, hh)] = _dot_nt(jnp.where(heads[hh], qg, jnp.zeros_like(qg)), kg)
                pre[u] = jnp.where(jnp.logical_and(t_idx == 0, n == 0), 1, 0)

            def softmax(u):
                for hh in range(2):
                    sb = s.pop((u, hh)) + bias_ref[0, hh, g, pre[u]]
                    m = jnp.max(sb, axis=1, keepdims=True)
                    p[(u, hh)] = jnp.exp2(sb - m).astype(BF16)
                    mx[(u, hh)] = jnp.broadcast_to(m, (DQ, LANES))

            def values(u):
                vg = vf[pl.ds(k0s[u], 2 * DQ, stride=r), :].astype(BF16)
                res = [_dot(p.pop((u, hh)), jnp.where(heads[hh], vg, jnp.ones_like(vg))) for hh in range(2)]
                acc_s[g, rows[u], :] = jnp.where(head0, res[0], res[1])
                m_s[g, rows[u], :] = jnp.where(head0, mx.pop((u, 0)), mx.pop((u, 1)))
                l_s[g, rows[u], :] = pltpu.roll(jnp.where(head0, res[1], res[0]), HEAD_DIM, 1)

            for step in range(DIL_UNROLL + 2 * DIL_SKEW):
                if step < DIL_UNROLL:
                    scores(step)
                if 0 <= step - DIL_SKEW < DIL_UNROLL:
                    softmax(step - DIL_SKEW)
                if step >= 2 * DIL_SKEW:
                    values(step - 2 * DIL_SKEW)
            return carry

        lax.fori_loop(0, r * nblk // DIL_UNROLL, group, 0)

    def merge(ci, carry):
        rows = pl.ds(pl.multiple_of(ci * TQ, TQ), TQ)
        ms = [m_s[g, rows, :] for g in range(len(DIL_BRANCHES))]
        m = functools.reduce(jnp.maximum, ms)
        num = jnp.zeros((TQ, LANES), F32)
        den = jnp.zeros((TQ, LANES), F32)
        for g in range(len(DIL_BRANCHES)):
            wgt = jnp.exp2(ms[g] - m)
            num = num + wgt * acc_s[g, rows, :]
            den = den + wgt * l_s[g, rows, :]
        o_ref[0, rows, :] = (num / den).astype(o_ref.dtype)
        return carry

    lax.fori_loop(0, DT // TQ, merge, 0)


def _dil_bias(t5_table):
    qi = np.arange(DQ)[:, None]
    ki = np.arange(2 * DQ)[None, :] - DQ
    steps = qi - ki
    valid = (steps >= 0) & (steps <= DIL_WINDOW)
    table = t5_table.astype(F32) * LOG2E
    tiles = [jnp.stack([_bias_tile(table, steps * r, v) for v in (valid, valid & (ki >= 0))], axis=1)
             for _, r in DIL_BRANCHES]
    return jnp.stack(tiles, axis=1).reshape(DIL_HEADS // 2, 2, len(DIL_BRANCHES), 2, DQ, 2 * DQ)


def _dilated_attention(proj3, t5_table):
    b, s, _ = proj3.shape
    pairs = DIL_HEADS // 2
    bias = _dil_bias(t5_table)
    cur = lambda off: (lambda bb, hp, t: (bb, t, off * pairs + hp))
    prev = lambda off: (lambda bb, hp, t: (bb, jnp.maximum(t - 1, 0), off * pairs + hp))
    tile = lambda imap: pl.BlockSpec((1, DT, LANES), imap)
    slab = lambda rows: pltpu.VMEM((rows, LANES), F32)
    per_branch = pltpu.VMEM((len(DIL_BRANCHES), DT, LANES), F32)
    return pl.pallas_call(
        _dil_kernel,
        grid=(b, pairs, s // DT),
        in_specs=[tile(cur(0)), tile(prev(1)), tile(cur(1)), tile(prev(2)), tile(cur(2)),
                  pl.BlockSpec((1, 2, len(DIL_BRANCHES), 2, DQ, 2 * DQ), lambda bb, hp, t: (hp, 0, 0, 0, 0, 0))],
        out_specs=tile(cur(0)),
        out_shape=jax.ShapeDtypeStruct((b, s, D_MODEL), BF16),
        scratch_shapes=[slab(DT), slab(2 * DT), slab(2 * DT), per_branch, per_branch, per_branch],
        compiler_params=_cparams(("parallel", "parallel", "arbitrary")),
        name="dilated_attention",
    )(proj3, proj3, proj3, proj3, proj3, bias)


def _post_kernel(*refs, n_o, final):
    h_ref = refs[0]
    o_refs = refs[1:1 + n_o]
    (wo_ref, p_ref, g_mlp_ref, w_up_ref, w_down_ref, g_ple_ref, w_gate_ref, w_proj_ref, g_fin_ref,
     out_ref, acc_ref) = refs[1 + n_o:]
    o = o_refs[0][...] if n_o == 1 else jnp.concatenate([r[...] for r in o_refs], axis=1)
    h = h_ref[...] + _dot(o, wo_ref[...])
    hn = _rms(h, g_mlp_ref[...], NORM_EPS).astype(BF16)
    acc_ref[...] = h
    for c in range(D_FF // FF_CHUNK):
        u = jnp.maximum(_dot(hn, w_up_ref[:, c * FF_CHUNK:(c + 1) * FF_CHUNK]), 0.0)
        acc_ref[...] += _dot((u * u).astype(BF16), w_down_ref[c * FF_CHUNK:(c + 1) * FF_CHUNK, :])
    h = acc_ref[...]
    gate = jax.nn.sigmoid(_dot(_rms(h, g_ple_ref[...], NORM_EPS).astype(BF16), w_gate_ref[...]))
    h = h + _dot(p_ref[...].astype(BF16), w_proj_ref[...]) * gate
    if final:
        h = _rms(h, g_fin_ref[...], NORM_EPS)
    out_ref[...] = h


def _post(h2d, o_list, w_out, p2d, g_mlp, w_up, w_down, g_ple, w_gate, w_proj, g_fin, final):
    m, d = h2d.shape
    n_o = len(o_list)
    row = lambda width: pl.BlockSpec((TM_POST, width), lambda i: (i, 0))
    const = lambda a: pl.BlockSpec(a.shape, lambda i: (0,) * a.ndim, pipeline_mode=pl.Buffered(1))
    vec = lambda g: g.reshape(1, d).astype(F32)
    weights = [w_out, vec(g_mlp), w_up, w_down, vec(g_ple), w_gate, w_proj, vec(g_fin)]
    return pl.pallas_call(
        functools.partial(_post_kernel, n_o=n_o, final=final),
        grid=(m // TM_POST,),
        in_specs=([row(d)] + [row(o.shape[1]) for o in o_list] + [const(weights[0])]
                  + [row(p2d.shape[1])] + [const(w) for w in weights[1:]]),
        out_specs=row(d),
        out_shape=jax.ShapeDtypeStruct((m, d), F32),
        scratch_shapes=[pltpu.VMEM((TM_POST, d), F32)],
        compiler_params=_cparams(("parallel",)),
        name="out_mlp_ple",
    )(h2d, *o_list, weights[0], p2d, *weights[1:])


def _scale_cols(w_in, col_scale):
    return (w_in * jnp.asarray(col_scale, F32)[None, :]).astype(BF16)


def kernel(x, p, t5_table, w_in_even, w_out_even, lambda_q1, lambda_k1, lambda_q2, lambda_k2, subln_g,
           w_in_odd, w_out_odd, norm_mix_g, norm_mlp_g, w_mlp_up, w_mlp_down, norm_ple_g, w_ple_gate,
           w_ple_proj, final_norm_g):
    b, s, d = x.shape
    depth = p.shape[0]
    assert d == D_MODEL and s % DT == 0 and (b * s) % TM_POST == 0
    assert all(w == DIL_WINDOW * r and w <= DT for w, r in DIL_BRANCHES)
    col = np.arange(3 * D_MODEL)
    qk_scale = HEAD_DIM ** -0.5
    even_q = (col < SB_WIDTH) | ((col >= 3 * SB_WIDTH) & (col < 3 * SB_WIDTH + DIFF_WIDTH))
    even_scale = np.where(even_q, qk_scale * LOG2E, 1.0)
    odd_scale = np.where(col < DIL_HEADS * HEAD_DIM, qk_scale * LOG2E, 1.0)
    h = x.reshape(b * s, d)
    for i in range(depth):
        if i % 2 == 0:
            e = i // 2
            w_in = _scale_cols(w_in_even[e], even_scale)
            qa, ka, va, qb, kb, vb = (w_in[:, lo:lo + SB_WIDTH] for lo in range(0, 6 * SB_WIDTH, SB_WIDTH))
            proj, qa_t, va_t, t_d = _norm_matmul(
                h, norm_mix_g[i], jnp.concatenate([ka, kb], axis=1),
                ((qa.T, TS), (va.T, SB_HALF), (jnp.concatenate([qb, vb], axis=1).T, TD)))
            proj = proj.reshape(b, s, -1)
            qa_t = qa_t.reshape(b, s // TS, SB_WIDTH, TS)
            va_t = va_t.reshape(b, s // SB_HALF, SB_WIDTH, SB_HALF)
            t_d = t_d.reshape(b, s // TD, 2 * DIFF_WIDTH, TD)
            lambda_init = 0.8 - 0.6 * math.exp(-0.3 * i)
            lam_params = jnp.stack([lambda_q1[e], lambda_k1[e], lambda_q2[e], lambda_k2[e]]).astype(F32)
            o_sb = _sb_attention(proj, qa_t, va_t)
            o_d = _diff_attention(proj, t_d, _diff_bias(t5_table), lam_params, subln_g[e].astype(F32),
                                  lambda_init)
            w_out = w_out_even[e].astype(BF16)
            o_list = [o_sb.reshape(b * s, SB_WIDTH), o_d.reshape(b * s, DIFF_WIDTH)]
        else:
            o = i // 2
            (proj,) = _norm_matmul(h, norm_mix_g[i], _scale_cols(w_in_odd[o], odd_scale))
            proj = proj.reshape(b, s, -1)
            o_list = [_dilated_attention(proj, t5_table).reshape(b * s, d)]
            w_out = w_out_odd[o].astype(BF16)
        h = _post(h, o_list, w_out, p[i].reshape(b * s, PLE_DIM), norm_mlp_g[i],
                  w_mlp_up[i].astype(BF16), w_mlp_down[i].astype(BF16), norm_ple_g[i],
                  w_ple_gate[i].astype(BF16), w_ple_proj[i].astype(BF16), final_norm_g,
                  final=(i == depth - 1))
    return h.reshape(b, s, d)
```

```python
import functools
import math

import numpy as np
import jax
import jax.numpy as jnp
from jax import lax
from jax.experimental import pallas as pl
from jax.experimental.pallas import tpu as pltpu

F32 = jnp.float32
BF16 = jnp.bfloat16

D_MODEL = 1024
HEAD_DIM = 64
LANES = 128
SB_HEADS = 8
DIFF_HEADS = 4
DIL_HEADS = 16
SB_WIDTH = SB_HEADS * HEAD_DIM
DIFF_WIDTH = DIFF_HEADS * 2 * HEAD_DIM
DIL_BRANCHES = ((128, 1), (512, 4), (2048, 16))
DIL_WINDOW = 128
NUM_BUCKETS = 32
MAX_DISTANCE = 128
D_FF = 4 * D_MODEL
PLE_DIM = 256
NORM_EPS = 1e-6
SUBLN_EPS = 1e-5
NEG = -1e30
LOG2E = math.log2(math.e)
SB_DEAD_LOG = 88.0

VMEM_LIMIT_BYTES = 58 * 1024 * 1024

TM_PROJ = 1024
TM_POST = 1024
FF_CHUNK = 512
TQ = 256
TS = 256
SB_QS = 8
SB_HALF = 128
TD = 512
DIFF_SUM_ROWS = 16
DIFF_HPS = 4
DT = 2048
DQ = DIL_WINDOW
DIL_UNROLL = 16
DIL_SKEW = 3


def _cparams(sem):
    return pltpu.CompilerParams(dimension_semantics=sem, vmem_limit_bytes=VMEM_LIMIT_BYTES)


def _rms(x, g, eps):
    return x * lax.rsqrt(jnp.mean(x * x, axis=-1, keepdims=True) + eps) * g


def _dot(a, b):
    return jnp.dot(a, b, preferred_element_type=F32)


def _dot_nt(a, b):
    return lax.dot_general(a, b, (((1,), (1,)), ((), ())), preferred_element_type=F32)


def _norm_matmul_kernel(h_ref, g_ref, w_ref, *rest, slabs):
    hn = _rms(h_ref[...], g_ref[...], NORM_EPS).astype(BF16)
    n_t = len(slabs)
    wt_refs, o_ref, ot_refs = rest[:n_t], rest[n_t], rest[n_t + 1:]
    for wt_ref, ot_ref, width in zip(wt_refs, ot_refs, slabs):
        res = _dot_nt(wt_ref[...], hn).astype(ot_ref.dtype)
        for c in range(TM_PROJ // width):
            ot_ref[c] = res[:, c * width:(c + 1) * width]
    o_ref[...] = _dot(hn, w_ref[...]).astype(o_ref.dtype)


def _norm_matmul(h2d, g, w, wts=()):
    m, k = h2d.shape
    n = w.shape[1]
    const = lambda a: pl.BlockSpec(a.shape, lambda i: (0,) * a.ndim, pipeline_mode=pl.Buffered(1))
    in_specs = [pl.BlockSpec((TM_PROJ, k), lambda i: (i, 0)), pl.BlockSpec((1, k), lambda i: (0, 0)), const(w)]
    in_specs += [const(wt) for wt, _ in wts]
    out_specs = [pl.BlockSpec((TM_PROJ, n), lambda i: (i, 0))]
    out_shape = [jax.ShapeDtypeStruct((m, n), BF16)]
    for wt, width in wts:
        per_step = TM_PROJ // width
        out_specs.append(pl.BlockSpec((per_step, wt.shape[0], width), lambda i: (i, 0, 0)))
        out_shape.append(jax.ShapeDtypeStruct((m // width, wt.shape[0], width), BF16))
    return pl.pallas_call(
        functools.partial(_norm_matmul_kernel, slabs=tuple(width for _, width in wts)),
        grid=(m // TM_PROJ,),
        in_specs=in_specs, out_specs=out_specs, out_shape=out_shape,
        compiler_params=_cparams(("parallel",)),
        name="norm_in_proj",
    )(h2d, g.reshape(1, k), w, *[wt for wt, _ in wts])


def _t5_bucket_np(dist):
    n = np.maximum(dist, 0)
    max_exact = NUM_BUCKETS // 2
    nf = np.maximum(n, 1).astype(np.float32)
    large = max_exact + (np.log(nf / np.float32(max_exact)) / np.float32(math.log(MAX_DISTANCE / max_exact))
                         * np.float32(NUM_BUCKETS - max_exact)).astype(np.int32)
    large = np.minimum(large, NUM_BUCKETS - 1)
    return np.where(n < max_exact, n, large).astype(np.int32)


def _bias_tile(table_cols, dist, valid):
    bucket = _t5_bucket_np(dist)
    out = jnp.zeros((table_cols.shape[1],) + dist.shape, F32)
    for bkt in np.unique(bucket[valid]):
        out = jnp.where(jnp.asarray(bucket == bkt)[None], table_cols[bkt][:, None, None], out)
    return jnp.where(jnp.asarray(valid)[None], out, NEG)


def _sb_kernel(qt_ref, k_ref, vt_ref, o_ref, acc_ref, c_ref):
    i = pl.program_id(2)
    row = lax.broadcasted_iota(jnp.int32, (LANES, 1), 0)
    head_rows = (row < HEAD_DIM, row >= HEAD_DIM)
    qts = [tuple(jnp.where(hr, qt_ref[0, a], jnp.zeros((LANES, TS), BF16)) for hr in head_rows)
           for a in range(SB_QS)]
    kk = lax.broadcasted_iota(jnp.int32, (TS, TS), 0)
    qq = lax.broadcasted_iota(jnp.int32, (TS, TS), 1)
    strict = kk < qq
    per_half = TS // SB_HALF

    def tri2(n):
        tri = (lax.broadcasted_iota(jnp.int32, (n, n), 0) <= lax.broadcasted_iota(jnp.int32, (n, n), 1)).astype(BF16)
        return jnp.concatenate([tri, tri], axis=1)

    tri2s = {n: tri2(n) for n in (TS, SB_HALF)}

    def tiles(work, c_in):
        items = [(n, hh) for n in range(len(work)) for hh in range(2)]
        ks = [k_ref[0, pl.ds(pl.multiple_of(k0, SB_HALF), nk), :] for _, k0, nk, _, _ in work]
        z = {(n, hh): _dot(ks[n], qts[work[n][0]][hh]) for n, hh in items}
        parts = {}
        for it in items:
            keep = work[it[0]][4]
            if keep is not None:
                z[it] = jnp.where(keep, z[it], NEG)
            lk = -(jnp.maximum(z[it], 0.0) + jnp.log2(1.0 + jnp.exp2(-jnp.abs(z[it]))))
            hi = lk.astype(BF16)
            parts[it] = jnp.concatenate([hi, (lk - hi.astype(F32)).astype(BF16)], axis=0)
        incl = {it: _dot(tri2s[work[it[0]][2]], parts[it]) for it in items}
        c = dict(c_in)
        w = {}
        for n, hh in items:
            a = work[n][0]
            w[(n, hh)] = jnp.exp2(z[(n, hh)] + incl[(n, hh)] + c[(a, hh)]).astype(BF16)
            c[(a, hh)] = c[(a, hh)] + incl[(n, hh)][0:1, :]
        per_head = {}
        for n, hh in items:
            a = work[n][0]
            part = _dot(work[n][3](), w[(n, hh)])
            per_head[(a, hh)] = part if (a, hh) not in per_head else per_head[(a, hh)] + part
        upd = {a: jnp.where(head_rows[0], per_head[(a, 0)], per_head[(a, 1)]) for a, _ in per_head}
        return upd, c

    def half_tile(a, e, keep=None):
        return (a, e * SB_HALF, SB_HALF, lambda: vt_ref[0, e], keep)

    def diag_vt(g):
        return jnp.concatenate([vt_ref[0, g * per_half + c] for c in range(per_half)], axis=1)

    zero_c = jnp.zeros((1, TS), F32)
    work = []
    for a in range(SB_QS):
        g = i * SB_QS + a
        work.append((a, g * TS, TS, functools.partial(diag_vt, g), strict))
        if a > 0:
            work.append(half_tile(a, g * per_half - 1))
        else:
            exists = lax.broadcasted_iota(jnp.int32, (SB_HALF, TS), 0) < jnp.where(g >= 1, SB_HALF, 0)
            work.append(half_tile(a, jnp.maximum(g * per_half - 1, 0), exists))
    upd, c1 = tiles(work, {(a, hh): zero_c for a in range(SB_QS) for hh in range(2)})
    for a in range(SB_QS):
        acc_ref[a] = upd[a]
        for hh in range(2):
            c_ref[a, hh] = c1[(a, hh)]

    for a in range(SB_QS):
        def cond(state):
            e, alive = state
            return jnp.logical_and(e >= 0, alive > -SB_DEAD_LOG * LOG2E)

        def body(state, a=a):
            e, _ = state
            upd, c_new = tiles([half_tile(a, e)], {(a, hh): c_ref[a, hh] for hh in range(2)})
            acc_ref[a] += upd[a]
            for hh in range(2):
                c_ref[a, hh] = c_new[(a, hh)]
            return e - 1, jnp.maximum(jnp.max(c_new[(a, 0)]), jnp.max(c_new[(a, 1)]))

        alive1 = jnp.maximum(jnp.max(c1[(a, 0)]), jnp.max(c1[(a, 1)]))
        lax.while_loop(cond, body, ((i * SB_QS + a) * per_half - 2, alive1))
        o_ref[0, a * TS:(a + 1) * TS, :] = jnp.transpose(acc_ref[a]).astype(o_ref.dtype)


def _sb_attention(proj3, q_t, v_t):
    b, s, _ = proj3.shape
    pairs = SB_WIDTH // LANES
    return pl.pallas_call(
        _sb_kernel,
        grid=(b, pairs, s // (SB_QS * TS)),
        in_specs=[pl.BlockSpec((1, SB_QS, LANES, TS), lambda bb, hp, i: (bb, i, hp, 0)),
                  pl.BlockSpec((1, s, LANES), lambda bb, hp, i: (bb, 0, hp)),
                  pl.BlockSpec((1, s // SB_HALF, LANES, SB_HALF), lambda bb, hp, i: (bb, 0, hp, 0))],
        out_specs=pl.BlockSpec((1, SB_QS * TS, LANES), lambda bb, hp, i: (bb, i, hp)),
        out_shape=jax.ShapeDtypeStruct((b, s, SB_WIDTH), BF16),
        scratch_shapes=[pltpu.VMEM((SB_QS, LANES, TS), F32), pltpu.VMEM((SB_QS, 2, 1, TS), F32)],
        compiler_params=_cparams(("parallel", "parallel", "arbitrary")),
        name="sb_attention",
    )(q_t, proj3, v_t)


def _diff_kernel(qt_ref, k_ref, vt_ref, bias_ref, lam_ref, g_ref, o_ref, m_ref, acc_ref, sa_ref, sb_ref,
                 *, lambda_init):
    i = pl.program_id(2)
    chains = [(hd, mm) for hd in range(DIFF_HPS) for mm in range(2)]
    row = lax.broadcasted_iota(jnp.int32, (LANES, 1), 0)
    map_rows = (row < HEAD_DIM, row >= HEAD_DIM)
    qts = {}
    for hd in range(DIFF_HPS):
        qt = qt_ref[0, 0, hd * LANES:(hd + 1) * LANES, :]
        for mm in range(2):
            qts[(hd, mm)] = jnp.where(map_rows[mm], qt, jnp.zeros_like(qt))

    m_ref[...] = jnp.full_like(m_ref, NEG)
    acc_ref[...] = jnp.zeros_like(acc_ref)

    def stage(s_ref, j):
        ks = k_ref[0, pl.ds(pl.multiple_of(j * TD, TD), TD), :]
        for hd, mm in chains:
            s_ref[hd, mm] = _dot(ks[:, hd * LANES:(hd + 1) * LANES], qts[(hd, mm)])

    def add_bias(s_ref, hd, mm, near):
        nb = TD // MAX_DISTANCE
        blk = lambda kb, qb: (hd, mm, slice(kb * MAX_DISTANCE, (kb + 1) * MAX_DISTANCE),
                              slice(qb * MAX_DISTANCE, (qb + 1) * MAX_DISTANCE))
        if near == 1:
            s_ref[blk(nb - 1, 0)] += bias_ref[hd, mm, 1]
            return
        for kb in range(nb):
            for qb in range(nb):
                if kb > qb:
                    s_ref[blk(kb, qb)] = jnp.full((MAX_DISTANCE, MAX_DISTANCE), NEG, F32)
                elif kb == qb:
                    s_ref[blk(kb, qb)] += bias_ref[hd, mm, 0]
                elif kb == qb - 1:
                    s_ref[blk(kb, qb)] += bias_ref[hd, mm, 1]

    def absorb(j, s_ref, near=None, offset=None, nxt=None):
        s, alpha = {}, {}
        for hd, mm in chains:
            if near is not None:
                add_bias(s_ref, hd, mm, near)
            sm = s_ref[hd, mm]
            if offset is not None:
                sm = sm + offset
            m_old = m_ref[hd, mm]
            m_new = jnp.maximum(m_old, jnp.max(sm, axis=0, keepdims=True))
            m_ref[hd, mm] = m_new
            alpha[(hd, mm)] = jnp.exp2(m_old - m_new)
            s[(hd, mm)] = sm - m_new
        if nxt is not None:
            stage(*nxt)
        p = {ch: jnp.exp2(s[ch]).astype(BF16) for ch in chains}
        vt = vt_ref[0, j]
        ones = jnp.ones((DIFF_SUM_ROWS, TD), BF16)
        for hd in range(DIFF_HPS):
            vts = jnp.concatenate([vt[hd * LANES:(hd + 1) * LANES], ones], axis=0)
            for mm in range(2):
                acc_ref[hd, mm] = alpha[(hd, mm)] * acc_ref[hd, mm] + _dot(vts, p[(hd, mm)])

    stage(sa_ref, i)
    absorb(i, sa_ref, near=0, nxt=(sb_ref, jnp.maximum(i - 1, 0)))
    absorb(jnp.maximum(i - 1, 0), sb_ref, near=1, offset=jnp.where(i >= 1, 0.0, NEG),
           nxt=(sa_ref, jnp.maximum(i - 2, 0)))

    @pl.when(i >= 2)
    def _():
        def body(u, carry):
            ja = i - 2 - 2 * u
            absorb(ja, sa_ref, nxt=(sb_ref, ja - 1))
            absorb(ja - 1, sb_ref, nxt=(sa_ref, jnp.maximum(ja - 2, 0)))
            return carry

        lax.fori_loop(0, (i - 1) // 2, body, 0)

        @pl.when((i - 1) % 2 == 1)
        def _():
            absorb(0, sa_ref)

    lam = (jnp.exp(jnp.sum(lam_ref[0:1, :] * lam_ref[1:2, :], axis=1, keepdims=True))
           - jnp.exp(jnp.sum(lam_ref[2:3, :] * lam_ref[3:4, :], axis=1, keepdims=True)) + lambda_init)
    for hd in range(DIFF_HPS):
        l0, l1 = acc_ref[hd, 0, LANES:LANES + 1, :], acc_ref[hd, 1, LANES:LANES + 1, :]
        ot = acc_ref[hd, 0, 0:LANES, :] / l0 - lam * (acc_ref[hd, 1, 0:LANES, :] / l1)
        o = jnp.transpose(ot)
        o_ref[0, :, hd * LANES:(hd + 1) * LANES] = (
            _rms(o, g_ref[...], SUBLN_EPS) * (1.0 - lambda_init)).astype(o_ref.dtype)


def _diff_attention(proj3, proj_t, bias, lam_params, subln_g, lambda_init):
    b, s, _ = proj3.shape
    width = DIFF_HPS * LANES
    groups = DIFF_WIDTH // width
    nt = s // TD
    return pl.pallas_call(
        functools.partial(_diff_kernel, lambda_init=lambda_init),
        grid=(b, groups, nt),
        in_specs=[pl.BlockSpec((1, 1, width, TD), lambda bb, h, i: (bb, i, h, 0)),
                  pl.BlockSpec((1, s, width), lambda bb, h, i: (bb, 0, groups + h), pipeline_mode=pl.Buffered(1)),
                  pl.BlockSpec((1, nt, width, TD), lambda bb, h, i: (bb, 0, groups + h, 0),
                               pipeline_mode=pl.Buffered(1)),
                  pl.BlockSpec((DIFF_HPS, 2, 2, MAX_DISTANCE, MAX_DISTANCE), lambda bb, h, i: (h, 0, 0, 0, 0)),
                  pl.BlockSpec((4, HEAD_DIM), lambda bb, h, i: (0, 0)),
                  pl.BlockSpec((1, LANES), lambda bb, h, i: (0, 0))],
        out_specs=pl.BlockSpec((1, TD, width), lambda bb, h, i: (bb, i, h)),
        out_shape=jax.ShapeDtypeStruct((b, s, DIFF_WIDTH), BF16),
        scratch_shapes=[pltpu.VMEM((DIFF_HPS, 2, 1, TD), F32),
                        pltpu.VMEM((DIFF_HPS, 2, LANES + DIFF_SUM_ROWS, TD), F32),
                        pltpu.VMEM((DIFF_HPS, 2, TD, TD), F32), pltpu.VMEM((DIFF_HPS, 2, TD, TD), F32)],
        compiler_params=_cparams(("parallel", "parallel", "arbitrary")),
        name="diff_attention",
    )(proj_t, proj3, proj_t, bias, lam_params, subln_g.reshape(1, LANES))


def _diff_bias(t5_table):
    cols = t5_table.astype(F32)[:, SB_HEADS:]
    cols = (cols - cols[NUM_BUCKETS - 1:NUM_BUCKETS, :]) * LOG2E
    ki = np.arange(MAX_DISTANCE)[:, None]
    qi = np.arange(MAX_DISTANCE)[None, :]
    d0 = qi - ki
    diag = _bias_tile(cols, d0, d0 >= 0)
    prev = _bias_tile(cols, d0 + MAX_DISTANCE, np.ones_like(d0, bool))
    return jnp.stack([diag, prev], axis=1).reshape(DIFF_HEADS, 2, 2, MAX_DISTANCE, MAX_DISTANCE)


def _dil_kernel(q_ref, kp_ref, kc_ref, vp_ref, vc_ref, bias_ref, o_ref, qf, kf, vf, acc_s, m_s, l_s):
    t_idx = pl.program_id(2)
    qf[...] = q_ref[0].astype(F32)
    kf[0:DT, :] = kp_ref[0].astype(F32)
    kf[DT:2 * DT, :] = kc_ref[0].astype(F32)
    vf[0:DT, :] = vp_ref[0].astype(F32)
    vf[DT:2 * DT, :] = vc_ref[0].astype(F32)
    lane = lax.broadcasted_iota(jnp.int32, (1, LANES), 1)
    head0 = lane < HEAD_DIM
    heads = (head0, jnp.logical_not(head0))

    for g, (_, r) in enumerate(DIL_BRANCHES):
        nblk = DT // (r * DQ)
        shift = nblk.bit_length() - 1

        def group(t0, carry, g=g, r=r, nblk=nblk, shift=shift):
            rows, k0s, pre, s, p, mx = {}, {}, {}, {}, {}, {}

            def scores(u):
                t = t0 * DIL_UNROLL + u
                c = lax.shift_right_logical(t, shift)
                n = jnp.bitwise_and(t, nblk - 1)
                q0 = c + n * (DQ * r)
                k0s[u] = DT + q0 - DQ * r
                rows[u] = pl.ds(q0, DQ, stride=r)
                qg = qf[rows[u], :].astype(BF16)
                kg = kf[pl.ds(k0s[u], 2 * DQ, stride=r), :].astype(BF16)
                for hh in range(2):
                    s[(u, hh)] = _dot_nt(jnp.where(heads[hh], qg, jnp.zeros_like(qg)), kg)
                pre[u] = jnp.where(jnp.logical_and(t_idx == 0, n == 0), 1, 0)

            def softmax(u):
                for hh in range(2):
                    sb = s.pop((u, hh)) + bias_ref[0, hh, g, pre[u]]
                    m = jnp.max(sb, axis=1, keepdims=True)
                    p[(u, hh)] = jnp.exp2(sb - m).astype(BF16)
                    mx[(u, hh)] = jnp.broadcast_to(m, (DQ, LANES))

            def values(u):
                vg = vf[pl.ds(k0s[u], 2 * DQ, stride=r), :].astype(BF16)
                res = [_dot(p.pop((u, hh)), jnp.where(heads[hh], vg, jnp.ones_like(vg))) for hh in range(2)]
                acc_s[g, rows[u], :] = jnp.where(head0, res[0], res[1])
                m_s[g, rows[u], :] = jnp.where(head0, mx.pop((u, 0)), mx.pop((u, 1)))
                l_s[g, rows[u], :] = pltpu.roll(jnp.where(head0, res[1], res[0]), HEAD_DIM, 1)

            for step in range(DIL_UNROLL + 2 * DIL_SKEW):
                if step < DIL_UNROLL:
                    scores(step)
                if 0 <= step - DIL_SKEW < DIL_UNROLL:
                    softmax(step - DIL_SKEW)
                if step >= 2 * DIL_SKEW:
                    values(step - 2 * DIL_SKEW)
            return carry

        lax.fori_loop(0, r * nblk // DIL_UNROLL, group, 0)

    def merge(ci, carry):
        rows = pl.ds(pl.multiple_of(ci * TQ, TQ), TQ)
        ms = [m_s[g, rows, :] for g in range(len(DIL_BRANCHES))]
        m = functools.reduce(jnp.maximum, ms)
        num = jnp.zeros((TQ, LANES), F32)
        den = jnp.zeros((TQ, LANES), F32)
        for g in range(len(DIL_BRANCHES)):
            wgt = jnp.exp2(ms[g] - m)
            num = num + wgt * acc_s[g, rows, :]
            den = den + wgt * l_s[g, rows, :]
        o_ref[0, rows, :] = (num / den).astype(o_ref.dtype)
        return carry

    lax.fori_loop(0, DT // TQ, merge, 0)


def _dil_bias(t5_table):
    qi = np.arange(DQ)[:, None]
    ki = np.arange(2 * DQ)[None, :] - DQ
    steps = qi - ki
    valid = (steps >= 0) & (steps <= DIL_WINDOW)
    table = t5_table.astype(F32) * LOG2E
    tiles = [jnp.stack([_bias_tile(table, steps * r, v) for v in (valid, valid & (ki >= 0))], axis=1)
             for _, r in DIL_BRANCHES]
    return jnp.stack(tiles, axis=1).reshape(DIL_HEADS // 2, 2, len(DIL_BRANCHES), 2, DQ, 2 * DQ)


def _dilated_attention(proj3, t5_table):
    b, s, _ = proj3.shape
    pairs = DIL_HEADS // 2
    bias = _dil_bias(t5_table)
    cur = lambda off: (lambda bb, hp, t: (bb, t, off * pairs + hp))
    prev = lambda off: (lambda bb, hp, t: (bb, jnp.maximum(t - 1, 0), off * pairs + hp))
    tile = lambda imap: pl.BlockSpec((1, DT, LANES), imap)
    slab = lambda rows: pltpu.VMEM((rows, LANES), F32)
    per_branch = pltpu.VMEM((len(DIL_BRANCHES), DT, LANES), F32)
    return pl.pallas_call(
        _dil_kernel,
        grid=(b, pairs, s // DT),
        in_specs=[tile(cur(0)), tile(prev(1)), tile(cur(1)), tile(prev(2)), tile(cur(2)),
                  pl.BlockSpec((1, 2, len(DIL_BRANCHES), 2, DQ, 2 * DQ), lambda bb, hp, t: (hp, 0, 0, 0, 0, 0))],
        out_specs=tile(cur(0)),
        out_shape=jax.ShapeDtypeStruct((b, s, D_MODEL), BF16),
        scratch_shapes=[slab(DT), slab(2 * DT), slab(2 * DT), per_branch, per_branch, per_branch],
        compiler_params=_cparams(("parallel", "parallel", "arbitrary")),
        name="dilated_attention",
    )(proj3, proj3, proj3, proj3, proj3, bias)


def _post_kernel(*refs, n_o, final):
    h_ref = refs[0]
    o_refs = refs[1:1 + n_o]
    (wo_ref, p_ref, g_mlp_ref, w_up_ref, w_down_ref, g_ple_ref, w_gate_ref, w_proj_ref, g_fin_ref,
     out_ref, acc_ref) = refs[1 + n_o:]
    o = o_refs[0][...] if n_o == 1 else jnp.concatenate([r[...] for r in o_refs], axis=1)
    h = h_ref[...] + _dot(o, wo_ref[...])
    hn = _rms(h, g_mlp_ref[...], NORM_EPS).astype(BF16)
    acc_ref[...] = h
    for c in range(D_FF // FF_CHUNK):
        u = jnp.maximum(_dot(hn, w_up_ref[:, c * FF_CHUNK:(c + 1) * FF_CHUNK]), 0.0)
        acc_ref[...] += _dot((u * u).astype(BF16), w_down_ref[c * FF_CHUNK:(c + 1) * FF_CHUNK, :])
    h = acc_ref[...]
    gate = jax.nn.sigmoid(_dot(_rms(h, g_ple_ref[...], NORM_EPS).astype(BF16), w_gate_ref[...]))
    h = h + _dot(p_ref[...].astype(BF16), w_proj_ref[...]) * gate
    if final:
        h = _rms(h, g_fin_ref[...], NORM_EPS)
    out_ref[...] = h


def _post(h2d, o_list, w_out, p2d, g_mlp, w_up, w_down, g_ple, w_gate, w_proj, g_fin, final):
    m, d = h2d.shape
    n_o = len(o_list)
    row = lambda width: pl.BlockSpec((TM_POST, width), lambda i: (i, 0))
    const = lambda a: pl.BlockSpec(a.shape, lambda i: (0,) * a.ndim, pipeline_mode=pl.Buffered(1))
    vec = lambda g: g.reshape(1, d).astype(F32)
    weights = [w_out, vec(g_mlp), w_up, w_down, vec(g_ple), w_gate, w_proj, vec(g_fin)]
    return pl.pallas_call(
        functools.partial(_post_kernel, n_o=n_o, final=final),
        grid=(m // TM_POST,),
        in_specs=([row(d)] + [row(o.shape[1]) for o in o_list] + [const(weights[0])]
                  + [row(p2d.shape[1])] + [const(w) for w in weights[1:]]),
        out_specs=row(d),
        out_shape=jax.ShapeDtypeStruct((m, d), F32),
        scratch_shapes=[pltpu.VMEM((TM_POST, d), F32)],
        compiler_params=_cparams(("parallel",)),
        name="out_mlp_ple",
    )(h2d, *o_list, weights[0], p2d, *weights[1:])


def _scale_cols(w_in, col_scale):
    return (w_in * jnp.asarray(col_scale, F32)[None, :]).astype(BF16)


def kernel(x, p, t5_table, w_in_even, w_out_even, lambda_q1, lambda_k1, lambda_q2, lambda_k2, subln_g,
           w_in_odd, w_out_odd, norm_mix_g, norm_mlp_g, w_mlp_up, w_mlp_down, norm_ple_g, w_ple_gate,
           w_ple_proj, final_norm_g):
    b, s, d = x.shape
    depth = p.shape[0]
    assert d == D_MODEL and s % DT == 0 and (b * s) % TM_POST == 0
    assert all(w == DIL_WINDOW * r and w <= DT for w, r in DIL_BRANCHES)
    col = np.arange(3 * D_MODEL)
    qk_scale = HEAD_DIM ** -0.5
    even_q = (col < SB_WIDTH) | ((col >= 3 * SB_WIDTH) & (col < 3 * SB_WIDTH + DIFF_WIDTH))
    even_scale = np.where(even_q, qk_scale * LOG2E, 1.0)
    odd_scale = np.where(col < DIL_HEADS * HEAD_DIM, qk_scale * LOG2E, 1.0)
    h = x.reshape(b * s, d)
    for i in range(depth):
        if i % 2 == 0:
            e = i // 2
            w_in = _scale_cols(w_in_even[e], even_scale)
            qa, ka, va, qb, kb, vb = (w_in[:, lo:lo + SB_WIDTH] for lo in range(0, 6 * SB_WIDTH, SB_WIDTH))
            proj, qa_t, va_t, t_d = _norm_matmul(
                h, norm_mix_g[i], jnp.concatenate([ka, kb], axis=1),
                ((qa.T, TS), (va.T, SB_HALF), (jnp.concatenate([qb, vb], axis=1).T, TD)))
            proj = proj.reshape(b, s, -1)
            qa_t = qa_t.reshape(b, s // TS, SB_WIDTH, TS)
            va_t = va_t.reshape(b, s // SB_HALF, SB_WIDTH, SB_HALF)
            t_d = t_d.reshape(b, s // TD, 2 * DIFF_WIDTH, TD)
            lambda_init = 0.8 - 0.6 * math.exp(-0.3 * i)
            lam_params = jnp.stack([lambda_q1[e], lambda_k1[e], lambda_q2[e], lambda_k2[e]]).astype(F32)
            o_sb = _sb_attention(proj, qa_t, va_t)
            o_d = _diff_attention(proj, t_d, _diff_bias(t5_table), lam_params, subln_g[e].astype(F32),
                                  lambda_init)
            w_out = w_out_even[e].astype(BF16)
            o_list = [o_sb.reshape(b * s, SB_WIDTH), o_d.reshape(b * s, DIFF_WIDTH)]
        else:
            o = i // 2
            (proj,) = _norm_matmul(h, norm_mix_g[i], _scale_cols(w_in_odd[o], odd_scale))
            proj = proj.reshape(b, s, -1)
            o_list = [_dilated_attention(proj, t5_table).reshape(b * s, d)]
            w_out = w_out_odd[o].astype(BF16)
        h = _post(h, o_list, w_out, p[i].reshape(b * s, PLE_DIM), norm_mlp_g[i],
                  w_mlp_up[i].astype(BF16), w_mlp_down[i].astype(BF16), norm_ple_g[i],
                  w_ple_gate[i].astype(BF16), w_ple_proj[i].astype(BF16), final_norm_g,
                  final=(i == depth - 1))
    return h.reshape(b, s, d)
```

```python
import functools
import math

import numpy as np
import jax
import jax.numpy as jnp
from jax import lax
from jax.experimental import pallas as pl
from jax.experimental.pallas import tpu as pltpu

F32 = jnp.float32
BF16 = jnp.bfloat16

D_MODEL = 1024
HEAD_DIM = 64
LANES = 128
SB_HEADS = 8
DIFF_HEADS = 4
DIL_HEADS = 16
SB_WIDTH = SB_HEADS * HEAD_DIM
DIFF_WIDTH = DIFF_HEADS * 2 * HEAD_DIM
DIL_BRANCHES = ((128, 1), (512, 4), (2048, 16))
DIL_WINDOW = 128
NUM_BUCKETS = 32
MAX_DISTANCE = 128
D_FF = 4 * D_MODEL
PLE_DIM = 256
NORM_EPS = 1e-6
SUBLN_EPS = 1e-5
NEG = -1e30
LOG2E = math.log2(math.e)
SB_DEAD_LOG = 88.0

VMEM_LIMIT_BYTES = 58 * 1024 * 1024

TM_PROJ = 1024
TM_POST = 1024
FF_CHUNK = 512
TQ = 256
TS = 256
SB_QS = 4
SB_HALF = 128
TD = 512
DIFF_SUM_ROWS = 16
DIFF_HPS = 4
DT = 2048
DQ = DIL_WINDOW
DIL_PAIRS = 2
DIL_UNROLL = 16
DIL_SKEW = 3


def _cparams(sem):
    return pltpu.CompilerParams(dimension_semantics=sem, vmem_limit_bytes=VMEM_LIMIT_BYTES)


def _rms(x, g, eps):
    return x * lax.rsqrt(jnp.mean(x * x, axis=-1, keepdims=True) + eps) * g


def _dot(a, b):
    return jnp.dot(a, b, preferred_element_type=F32)


def _dot_nt(a, b):
    return lax.dot_general(a, b, (((1,), (1,)), ((), ())), preferred_element_type=F32)


def _norm_matmul_kernel(h_ref, g_ref, w_ref, *rest, slabs):
    hn = _rms(h_ref[...], g_ref[...], NORM_EPS).astype(BF16)
    n_t = len(slabs)
    wt_refs, o_ref, ot_refs = rest[:n_t], rest[n_t], rest[n_t + 1:]
    for wt_ref, ot_ref, width in zip(wt_refs, ot_refs, slabs):
        res = _dot_nt(wt_ref[...], hn).astype(ot_ref.dtype)
        for c in range(TM_PROJ // width):
            ot_ref[c] = res[:, c * width:(c + 1) * width]
    o_ref[...] = _dot(hn, w_ref[...]).astype(o_ref.dtype)


def _norm_matmul(h2d, g, w, wts=()):
    m, k = h2d.shape
    n = w.shape[1]
    const = lambda a: pl.BlockSpec(a.shape, lambda i: (0,) * a.ndim, pipeline_mode=pl.Buffered(1))
    in_specs = [pl.BlockSpec((TM_PROJ, k), lambda i: (i, 0)), pl.BlockSpec((1, k), lambda i: (0, 0)), const(w)]
    in_specs += [const(wt) for wt, _ in wts]
    out_specs = [pl.BlockSpec((TM_PROJ, n), lambda i: (i, 0))]
    out_shape = [jax.ShapeDtypeStruct((m, n), BF16)]
    for wt, width in wts:
        per_step = TM_PROJ // width
        out_specs.append(pl.BlockSpec((per_step, wt.shape[0], width), lambda i: (i, 0, 0)))
        out_shape.append(jax.ShapeDtypeStruct((m // width, wt.shape[0], width), BF16))
    return pl.pallas_call(
        functools.partial(_norm_matmul_kernel, slabs=tuple(width for _, width in wts)),
        grid=(m // TM_PROJ,),
        in_specs=in_specs, out_specs=out_specs, out_shape=out_shape,
        compiler_params=_cparams(("parallel",)),
        name="norm_in_proj",
    )(h2d, g.reshape(1, k), w, *[wt for wt, _ in wts])


def _t5_bucket_np(dist):
    n = np.maximum(dist, 0)
    max_exact = NUM_BUCKETS // 2
    nf = np.maximum(n, 1).astype(np.float32)
    large = max_exact + (np.log(nf / np.float32(max_exact)) / np.float32(math.log(MAX_DISTANCE / max_exact))
                         * np.float32(NUM_BUCKETS - max_exact)).astype(np.int32)
    large = np.minimum(large, NUM_BUCKETS - 1)
    return np.where(n < max_exact, n, large).astype(np.int32)


def _bias_tile(table_cols, dist, valid):
    bucket = _t5_bucket_np(dist)
    out = jnp.zeros((table_cols.shape[1],) + dist.shape, F32)
    for bkt in np.unique(bucket[valid]):
        out = jnp.where(jnp.asarray(bucket == bkt)[None], table_cols[bkt][:, None, None], out)
    return jnp.where(jnp.asarray(valid)[None], out, NEG)


def _sb_kernel(qt_ref, k_ref, vt_ref, o_ref, acc_ref, c_ref):
    i = pl.program_id(2)
    row = lax.broadcasted_iota(jnp.int32, (LANES, 1), 0)
    head_rows = (row < HEAD_DIM, row >= HEAD_DIM)
    qts = [tuple(jnp.where(hr, qt_ref[0, a], jnp.zeros((LANES, TS), BF16)) for hr in head_rows)
           for a in range(SB_QS)]
    kk = lax.broadcasted_iota(jnp.int32, (TS, TS), 0)
    qq = lax.broadcasted_iota(jnp.int32, (TS, TS), 1)
    strict = kk < qq
    per_half = TS // SB_HALF

    def tri2(n):
        tri = (lax.broadcasted_iota(jnp.int32, (n, n), 0) <= lax.broadcasted_iota(jnp.int32, (n, n), 1)).astype(BF16)
        return jnp.concatenate([tri, tri], axis=1)

    tri2s = {n: tri2(n) for n in (TS, SB_HALF)}

    def tiles(work, c_in):
        items = [(n, hh) for n in range(len(work)) for hh in range(2)]
        ks = [k_ref[0, pl.ds(pl.multiple_of(k0, SB_HALF), nk), :] for _, k0, nk, _, _ in work]
        z = {(n, hh): _dot(ks[n], qts[work[n][0]][hh]) for n, hh in items}
        parts = {}
        for it in items:
            keep = work[it[0]][4]
            if keep is not None:
                z[it] = jnp.where(keep, z[it], NEG)
            lk = -(jnp.maximum(z[it], 0.0) + jnp.log2(1.0 + jnp.exp2(-jnp.abs(z[it]))))
            hi = lk.astype(BF16)
            parts[it] = jnp.concatenate([hi, (lk - hi.astype(F32)).astype(BF16)], axis=0)
        incl = {it: _dot(tri2s[work[it[0]][2]], parts[it]) for it in items}
        c = dict(c_in)
        w = {}
        for n, hh in items:
            a = work[n][0]
            w[(n, hh)] = jnp.exp2(z[(n, hh)] + incl[(n, hh)] + c[(a, hh)]).astype(BF16)
            c[(a, hh)] = c[(a, hh)] + incl[(n, hh)][0:1, :]
        per_head = {}
        for n, hh in items:
            a = work[n][0]
            part = _dot(work[n][3](), w[(n, hh)])
            per_head[(a, hh)] = part if (a, hh) not in per_head else per_head[(a, hh)] + part
        upd = {a: jnp.where(head_rows[0], per_head[(a, 0)], per_head[(a, 1)]) for a, _ in per_head}
        return upd, c

    def half_tile(a, e, keep=None):
        return (a, e * SB_HALF, SB_HALF, lambda: vt_ref[0, e], keep)

    def diag_vt(g):
        return jnp.concatenate([vt_ref[0, g * per_half + c] for c in range(per_half)], axis=1)

    zero_c = jnp.zeros((1, TS), F32)
    work = []
    for a in range(SB_QS):
        g = i * SB_QS + a
        work.append((a, g * TS, TS, functools.partial(diag_vt, g), strict))
        if a > 0:
            work.append(half_tile(a, g * per_half - 1))
        else:
            exists = lax.broadcasted_iota(jnp.int32, (SB_HALF, TS), 0) < jnp.where(g >= 1, SB_HALF, 0)
            work.append(half_tile(a, jnp.maximum(g * per_half - 1, 0), exists))
    upd, c1 = tiles(work, {(a, hh): zero_c for a in range(SB_QS) for hh in range(2)})
    for a in range(SB_QS):
        acc_ref[a] = upd[a]
        for hh in range(2):
            c_ref[a, hh] = c1[(a, hh)]

    for a in range(SB_QS):
        def cond(state):
            e, alive = state
            return jnp.logical_and(e >= 0, alive > -SB_DEAD_LOG * LOG2E)

        def body(state, a=a):
            e, _ = state
            upd, c_new = tiles([half_tile(a, e)], {(a, hh): c_ref[a, hh] for hh in range(2)})
            acc_ref[a] += upd[a]
            for hh in range(2):
                c_ref[a, hh] = c_new[(a, hh)]
            return e - 1, jnp.maximum(jnp.max(c_new[(a, 0)]), jnp.max(c_new[(a, 1)]))

        alive1 = jnp.maximum(jnp.max(c1[(a, 0)]), jnp.max(c1[(a, 1)]))
        lax.while_loop(cond, body, ((i * SB_QS + a) * per_half - 2, alive1))
        o_ref[0, a * TS:(a + 1) * TS, :] = jnp.transpose(acc_ref[a]).astype(o_ref.dtype)


def _sb_attention(proj3, q_t, v_t):
    b, s, _ = proj3.shape
    pairs = SB_WIDTH // LANES
    return pl.pallas_call(
        _sb_kernel,
        grid=(b, pairs, s // (SB_QS * TS)),
        in_specs=[pl.BlockSpec((1, SB_QS, LANES, TS), lambda bb, hp, i: (bb, i, hp, 0)),
                  pl.BlockSpec((1, s, LANES), lambda bb, hp, i: (bb, 0, hp)),
                  pl.BlockSpec((1, s // SB_HALF, LANES, SB_HALF), lambda bb, hp, i: (bb, 0, hp, 0))],
        out_specs=pl.BlockSpec((1, SB_QS * TS, LANES), lambda bb, hp, i: (bb, i, hp)),
        out_shape=jax.ShapeDtypeStruct((b, s, SB_WIDTH), BF16),
        scratch_shapes=[pltpu.VMEM((SB_QS, LANES, TS), F32), pltpu.VMEM((SB_QS, 2, 1, TS), F32)],
        compiler_params=_cparams(("parallel", "parallel", "arbitrary")),
        name="sb_attention",
    )(q_t, proj3, v_t)


def _diff_kernel(qt_ref, k_ref, vt_ref, bias_ref, lam_ref, g_ref, o_ref, m_ref, acc_ref, sa_ref, sb_ref,
                 *, lambda_init):
    i = pl.program_id(2)
    chains = [(hd, mm) for hd in range(DIFF_HPS) for mm in range(2)]
    row = lax.broadcasted_iota(jnp.int32, (LANES, 1), 0)
    map_rows = (row < HEAD_DIM, row >= HEAD_DIM)
    qts = {}
    for hd in range(DIFF_HPS):
        qt = qt_ref[0, 0, hd * LANES:(hd + 1) * LANES, :]
        for mm in range(2):
            qts[(hd, mm)] = jnp.where(map_rows[mm], qt, jnp.zeros_like(qt))

    m_ref[...] = jnp.full_like(m_ref, NEG)
    acc_ref[...] = jnp.zeros_like(acc_ref)

    def stage(s_ref, j):
        ks = k_ref[0, pl.ds(pl.multiple_of(j * TD, TD), TD), :]
        for hd, mm in chains:
            s_ref[hd, mm] = _dot(ks[:, hd * LANES:(hd + 1) * LANES], qts[(hd, mm)])

    def add_bias(s_ref, hd, mm, near):
        nb = TD // MAX_DISTANCE
        blk = lambda kb, qb: (hd, mm, slice(kb * MAX_DISTANCE, (kb + 1) * MAX_DISTANCE),
                              slice(qb * MAX_DISTANCE, (qb + 1) * MAX_DISTANCE))
        if near == 1:
            s_ref[blk(nb - 1, 0)] += bias_ref[hd, mm, 1]
            return
        for kb in range(nb):
            for qb in range(nb):
                if kb > qb:
                    s_ref[blk(kb, qb)] = jnp.full((MAX_DISTANCE, MAX_DISTANCE), NEG, F32)
                elif kb == qb:
                    s_ref[blk(kb, qb)] += bias_ref[hd, mm, 0]
                elif kb == qb - 1:
                    s_ref[blk(kb, qb)] += bias_ref[hd, mm, 1]

    def absorb(j, s_ref, near=None, offset=None, nxt=None):
        s, alpha = {}, {}
        for hd, mm in chains:
            if near is not None:
                add_bias(s_ref, hd, mm, near)
            sm = s_ref[hd, mm]
            if offset is not None:
                sm = sm + offset
            m_old = m_ref[hd, mm]
            m_new = jnp.maximum(m_old, jnp.max(sm, axis=0, keepdims=True))
            m_ref[hd, mm] = m_new
            alpha[(hd, mm)] = jnp.exp2(m_old - m_new)
            s[(hd, mm)] = sm - m_new
        if nxt is not None:
            stage(*nxt)
        p = {ch: jnp.exp2(s[ch]).astype(BF16) for ch in chains}
        vt = vt_ref[0, j]
        ones = jnp.ones((DIFF_SUM_ROWS, TD), BF16)
        for hd in range(DIFF_HPS):
            vts = jnp.concatenate([vt[hd * LANES:(hd + 1) * LANES], ones], axis=0)
            for mm in range(2):
                acc_ref[hd, mm] = alpha[(hd, mm)] * acc_ref[hd, mm] + _dot(vts, p[(hd, mm)])

    stage(sa_ref, i)
    absorb(i, sa_ref, near=0, nxt=(sb_ref, jnp.maximum(i - 1, 0)))
    absorb(jnp.maximum(i - 1, 0), sb_ref, near=1, offset=jnp.where(i >= 1, 0.0, NEG),
           nxt=(sa_ref, jnp.maximum(i - 2, 0)))

    @pl.when(i >= 2)
    def _():
        def body(u, carry):
            ja = i - 2 - 2 * u
            absorb(ja, sa_ref, nxt=(sb_ref, ja - 1))
            absorb(ja - 1, sb_ref, nxt=(sa_ref, jnp.maximum(ja - 2, 0)))
            return carry

        lax.fori_loop(0, (i - 1) // 2, body, 0)

        @pl.when((i - 1) % 2 == 1)
        def _():
            absorb(0, sa_ref)

    lam = (jnp.exp(jnp.sum(lam_ref[0:1, :] * lam_ref[1:2, :], axis=1, keepdims=True))
           - jnp.exp(jnp.sum(lam_ref[2:3, :] * lam_ref[3:4, :], axis=1, keepdims=True)) + lambda_init)
    for hd in range(DIFF_HPS):
        l0, l1 = acc_ref[hd, 0, LANES:LANES + 1, :], acc_ref[hd, 1, LANES:LANES + 1, :]
        ot = acc_ref[hd, 0, 0:LANES, :] / l0 - lam * (acc_ref[hd, 1, 0:LANES, :] / l1)
        o = jnp.transpose(ot)
        o_ref[0, :, hd * LANES:(hd + 1) * LANES] = (
            _rms(o, g_ref[...], SUBLN_EPS) * (1.0 - lambda_init)).astype(o_ref.dtype)


def _diff_attention(proj3, proj_t, bias, lam_params, subln_g, lambda_init):
    b, s, _ = proj3.shape
    width = DIFF_HPS * LANES
    groups = DIFF_WIDTH // width
    nt = s // TD
    return pl.pallas_call(
        functools.partial(_diff_kernel, lambda_init=lambda_init),
        grid=(b, groups, nt),
        in_specs=[pl.BlockSpec((1, 1, width, TD), lambda bb, h, i: (bb, i, h, 0)),
                  pl.BlockSpec((1, s, width), lambda bb, h, i: (bb, 0, groups + h), pipeline_mode=pl.Buffered(1)),
                  pl.BlockSpec((1, nt, width, TD), lambda bb, h, i: (bb, 0, groups + h, 0),
                               pipeline_mode=pl.Buffered(1)),
                  pl.BlockSpec((DIFF_HPS, 2, 2, MAX_DISTANCE, MAX_DISTANCE), lambda bb, h, i: (h, 0, 0, 0, 0)),
                  pl.BlockSpec((4, HEAD_DIM), lambda bb, h, i: (0, 0)),
                  pl.BlockSpec((1, LANES), lambda bb, h, i: (0, 0))],
        out_specs=pl.BlockSpec((1, TD, width), lambda bb, h, i: (bb, i, h)),
        out_shape=jax.ShapeDtypeStruct((b, s, DIFF_WIDTH), BF16),
        scratch_shapes=[pltpu.VMEM((DIFF_HPS, 2, 1, TD), F32),
                        pltpu.VMEM((DIFF_HPS, 2, LANES + DIFF_SUM_ROWS, TD), F32),
                        pltpu.VMEM((DIFF_HPS, 2, TD, TD), F32), pltpu.VMEM((DIFF_HPS, 2, TD, TD), F32)],
        compiler_params=_cparams(("parallel", "parallel", "arbitrary")),
        name="diff_attention",
    )(proj_t, proj3, proj_t, bias, lam_params, subln_g.reshape(1, LANES))


def _diff_bias(t5_table):
    cols = t5_table.astype(F32)[:, SB_HEADS:]
    cols = (cols - cols[NUM_BUCKETS - 1:NUM_BUCKETS, :]) * LOG2E
    ki = np.arange(MAX_DISTANCE)[:, None]
    qi = np.arange(MAX_DISTANCE)[None, :]
    d0 = qi - ki
    diag = _bias_tile(cols, d0, d0 >= 0)
    prev = _bias_tile(cols, d0 + MAX_DISTANCE, np.ones_like(d0, bool))
    return jnp.stack([diag, prev], axis=1).reshape(DIFF_HEADS, 2, 2, MAX_DISTANCE, MAX_DISTANCE)


def _dil_pair(pp, q_ref, kp_ref, kc_ref, vp_ref, vc_ref, bias_ref, o_ref, qf, kf, vf, acc_s, m_s, l_s):
    t_idx = pl.program_id(2)
    cols = slice(pp * LANES, (pp + 1) * LANES)
    qf[...] = q_ref[0, :, cols].astype(F32)
    kf[0:DT, :] = kp_ref[0, :, cols].astype(F32)
    kf[DT:2 * DT, :] = kc_ref[0, :, cols].astype(F32)
    vf[0:DT, :] = vp_ref[0, :, cols].astype(F32)
    vf[DT:2 * DT, :] = vc_ref[0, :, cols].astype(F32)
    lane = lax.broadcasted_iota(jnp.int32, (1, LANES), 1)
    head0 = lane < HEAD_DIM
    heads = (head0, jnp.logical_not(head0))

    for g, (_, r) in enumerate(DIL_BRANCHES):
        nblk = DT // (r * DQ)
        shift = nblk.bit_length() - 1

        def group(t0, carry, g=g, r=r, nblk=nblk, shift=shift):
            rows, k0s, pre, s, p, mx = {}, {}, {}, {}, {}, {}

            def scores(u):
                t = t0 * DIL_UNROLL + u
                c = lax.shift_right_logical(t, shift)
                n = jnp.bitwise_and(t, nblk - 1)
                q0 = c + n * (DQ * r)
                k0s[u] = DT + q0 - DQ * r
                rows[u] = pl.ds(q0, DQ, stride=r)
                qg = qf[rows[u], :].astype(BF16)
                kg = kf[pl.ds(k0s[u], 2 * DQ, stride=r), :].astype(BF16)
                for hh in range(2):
                    s[(u, hh)] = _dot_nt(jnp.where(heads[hh], qg, jnp.zeros_like(qg)), kg)
                pre[u] = jnp.where(jnp.logical_and(t_idx == 0, n == 0), 1, 0)

            def softmax(u):
                for hh in range(2):
                    sb = s.pop((u, hh)) + bias_ref[pp, hh, g, pre[u]]
                    m = jnp.max(sb, axis=1, keepdims=True)
                    p[(u, hh)] = jnp.exp2(sb - m).astype(BF16)
                    mx[(u, hh)] = jnp.broadcast_to(m, (DQ, LANES))

            def values(u):
                vg = vf[pl.ds(k0s[u], 2 * DQ, stride=r), :].astype(BF16)
                res = [_dot(p.pop((u, hh)), jnp.where(heads[hh], vg, jnp.ones_like(vg))) for hh in range(2)]
                acc_s[g, rows[u], :] = jnp.where(head0, res[0], res[1])
                m_s[g, rows[u], :] = jnp.where(head0, mx.pop((u, 0)), mx.pop((u, 1)))
                l_s[g, rows[u], :] = pltpu.roll(jnp.where(head0, res[1], res[0]), HEAD_DIM, 1)

            for step in range(DIL_UNROLL + 2 * DIL_SKEW):
                if step < DIL_UNROLL:
                    scores(step)
                if 0 <= step - DIL_SKEW < DIL_UNROLL:
                    softmax(step - DIL_SKEW)
                if step >= 2 * DIL_SKEW:
                    values(step - 2 * DIL_SKEW)
            return carry

        lax.fori_loop(0, r * nblk // DIL_UNROLL, group, 0)

    def merge(ci, carry):
        rows = pl.ds(pl.multiple_of(ci * TQ, TQ), TQ)
        ms = [m_s[g, rows, :] for g in range(len(DIL_BRANCHES))]
        m = functools.reduce(jnp.maximum, ms)
        num = jnp.zeros((TQ, LANES), F32)
        den = jnp.zeros((TQ, LANES), F32)
        for g in range(len(DIL_BRANCHES)):
            wgt = jnp.exp2(ms[g] - m)
            num = num + wgt * acc_s[g, rows, :]
            den = den + wgt * l_s[g, rows, :]
        o_ref[0, rows, cols] = (num / den).astype(o_ref.dtype)
        return carry

    lax.fori_loop(0, DT // TQ, merge, 0)


def _dil_kernel(*refs):
    for pp in range(DIL_PAIRS):
        _dil_pair(pp, *refs)


def _dil_bias(t5_table):
    qi = np.arange(DQ)[:, None]
    ki = np.arange(2 * DQ)[None, :] - DQ
    steps = qi - ki
    valid = (steps >= 0) & (steps <= DIL_WINDOW)
    table = t5_table.astype(F32) * LOG2E
    tiles = [jnp.stack([_bias_tile(table, steps * r, v) for v in (valid, valid & (ki >= 0))], axis=1)
             for _, r in DIL_BRANCHES]
    return jnp.stack(tiles, axis=1).reshape(DIL_HEADS // 2, 2, len(DIL_BRANCHES), 2, DQ, 2 * DQ)


def _dilated_attention(proj3, t5_table):
    b, s, _ = proj3.shape
    pairs = DIL_HEADS // 2
    bias = _dil_bias(t5_table)
    groups = pairs // DIL_PAIRS
    cur = lambda off: (lambda bb, hp, t: (bb, t, off * groups + hp))
    prev = lambda off: (lambda bb, hp, t: (bb, jnp.maximum(t - 1, 0), off * groups + hp))
    tile = lambda imap: pl.BlockSpec((1, DT, DIL_PAIRS * LANES), imap)
    slab = lambda rows: pltpu.VMEM((rows, LANES), F32)
    per_branch = pltpu.VMEM((len(DIL_BRANCHES), DT, LANES), F32)
    return pl.pallas_call(
        _dil_kernel,
        grid=(b, groups, s // DT),
        in_specs=[tile(cur(0)), tile(prev(1)), tile(cur(1)), tile(prev(2)), tile(cur(2)),
                  pl.BlockSpec((DIL_PAIRS, 2, len(DIL_BRANCHES), 2, DQ, 2 * DQ), lambda bb, hp, t: (hp, 0, 0, 0, 0, 0))],
        out_specs=tile(cur(0)),
        out_shape=jax.ShapeDtypeStruct((b, s, D_MODEL), BF16),
        scratch_shapes=[slab(DT), slab(2 * DT), slab(2 * DT), per_branch, per_branch, per_branch],
        compiler_params=_cparams(("parallel", "parallel", "arbitrary")),
        name="dilated_attention",
    )(proj3, proj3, proj3, proj3, proj3, bias)


def _post_kernel(*refs, n_o, final):
    h_ref = refs[0]
    o_refs = refs[1:1 + n_o]
    (wo_ref, p_ref, g_mlp_ref, w_up_ref, w_down_ref, g_ple_ref, w_gate_ref, w_proj_ref, g_fin_ref,
     out_ref, acc_ref) = refs[1 + n_o:]
    o = o_refs[0][...] if n_o == 1 else jnp.concatenate([r[...] for r in o_refs], axis=1)
    h = h_ref[...] + _dot(o, wo_ref[...])
    hn = _rms(h, g_mlp_ref[...], NORM_EPS).astype(BF16)
    acc_ref[...] = h
    for c in range(D_FF // FF_CHUNK):
        u = jnp.maximum(_dot(hn, w_up_ref[:, c * FF_CHUNK:(c + 1) * FF_CHUNK]), 0.0)
        acc_ref[...] += _dot((u * u).astype(BF16), w_down_ref[c * FF_CHUNK:(c + 1) * FF_CHUNK, :])
    h = acc_ref[...]
    gate = jax.nn.sigmoid(_dot(_rms(h, g_ple_ref[...], NORM_EPS).astype(BF16), w_gate_ref[...]))
    h = h + _dot(p_ref[...].astype(BF16), w_proj_ref[...]) * gate
    if final:
        h = _rms(h, g_fin_ref[...], NORM_EPS)
    out_ref[...] = h


def _post(h2d, o_list, w_out, p2d, g_mlp, w_up, w_down, g_ple, w_gate, w_proj, g_fin, final):
    m, d = h2d.shape
    n_o = len(o_list)
    row = lambda width: pl.BlockSpec((TM_POST, width), lambda i: (i, 0))
    const = lambda a: pl.BlockSpec(a.shape, lambda i: (0,) * a.ndim, pipeline_mode=pl.Buffered(1))
    vec = lambda g: g.reshape(1, d).astype(F32)
    weights = [w_out, vec(g_mlp), w_up, w_down, vec(g_ple), w_gate, w_proj, vec(g_fin)]
    return pl.pallas_call(
        functools.partial(_post_kernel, n_o=n_o, final=final),
        grid=(m // TM_POST,),
        in_specs=([row(d)] + [row(o.shape[1]) for o in o_list] + [const(weights[0])]
                  + [row(p2d.shape[1])] + [const(w) for w in weights[1:]]),
        out_specs=row(d),
        out_shape=jax.ShapeDtypeStruct((m, d), F32),
        scratch_shapes=[pltpu.VMEM((TM_POST, d), F32)],
        compiler_params=_cparams(("parallel",)),
        name="out_mlp_ple",
    )(h2d, *o_list, weights[0], p2d, *weights[1:])


def _scale_cols(w_in, col_scale):
    return (w_in * jnp.asarray(col_scale, F32)[None, :]).astype(BF16)


def kernel(x, p, t5_table, w_in_even, w_out_even, lambda_q1, lambda_k1, lambda_q2, lambda_k2, subln_g,
           w_in_odd, w_out_odd, norm_mix_g, norm_mlp_g, w_mlp_up, w_mlp_down, norm_ple_g, w_ple_gate,
           w_ple_proj, final_norm_g):
    b, s, d = x.shape
    depth = p.shape[0]
    assert d == D_MODEL and s % DT == 0 and (b * s) % TM_POST == 0
    assert all(w == DIL_WINDOW * r and w <= DT for w, r in DIL_BRANCHES)
    col = np.arange(3 * D_MODEL)
    qk_scale = HEAD_DIM ** -0.5
    even_q = (col < SB_WIDTH) | ((col >= 3 * SB_WIDTH) & (col < 3 * SB_WIDTH + DIFF_WIDTH))
    even_scale = np.where(even_q, qk_scale * LOG2E, 1.0)
    odd_scale = np.where(col < DIL_HEADS * HEAD_DIM, qk_scale * LOG2E, 1.0)
    h = x.reshape(b * s, d)
    for i in range(depth):
        if i % 2 == 0:
            e = i // 2
            w_in = _scale_cols(w_in_even[e], even_scale)
            qa, ka, va, qb, kb, vb = (w_in[:, lo:lo + SB_WIDTH] for lo in range(0, 6 * SB_WIDTH, SB_WIDTH))
            proj, qa_t, va_t, t_d = _norm_matmul(
                h, norm_mix_g[i], jnp.concatenate([ka, kb], axis=1),
                ((qa.T, TS), (va.T, SB_HALF), (jnp.concatenate([qb, vb], axis=1).T, TD)))
            proj = proj.reshape(b, s, -1)
            qa_t = qa_t.reshape(b, s // TS, SB_WIDTH, TS)
            va_t = va_t.reshape(b, s // SB_HALF, SB_WIDTH, SB_HALF)
            t_d = t_d.reshape(b, s // TD, 2 * DIFF_WIDTH, TD)
            lambda_init = 0.8 - 0.6 * math.exp(-0.3 * i)
            lam_params = jnp.stack([lambda_q1[e], lambda_k1[e], lambda_q2[e], lambda_k2[e]]).astype(F32)
            o_sb = _sb_attention(proj, qa_t, va_t)
            o_d = _diff_attention(proj, t_d, _diff_bias(t5_table), lam_params, subln_g[e].astype(F32),
                                  lambda_init)
            w_out = w_out_even[e].astype(BF16)
            o_list = [o_sb.reshape(b * s, SB_WIDTH), o_d.reshape(b * s, DIFF_WIDTH)]
        else:
            o = i // 2
            (proj,) = _norm_matmul(h, norm_mix_g[i], _scale_cols(w_in_odd[o], odd_scale))
            proj = proj.reshape(b, s, -1)
            o_list = [_dilated_attention(proj, t5_table).reshape(b * s, d)]
            w_out = w_out_odd[o].astype(BF16)
        h = _post(h, o_list, w_out, p[i].reshape(b * s, PLE_DIM), norm_mlp_g[i],
                  w_mlp_up[i].astype(BF16), w_mlp_down[i].astype(BF16), norm_ple_g[i],
                  w_ple_gate[i].astype(BF16), w_ple_proj[i].astype(BF16), final_norm_g,
                  final=(i == depth - 1))
    return h.reshape(b, s, d)
```
